```python
import math
import jax, jax.numpy as jnp
from jax import lax
import numpy as np

D_MODEL = 1024
BATCH = 8
SEQ = 2048
DEPTH = 2
DEC_BATCH = 32
DEC_SEQ = 1
PAST_LEN = 16384
PAGE_SIZE = 128

GROUP_W = D_MODEL // 4
N_PROJ = 9 * GROUP_W
CONV_A_WIDTH = 31
HEAD_DIM = 64
N_HEADS_B = GROUP_W // HEAD_DIM
DIL_CONFIGS = ((128, 1), (512, 4), (2048, 16))
MAX_WINDOW = max(w for w, _ in DIL_CONFIGS)
Q_BLOCK = 128
ATTN_SCALE = 1.0 / math.sqrt(HEAD_DIM)
CONV_C_WIDTH = 3
POOL_WINDOWS = (2, 4, 8, 16)
POOL_GROUP = GROUP_W // len(POOL_WINDOWS)
POOL_STATE = max(POOL_WINDOWS) - 1
FF_DIM = -(-8 * D_MODEL // (3 * 256)) * 256
EPS = 1e-6
NEG = -1e30

kernel_name = "hybrid_conformer_dilattn_shortconv_pool_step"


def rmsnorm(x, g):
    xf = x.astype(jnp.float32)
    y = xf * lax.rsqrt(jnp.mean(xf * xf, axis=-1, keepdims=True) + EPS)
    return (y * g.astype(jnp.float32)).astype(x.dtype)


def layernorm(x, g, b):
    xf = x.astype(jnp.float32)
    mu = jnp.mean(xf, axis=-1, keepdims=True)
    xc = xf - mu
    y = xc * lax.rsqrt(jnp.mean(xc * xc, axis=-1, keepdims=True) + EPS)
    return (y * g.astype(jnp.float32) + b.astype(jnp.float32)).astype(x.dtype)


def depthwise_causal_conv(xp, w):
    c = xp.shape[-1]
    return lax.conv_general_dilated(xp, w[:, None, :].astype(xp.dtype), window_strides=(1,), padding="VALID",
                                    dimension_numbers=("NWC", "WIO", "NWC"), feature_group_count=c)


def multi_pool(up, pos0):
    n, tot, c = up.shape
    t = tot - POOL_STATE
    uf = up.astype(jnp.float32)
    cs = jnp.concatenate([jnp.zeros((n, 1, c), jnp.float32), jnp.cumsum(uf, axis=1)], axis=1)
    pos = pos0 + jnp.arange(t)
    p1 = POOL_STATE + 1
    outs = []
    for g, w in enumerate(POOL_WINDOWS):
        sl = slice(g * POOL_GROUP, (g + 1) * POOL_GROUP)
        s = cs[:, p1:p1 + t, sl] - cs[:, p1 - w:p1 - w + t, sl]
        cnt = jnp.minimum(w, pos + 1).astype(jnp.float32)[None, :, None]
        outs.append(s / cnt)
    return (jnp.concatenate(outs, axis=-1) - uf[:, POOL_STATE:]).astype(up.dtype)


def band_attn(q, k, v, n_back):
    n, l, h, hd = q.shape
    nb = -(-l // Q_BLOCK)
    lp = nb * Q_BLOCK
    qb = jnp.pad(q, [(0, 0), (0, lp - l), (0, 0), (0, 0)]).reshape(n, nb, Q_BLOCK, h, hd)

    def windows(a):
        ab = jnp.pad(a, [(0, 0), (Q_BLOCK, lp - l), (0, 0), (0, 0)]).reshape(n, nb + 1, Q_BLOCK, h, hd)
        return jnp.concatenate([ab[:, :-1], ab[:, 1:]], axis=2)

    kw, vw = windows(k), windows(v)
    s = jnp.einsum("nbqhd,nbkhd->nbhqk", qb.astype(jnp.float32), kw.astype(jnp.float32)) * ATTN_SCALE
    qi = jnp.arange(Q_BLOCK)[:, None]
    ki = jnp.arange(2 * Q_BLOCK)[None, :] - Q_BLOCK
    rel = qi - ki
    kpos = jnp.arange(nb)[:, None, None] * Q_BLOCK + ki[None]
    mask = (rel >= 0)[None] & (rel <= n_back)[None] & (kpos >= 0)
    s = jnp.where(mask[None, :, None], s, NEG)
    m = jnp.max(s, axis=-1, keepdims=True)
    p = jnp.exp(s - m)
    den = jnp.sum(p, axis=-1)
    o = jnp.einsum("nbhqk,nbkhd->nbqhd", p, vw.astype(jnp.float32))
    den_t = jnp.moveaxis(den, 2, 3)
    lse_t = jnp.moveaxis(m[..., 0] + jnp.log(den), 2, 3)
    o = o / den_t[..., None]
    return o.reshape(n, lp, h, hd)[:, :l], lse_t.reshape(n, lp, h)[:, :l]


def combine_by_denominator(outs, lses):
    w = jax.nn.softmax(jnp.stack(lses, axis=0), axis=0)
    return jnp.einsum("cnth,cnthd->nthd", w, jnp.stack(outs, axis=0))


def dilated_attn_prompt(q, k, v):
    n, s, h, hd = q.shape
    outs, lses = [], []
    for window, dil in DIL_CONFIGS:
        l = s // dil

        def to_sub(a):
            return a.reshape(n, l, dil, h, hd).transpose(0, 2, 1, 3, 4).reshape(n * dil, l, h, hd)

        o, lse = band_attn(to_sub(q), to_sub(k), to_sub(v), window // dil)
        outs.append(o.reshape(n, dil, l, h, hd).transpose(0, 2, 1, 3, 4).reshape(n, s, h, hd))
        lses.append(lse.reshape(n, dil, l, h).transpose(0, 2, 1, 3).reshape(n, s, h))
    return combine_by_denominator(outs, lses)


def dilated_attn_sample(q, kc, vc, buf_len):
    t = q.shape[1]
    qf = q.astype(jnp.float32)
    outs, lses = [], []
    for window, dil in DIL_CONFIGS:
        nk = window // dil + 1
        idx = buf_len + jnp.arange(t)[:, None] - jnp.arange(nk)[None, :] * dil
        valid = idx >= 0
        idc = jnp.maximum(idx, 0)
        kg = jnp.take(kc, idc, axis=1).astype(jnp.float32)
        vg = jnp.take(vc, idc, axis=1).astype(jnp.float32)
        s = jnp.einsum("nthd,ntkhd->nthk", qf, kg) * ATTN_SCALE
        s = jnp.where(valid[None, :, None, :], s, NEG)
        m = jnp.max(s, axis=-1, keepdims=True)
        p = jnp.exp(s - m)
        den = jnp.sum(p, axis=-1)
        o = jnp.einsum("nthk,ntkhd->nthd", p, vg) / den[..., None]
        outs.append(o)
        lses.append(m[..., 0] + jnp.log(den))
    return combine_by_denominator(outs, lses)


def hybrid_mixer(h, pos0, kbuf, vbuf, abuf, cbuf, pbuf, w_in, conv_a_w, conv_a_b, ln_a_g, ln_a_b,
                 conv_c_w, pool_w, pool_scale, w_out):
    n, t, _ = h.shape
    proj = h @ w_in
    a_val, a_gate, q, k, v, c_x, c_b, c_c, d_u = jnp.split(proj, 9, axis=-1)
    ga = a_val * jax.nn.sigmoid(a_gate)
    ap = jnp.concatenate([abuf, ga], axis=1)
    ya = depthwise_causal_conv(ap, conv_a_w) + conv_a_b
    ya = jax.nn.silu(layernorm(ya, ln_a_g, ln_a_b))
    new_a = ap[:, -(CONV_A_WIDTH - 1):]
    q = q.reshape(n, t, N_HEADS_B, HEAD_DIM)
    k = k.reshape(n, t, N_HEADS_B, HEAD_DIM)
    v = v.reshape(n, t, N_HEADS_B, HEAD_DIM)
    if kbuf is None:
        ob = dilated_attn_prompt(q, k, v)
        keep = min(MAX_WINDOW, t)
        new_k, new_v = k[:, t - keep:], v[:, t - keep:]
    else:
        buf_len = kbuf.shape[1]
        kc = jnp.concatenate([kbuf, k], axis=1)
        vc = jnp.concatenate([vbuf, v], axis=1)
        ob = dilated_attn_sample(q, kc, vc, buf_len)
        new_k, new_v = kc[:, -buf_len:], vc[:, -buf_len:]
    ob = ob.reshape(n, t, GROUP_W).astype(h.dtype)
    cp = jnp.concatenate([cbuf, c_c * c_x], axis=1)
    yc = c_b * depthwise_causal_conv(cp, conv_c_w)
    new_c = cp[:, -(CONV_C_WIDTH - 1):]
    pp = jnp.concatenate([pbuf, d_u], axis=1)
    yd = multi_pool(pp, pos0).reshape(n, t, len(POOL_WINDOWS), POOL_GROUP)
    yd = jnp.einsum("ntgc,gce->ntge", yd, pool_w).reshape(n, t, GROUP_W) * pool_scale
    new_p = pp[:, -POOL_STATE:]
    mix = jnp.concatenate([ya, ob, yc, yd], axis=-1) @ w_out
    return mix, (new_k, new_v, new_a, new_c, new_p)


def swiglu(h, w_gu, w_down):
    g, u = jnp.split(h @ w_gu, 2, axis=-1)
    return (jax.nn.silu(g) * u) @ w_down


def trunk(x, pos0, caches, w_in, conv_a_w, conv_a_b, ln_a_g, ln_a_b, conv_c_w, pool_w, pool_scale,
          w_out, norm1_g, norm2_g, w_gu, w_down, final_g):
    n = x.shape[0]
    new = ([], [], [], [], [])
    for l in range(DEPTH):
        if caches is None:
            kb = vb = None
            ab = jnp.zeros((n, CONV_A_WIDTH - 1, GROUP_W), x.dtype)
            cb = jnp.zeros((n, CONV_C_WIDTH - 1, GROUP_W), x.dtype)
            pb = jnp.zeros((n, POOL_STATE, GROUP_W), x.dtype)
        else:
            kb, vb, ab, cb, pb = caches[0][l], caches[1][l], caches[2][l], caches[3][l], caches[4][l]
        h = rmsnorm(x, norm1_g[l])
        mix, st = hybrid_mixer(h, pos0, kb, vb, ab, cb, pb, w_in[l], conv_a_w[l], conv_a_b[l], ln_a_g[l],
                               ln_a_b[l], conv_c_w[l], pool_w[l], pool_scale[l], w_out[l])
        x = x + mix
        x = x + swiglu(rmsnorm(x, norm2_g[l]), w_gu[l], w_down[l])
        for lst, s in zip(new, st):
            lst.append(s)
    y = rmsnorm(x, final_g)
    return y, [jnp.stack(s, axis=0) for s in new]


def setup_inputs(seed: int = 0) -> dict:
    key = jax.random.key(seed)
    ks = jax.random.split(key, 24)
    f32 = jnp.float32
    buf_s = min(MAX_WINDOW, PAST_LEN)
    nrm = lambda k, shape, sc: jax.random.normal(k, shape, f32) * sc
    return {
        "x_prompt": nrm(ks[0], (BATCH, SEQ, D_MODEL), 1.0),
        "x_sample": nrm(ks[1], (DEC_BATCH, DEC_SEQ, D_MODEL), 1.0),
        "cache_win_k": nrm(ks[2], (DEPTH, DEC_BATCH, buf_s, N_HEADS_B, HEAD_DIM), 1.0),
        "cache_win_v": nrm(ks[3], (DEPTH, DEC_BATCH, buf_s, N_HEADS_B, HEAD_DIM), 1.0),
        "state_conv_a": nrm(ks[4], (DEPTH, DEC_BATCH, CONV_A_WIDTH - 1, GROUP_W), 0.5),
        "state_conv_c": nrm(ks[5], (DEPTH, DEC_BATCH, CONV_C_WIDTH - 1, GROUP_W), 1.0),
        "state_pool": nrm(ks[6], (DEPTH, DEC_BATCH, POOL_STATE, GROUP_W), 1.0),
        "w_in": nrm(ks[7], (DEPTH, D_MODEL, N_PROJ), D_MODEL ** -0.5),
        "conv_a_w": nrm(ks[8], (DEPTH, CONV_A_WIDTH, GROUP_W), CONV_A_WIDTH ** -0.5),
        "conv_a_b": nrm(ks[9], (DEPTH, GROUP_W), 0.02),
        "ln_a_g": 1.0 + nrm(ks[10], (DEPTH, GROUP_W), 0.02),
        "ln_a_b": nrm(ks[11], (DEPTH, GROUP_W), 0.02),
        "conv_c_w": nrm(ks[12], (DEPTH, CONV_C_WIDTH, GROUP_W), CONV_C_WIDTH ** -0.5),
        "pool_w": nrm(ks[13], (DEPTH, len(POOL_WINDOWS), POOL_GROUP, POOL_GROUP), POOL_GROUP ** -0.5),
        "pool_scale": 1.0 + nrm(ks[14], (DEPTH, GROUP_W), 0.02),
        "w_out": nrm(ks[15], (DEPTH, D_MODEL, D_MODEL), D_MODEL ** -0.5),
        "norm1_g": 1.0 + nrm(ks[16], (DEPTH, D_MODEL), 0.02),
        "norm2_g": 1.0 + nrm(ks[17], (DEPTH, D_MODEL), 0.02),
        "w_gu": nrm(ks[18], (DEPTH, D_MODEL, 2 * FF_DIM), D_MODEL ** -0.5),
        "w_down": nrm(ks[19], (DEPTH, FF_DIM, D_MODEL), FF_DIM ** -0.5),
        "final_g": 1.0 + nrm(ks[20], (D_MODEL,), 0.02),
    }


def reference(x_prompt, x_sample, cache_win_k, cache_win_v, state_conv_a, state_conv_c, state_pool,
              w_in, conv_a_w, conv_a_b, ln_a_g, ln_a_b, conv_c_w, pool_w, pool_scale, w_out,
              norm1_g, norm2_g, w_gu, w_down, final_g):
    y_prompt, st_p = trunk(x_prompt, 0, None, w_in, conv_a_w, conv_a_b, ln_a_g, ln_a_b, conv_c_w, pool_w,
                           pool_scale, w_out, norm1_g, norm2_g, w_gu, w_down, final_g)
    caches = (cache_win_k, cache_win_v, state_conv_a, state_conv_c, state_pool)
    y_sample, st_s = trunk(x_sample, PAST_LEN, caches, w_in, conv_a_w, conv_a_b, ln_a_g, ln_a_b, conv_c_w,
                           pool_w, pool_scale, w_out, norm1_g, norm2_g, w_gu, w_down, final_g)
    k_p, v_p, a_p, c_p, p_p = st_p
    k_s, v_s, a_s, c_s, p_s = st_s
    return (y_prompt, y_sample, k_p, v_p, a_p, c_p, p_p, k_s, v_s, a_s, c_s, p_s)
```

```python
import functools
import math

import jax
import jax.numpy as jnp
from jax import lax
from jax.experimental import pallas as pl
from jax.experimental.pallas import tpu as pltpu

F32 = jnp.float32
BF16 = jnp.bfloat16

GROUP_W = 256
HEAD_DIM = 64
N_HEADS = GROUP_W // HEAD_DIM
CONV_A_WIDTH = 31
CONV_C_WIDTH = 3
POOL_WINDOWS = (2, 4, 8, 16)
POOL_STATE = max(POOL_WINDOWS) - 1
DIL_CONFIGS = ((128, 1), (512, 4), (2048, 16))
Q_BLOCK = 128
PAST_LEN = 16384
ATTN_SCALE = 1.0 / math.sqrt(HEAD_DIM)
EPS = 1e-6
NEG = -1e30

VMEM_LIMIT_BYTES = 56 * 1024 * 1024


def _cparams(*sem):
    return pltpu.CompilerParams(dimension_semantics=sem, vmem_limit_bytes=VMEM_LIMIT_BYTES)


def _rmsnorm(x, g):
    return x * lax.rsqrt(jnp.mean(x * x, axis=-1, keepdims=True) + EPS) * g


def _layernorm(x, g, b):
    mu = jnp.mean(x, axis=-1, keepdims=True)
    xc = x - mu
    return xc * lax.rsqrt(jnp.mean(xc * xc, axis=-1, keepdims=True) + EPS) * g + b


def _dot(a, b):
    return jnp.dot(a, b, preferred_element_type=F32)


def _head_of_lane(shape, dim):
    return lax.broadcasted_iota(jnp.int32, shape, dim) // HEAD_DIM


def _inproj_kernel(x_ref, g_ref, w_ref, ag_ref, q_ref, k_ref, v_ref, cd_ref):
    h = _rmsnorm(x_ref[...], g_ref[...]).astype(BF16)
    gw = GROUP_W
    ag_ref[...] = _dot(h, w_ref[:, 0:2 * gw])
    q_ref[...] = _dot(h, w_ref[:, 2 * gw:3 * gw]) * ATTN_SCALE
    k_ref[...] = _dot(h, w_ref[:, 3 * gw:4 * gw])
    v_ref[...] = _dot(h, w_ref[:, 4 * gw:5 * gw])
    cd_ref[...] = _dot(h, w_ref[:, 5 * gw:9 * gw])


def _inproj(x, g, w_in, tm):
    t, d = x.shape
    gw = GROUP_W
    row = lambda w: pl.BlockSpec((tm, w), lambda i: (i, 0))
    full = lambda a: pl.BlockSpec(a.shape, lambda i: (0,) * a.ndim)
    return pl.pallas_call(
        _inproj_kernel,
        grid=(t // tm,),
        in_specs=[row(d), full(g), full(w_in)],
        out_specs=[row(2 * gw), row(gw), row(gw), row(gw), row(4 * gw)],
        out_shape=[jax.ShapeDtypeStruct((t, w), F32) for w in (2 * gw, gw, gw, gw, 4 * gw)],
        compiler_params=_cparams("parallel"),
        name="inproj",
    )(x, g, w_in)


def _softmax_pv(s, vwin, head_rows):
    r = s.shape[0] // N_HEADS
    m = jnp.max(s, axis=-1, keepdims=True)
    p = jnp.exp(s - m)
    den = jnp.sum(p, axis=-1, keepdims=True)
    pv = _dot(p.astype(BF16), vwin) / den
    lse = m + jnp.log(den)
    o = jnp.zeros((r, GROUP_W), F32)
    l = jnp.zeros((r, GROUP_W), F32)
    for h in range(N_HEADS):
        sel = head_rows == h
        o = jnp.where(sel, pv[h * r:(h + 1) * r], o)
        l = jnp.where(sel, lse[h * r:(h + 1) * r], l)
    return o, l


def _attn_kernel(q_ref, k_ref, v_ref, ob_ref, q_scr, k_scr, v_scr, o_scr, l_scr, *, seq):
    qb = Q_BLOCK
    hw = GROUP_W // 2
    for src, dst in ((q_ref, q_scr), (k_ref, k_scr), (v_ref, v_scr)):
        for half in range(2):
            dst[half] = src[0, :, half * hw:(half + 1) * hw]

    head_q = _head_of_lane((qb, GROUP_W), 1)
    qi = lax.broadcasted_iota(jnp.int32, (N_HEADS * qb, 2 * qb), 0) % qb
    kk = lax.broadcasted_iota(jnp.int32, (N_HEADS * qb, 2 * qb), 1)
    in_prev = kk < qb
    band = jnp.logical_or(jnp.logical_and(in_prev, kk >= qi), jnp.logical_and(kk >= qb, kk - qb <= qi))

    for c, (_, dil) in enumerate(DIL_CONFIGS):
        n_blocks = seq // (dil * qb)

        def unit(u, carry, c=c, dil=dil, n_blocks=n_blocks):
            r = u // n_blocks
            b = u % n_blocks
            start = r + b * (qb * dil)
            start_prev = jnp.maximum(start - qb * dil, r)
            if dil == 1:
                start, start_prev = pl.multiple_of(start, qb), pl.multiple_of(start_prev, qb)
            rows = lambda s0: pl.ds(s0, qb, stride=dil) if dil > 1 else pl.ds(s0, qb)
            load = lambda scr, s0: jnp.concatenate([scr[0, rows(s0), :], scr[1, rows(s0), :]], axis=1)
            q = load(q_scr, start)
            kwin = jnp.concatenate([load(k_scr, start_prev), load(k_scr, start)], axis=0).astype(BF16)
            vwin = jnp.concatenate([load(v_scr, start_prev), load(v_scr, start)], axis=0).astype(BF16)
            qs = jnp.concatenate([jnp.where(head_q == h, q, 0.0) for h in range(N_HEADS)], axis=0).astype(BF16)
            s = lax.dot_general(qs, kwin, (((1,), (1,)), ((), ())), preferred_element_type=F32)
            valid = jnp.logical_and(band, jnp.logical_or(kk >= qb, b > 0))
            s = jnp.where(valid, s, NEG)
            o, l = _softmax_pv(s, vwin, head_q)
            for half in range(2):
                o_scr[c, half, rows(start), :] = o[:, half * hw:(half + 1) * hw]
                l_scr[c, half, rows(start), :] = l[:, half * hw:(half + 1) * hw]
            return carry

        lax.fori_loop(0, dil * n_blocks, unit, 0)

    for half in range(2):
        l0, l1, l2 = l_scr[0, half], l_scr[1, half], l_scr[2, half]
        m = jnp.maximum(jnp.maximum(l0, l1), l2)
        e0, e1, e2 = jnp.exp(l0 - m), jnp.exp(l1 - m), jnp.exp(l2 - m)
        ob_ref[0, :, half * hw:(half + 1) * hw] = (
            (e0 * o_scr[0, half] + e1 * o_scr[1, half] + e2 * o_scr[2, half]) / (e0 + e1 + e2))


def _attn_prompt(q, k, v):
    n, s, w = q.shape
    blk = pl.BlockSpec((1, s, w), lambda i: (i, 0, 0))
    return pl.pallas_call(
        functools.partial(_attn_kernel, seq=s),
        grid=(n,),
        in_specs=[blk, blk, blk],
        out_specs=blk,
        out_shape=jax.ShapeDtypeStruct((n, s, w), F32),
        scratch_shapes=[pltpu.VMEM((2, s, w // 2), F32)] * 3 + [pltpu.VMEM((len(DIL_CONFIGS), 2, s, w // 2), F32)] * 2,
        compiler_params=_cparams("parallel"),
        name="attn_prompt",
    )(q, k, v)


def _pool_select(s2, s4, s8, s16):
    grp = _head_of_lane(s2.shape, s2.ndim - 1)
    return jnp.where(grp == 0, s2, jnp.where(grp == 1, s4, jnp.where(grp == 2, s8, s16)))


def _pool_window_lanes(shape):
    grp = _head_of_lane(shape, len(shape) - 1)
    return jnp.where(grp == 0, 2, jnp.where(grp == 1, 4, jnp.where(grp == 2, 8, 16)))


def _out_proj(ya, ob, yc, yd, w_out_ref):
    gw = GROUP_W
    acc = _dot(ya.astype(BF16), w_out_ref[0:gw, :])
    acc += _dot(ob.astype(BF16), w_out_ref[gw:2 * gw, :])
    acc += _dot(yc.astype(BF16), w_out_ref[2 * gw:3 * gw, :])
    acc += _dot(yd.astype(BF16), w_out_ref[3 * gw:4 * gw, :])
    return acc


def _ffn(x, g2, w_gu_ref, w_down_ref, ff_chunk):
    ff = w_down_ref.shape[0]
    h = _rmsnorm(x, g2).astype(BF16)
    acc = jnp.zeros_like(x)
    for c in range(ff // ff_chunk):
        lo = c * ff_chunk
        g = _dot(h, w_gu_ref[:, lo:lo + ff_chunk])
        u = _dot(h, w_gu_ref[:, ff + lo:ff + lo + ff_chunk])
        acc += _dot((jax.nn.silu(g) * u).astype(BF16), w_down_ref[lo:lo + ff_chunk, :])
    return x + acc


A_PAD = 32
C_PAD = 8
P_PAD = 16


def _mix_kernel(ag_ref, cd_ref, ob_ref, x_ref, caw_ref, cab_ref, lng_ref, lnb_ref, ccw_ref, pw_ref, ps_ref, wo_ref,
                x1_ref, sta_ref, stc_ref, stp_ref, abuf, cbuf, pbuf, *, tm):
    gw = GROUP_W
    j = pl.program_id(1)

    @pl.when(j == 0)
    def _():
        abuf[0:A_PAD, :] = jnp.zeros((A_PAD, gw), F32)
        cbuf[0:C_PAD, :] = jnp.zeros((C_PAD, gw), F32)
        pbuf[0:P_PAD, :] = jnp.zeros((P_PAD, gw), F32)

    abuf[A_PAD:A_PAD + tm, :] = ag_ref[:, 0:gw] * jax.nn.sigmoid(ag_ref[:, gw:2 * gw])
    off = A_PAD - (CONV_A_WIDTH - 1)
    acc = jnp.zeros((tm, gw), F32)
    for t in range(CONV_A_WIDTH):
        acc += caw_ref[t:t + 1, :] * abuf[off + t:off + t + tm, :]
    ya = jax.nn.silu(_layernorm(acc + cab_ref[...], lng_ref[...], lnb_ref[...]))

    cbuf[C_PAD:C_PAD + tm, :] = cd_ref[:, 2 * gw:3 * gw] * cd_ref[:, 0:gw]
    off = C_PAD - (CONV_C_WIDTH - 1)
    acc = jnp.zeros((tm, gw), F32)
    for t in range(CONV_C_WIDTH):
        acc += ccw_ref[t:t + 1, :] * cbuf[off + t:off + t + tm, :]
    yc = cd_ref[:, gw:2 * gw] * acc

    u = cd_ref[:, 3 * gw:4 * gw]
    pbuf[P_PAD:P_PAD + tm, :] = u
    back = lambda i: pbuf[P_PAD - i:P_PAD - i + tm, :]
    s2 = u + back(1)
    s4 = s2 + back(2) + back(3)
    s8 = s4 + back(4) + back(5) + back(6) + back(7)
    s16 = s8
    for i in range(8, 16):
        s16 = s16 + back(i)
    pos = j * tm + lax.broadcasted_iota(jnp.int32, (tm, gw), 0)
    cnt = jnp.minimum(_pool_window_lanes((tm, gw)), pos + 1).astype(F32)
    yd = _dot((_pool_select(s2, s4, s8, s16) / cnt - u).astype(BF16), pw_ref[...]) * ps_ref[...]

    x1_ref[...] = x_ref[...] + _out_proj(ya, ob_ref[...], yc, yd, wo_ref)

    @pl.when(j == pl.num_programs(1) - 1)
    def _():
        sta_ref[0] = abuf[A_PAD + tm - (CONV_A_WIDTH - 1):A_PAD + tm, :]
        stc_ref[0] = cbuf[C_PAD + tm - (CONV_C_WIDTH - 1):C_PAD + tm, :]
        stp_ref[0] = pbuf[P_PAD + tm - POOL_STATE:P_PAD + tm, :]

    abuf[0:A_PAD, :] = abuf[tm:tm + A_PAD, :]
    cbuf[0:C_PAD, :] = cbuf[tm:tm + C_PAD, :]
    pbuf[0:P_PAD, :] = pbuf[tm:tm + P_PAD, :]


def _mix_prompt(ag, cd, ob, x, caw, cab, lng, lnb, ccw, pool_bd, ps, w_out, n, tm):
    t, d = x.shape
    gw = GROUP_W
    spb = t // n // tm
    row = lambda w: pl.BlockSpec((tm, w), lambda i, j: (i * spb + j, 0))
    full = lambda a: pl.BlockSpec(a.shape, lambda i, j: (0,) * a.ndim)
    st = lambda r: pl.BlockSpec((1, r, gw), lambda i, j: (i, 0, 0))
    st_rows = (CONV_A_WIDTH - 1, CONV_C_WIDTH - 1, POOL_STATE)
    return pl.pallas_call(
        functools.partial(_mix_kernel, tm=tm),
        grid=(n, spb),
        in_specs=[row(2 * gw), row(4 * gw), row(gw), row(d)] + [full(a) for a in (caw, cab, lng, lnb, ccw, pool_bd, ps, w_out)],
        out_specs=[row(d)] + [st(r) for r in st_rows],
        out_shape=[jax.ShapeDtypeStruct((t, d), F32)] + [jax.ShapeDtypeStruct((n, r, gw), F32) for r in st_rows],
        scratch_shapes=[pltpu.VMEM((A_PAD + tm, gw), F32), pltpu.VMEM((C_PAD + tm, gw), F32),
                        pltpu.VMEM((P_PAD + tm, gw), F32)],
        compiler_params=_cparams("parallel", "arbitrary"),
        name="mix_prompt",
    )(ag, cd, ob, x, caw, cab, lng, lnb, ccw, pool_bd, ps, w_out)


def _ffn_kernel(x_ref, g2_ref, wgu_ref, wd_ref, gf_ref, o_ref, *, ff_chunk, final):
    x2 = _ffn(x_ref[...], g2_ref[...], wgu_ref, wd_ref, ff_chunk)
    o_ref[...] = _rmsnorm(x2, gf_ref[...]) if final else x2


def _ffn_prompt(x, g2, w_gu, w_down, gf, tm, final):
    t, d = x.shape
    row = pl.BlockSpec((tm, d), lambda i: (i, 0))
    full = lambda a: pl.BlockSpec(a.shape, lambda i: (0,) * a.ndim)
    return pl.pallas_call(
        functools.partial(_ffn_kernel, ff_chunk=256, final=final),
        grid=(t // tm,),
        in_specs=[row, full(g2), full(w_gu), full(w_down), full(gf)],
        out_specs=row,
        out_shape=jax.ShapeDtypeStruct((t, d), F32),
        compiler_params=_cparams("parallel"),
        name="ffn_prompt",
    )(x, g2, w_gu, w_down, gf)


def _cache_kernel(q_ref, kn_ref, vn_ref, kb_ref, vb_ref, ob_ref, ko_ref, vo_ref, kw, vw, ks, vs):
    buf = kb_ref.shape[1]
    qb = Q_BLOCK
    ko_ref[0, 0:buf - 1, :] = kb_ref[0, 1:buf, :]
    ko_ref[0, buf - 1:buf, :] = kn_ref[0]
    vo_ref[0, 0:buf - 1, :] = vb_ref[0, 1:buf, :]
    vo_ref[0, buf - 1:buf, :] = vn_ref[0]

    rows8 = lax.broadcasted_iota(jnp.int32, (8, GROUP_W), 0)
    qmat = jnp.where(_head_of_lane((8, GROUP_W), 1) == rows8, q_ref[0], 0.0).astype(BF16)
    key = lax.broadcasted_iota(jnp.int32, (8, 2 * qb), 1)
    first = lax.broadcasted_iota(jnp.int32, (qb, GROUP_W), 0) == 0
    kw[qb:2 * qb, :] = jnp.where(first, kn_ref[0], 0.0).astype(BF16)
    vw[qb:2 * qb, :] = jnp.where(first, vn_ref[0], 0.0).astype(BF16)
    hw = GROUP_W // 2
    for src, dst in ((kb_ref, ks), (vb_ref, vs)):
        for half in range(2):
            dst[half] = src[0, :, half * hw:(half + 1) * hw]
    outs, lses = [], []
    for _, dil in DIL_CONFIGS:
        rows = pl.ds(buf - qb * dil, qb, stride=dil) if dil > 1 else pl.ds(buf - qb, qb)
        kw[0:qb, :] = jnp.concatenate([ks[0, rows, :], ks[1, rows, :]], axis=1).astype(BF16)
        vw[0:qb, :] = jnp.concatenate([vs[0, rows, :], vs[1, rows, :]], axis=1).astype(BF16)
        s = lax.dot_general(qmat, kw[...], (((1,), (1,)), ((), ())), preferred_element_type=F32)
        s = jnp.where(key <= qb, s, NEG)
        m = jnp.max(s, axis=-1, keepdims=True)
        p = jnp.exp(s - m)
        den = jnp.sum(p, axis=-1, keepdims=True)
        outs.append(_dot(p.astype(BF16), vw[...]) / den)
        lses.append(m + jnp.log(den))
    m = jnp.maximum(jnp.maximum(lses[0], lses[1]), lses[2])
    es = [jnp.exp(l - m) for l in lses]
    o = (es[0] * outs[0] + es[1] * outs[1] + es[2] * outs[2]) / (es[0] + es[1] + es[2])
    o = jnp.where(_head_of_lane((8, GROUP_W), 1) == rows8, o, 0.0)
    ob_ref[0] = jnp.sum(o, axis=0, keepdims=True)


def _cache_step(q, kn, vn, kbuf, vbuf):
    n, buf, w = kbuf.shape
    tok = pl.BlockSpec((1, 1, w), lambda i: (i, 0, 0))
    blk = pl.BlockSpec((1, buf, w), lambda i: (i, 0, 0))
    return pl.pallas_call(
        _cache_kernel,
        grid=(n,),
        in_specs=[tok, tok, tok, blk, blk],
        out_specs=[tok, blk, blk],
        out_shape=[jax.ShapeDtypeStruct((n, 1, w), F32), jax.ShapeDtypeStruct((n, buf, w), F32),
                   jax.ShapeDtypeStruct((n, buf, w), F32)],
        scratch_shapes=[pltpu.VMEM((2 * Q_BLOCK, w), BF16)] * 2 + [pltpu.VMEM((2, buf, w // 2), F32)] * 2,
        compiler_params=_cparams("parallel"),
        name="cache_step",
    )(q, kn, vn, kbuf, vbuf)


def _sample_rest_kernel(ag_ref, cd_ref, ob_ref, x_ref, sa_ref, sc_ref, sp_ref, caw_ref, cab_ref, lng_ref, lnb_ref,
                        ccw_ref, pw_ref, ps_ref, wo_ref, g2_ref, wgu_ref, wd_ref, gf_ref,
                        y_ref, na_ref, nc_ref, np_ref, *, pos0, final):
    gw = GROUP_W
    ga = ag_ref[:, 0:gw] * jax.nn.sigmoid(ag_ref[:, gw:2 * gw])
    na = CONV_A_WIDTH - 1
    acc = caw_ref[na:na + 1, :] * ga
    for t in range(na):
        acc += caw_ref[t:t + 1, :] * sa_ref[:, t * gw:(t + 1) * gw]
    ya = jax.nn.silu(_layernorm(acc + cab_ref[...], lng_ref[...], lnb_ref[...]))
    na_ref[:, 0:(na - 1) * gw] = sa_ref[:, gw:na * gw]
    na_ref[:, (na - 1) * gw:na * gw] = ga

    cx = cd_ref[:, 2 * gw:3 * gw] * cd_ref[:, 0:gw]
    nc = CONV_C_WIDTH - 1
    acc = ccw_ref[nc:nc + 1, :] * cx
    for t in range(nc):
        acc += ccw_ref[t:t + 1, :] * sc_ref[:, t * gw:(t + 1) * gw]
    yc = cd_ref[:, gw:2 * gw] * acc
    nc_ref[:, 0:(nc - 1) * gw] = sc_ref[:, gw:nc * gw]
    nc_ref[:, (nc - 1) * gw:nc * gw] = cx

    u = cd_ref[:, 3 * gw:4 * gw]
    npl = POOL_STATE
    back = lambda i: sp_ref[:, (npl - i) * gw:(npl - i + 1) * gw]
    s2 = u + back(1)
    s4 = s2 + back(2) + back(3)
    s8 = s4 + back(4) + back(5) + back(6) + back(7)
    s16 = s8
    for i in range(8, 16):
        s16 = s16 + back(i)
    cnt = jnp.minimum(_pool_window_lanes(u.shape), pos0 + 1).astype(F32)
    yd = _dot((_pool_select(s2, s4, s8, s16) / cnt - u).astype(BF16), pw_ref[...]) * ps_ref[...]
    np_ref[:, 0:(npl - 1) * gw] = sp_ref[:, gw:npl * gw]
    np_ref[:, (npl - 1) * gw:npl * gw] = u

    x1 = x_ref[...] + _out_proj(ya, ob_ref[...], yc, yd, wo_ref)
    x2 = _ffn(x1, g2_ref[...], wgu_ref, wd_ref, 256)
    y_ref[...] = _rmsnorm(x2, gf_ref[...]) if final else x2


def _sample_rest(ag, cd, ob, x, sa, sc, sp, caw, cab, lng, lnb, ccw, pool_bd, ps, w_out, g2, w_gu, w_down, gf, pos0, final):
    outs = [jax.ShapeDtypeStruct(a.shape, F32) for a in (x, sa, sc, sp)]
    return pl.pallas_call(
        functools.partial(_sample_rest_kernel, pos0=pos0, final=final),
        out_shape=outs,
        compiler_params=pltpu.CompilerParams(vmem_limit_bytes=VMEM_LIMIT_BYTES),
        name="sample_rest",
    )(ag, cd, ob, x, sa, sc, sp, caw, cab, lng, lnb, ccw, pool_bd, ps, w_out, g2, w_gu, w_down, gf)


def _block_diag(pool_w):
    g, c, e = pool_w.shape
    eye = jnp.eye(g, dtype=pool_w.dtype)
    return (pool_w[:, :, None, :] * eye[:, None, :, None]).reshape(g * c, g * e)


def kernel(x_prompt, x_sample, cache_win_k, cache_win_v, state_conv_a, state_conv_c, state_pool, w_in, conv_a_w, conv_a_b, ln_a_g, ln_a_b, conv_c_w, pool_w, pool_scale, w_out, norm1_g, norm2_g, w_gu, w_down, final_g):
    depth = w_in.shape[0]
    n, s, d = x_prompt.shape
    ns, ts, _ = x_sample.shape
    assert ts == 1
    buf = cache_win_k.shape[2]
    past_len = PAST_LEN
    gw = GROUP_W
    tm = 512

    row = lambda a: a.reshape(1, -1)
    xp = x_prompt.reshape(n * s, d)
    xs = x_sample.reshape(ns, d)
    gf = row(final_g)
    st_p = [[] for _ in range(5)]
    st_s = [[] for _ in range(5)]
    for l in range(depth):
        w_in_l, w_out_l = w_in[l].astype(BF16), w_out[l].astype(BF16)
        w_gu_l, w_down_l = w_gu[l].astype(BF16), w_down[l].astype(BF16)
        pool_bd = _block_diag(pool_w[l]).astype(BF16)
        small = (conv_a_w[l], row(conv_a_b[l]), row(ln_a_g[l]), row(ln_a_b[l]), conv_c_w[l], pool_bd, row(pool_scale[l]))
        final = l == depth - 1

        ag, q, k, v, cd = _inproj(xp, row(norm1_g[l]), w_in_l, tm)
        ob = _attn_prompt(q.reshape(n, s, gw), k.reshape(n, s, gw), v.reshape(n, s, gw))
        xp, sa, sc, sp = _mix_prompt(ag, cd, ob.reshape(n * s, gw), xp, *small, w_out_l, n, tm)
        xp = _ffn_prompt(xp, row(norm2_g[l]), w_gu_l, w_down_l, gf, tm, final)
        for lst, a in zip(st_p, (k.reshape(n, s, N_HEADS, HEAD_DIM), v.reshape(n, s, N_HEADS, HEAD_DIM), sa, sc, sp)):
            lst.append(a)

        ag, q, k, v, cd = _inproj(xs, row(norm1_g[l]), w_in_l, ns)
        ob, nk, nv = _cache_step(q.reshape(ns, 1, gw), k.reshape(ns, 1, gw), v.reshape(ns, 1, gw),
                                 cache_win_k[l].reshape(ns, buf, gw), cache_win_v[l].reshape(ns, buf, gw))
        xs, na, nc, npool = _sample_rest(
            ag, cd, ob.reshape(ns, gw), xs, state_conv_a[l].reshape(ns, -1), state_conv_c[l].reshape(ns, -1),
            state_pool[l].reshape(ns, -1), *small, w_out_l, row(norm2_g[l]), w_gu_l, w_down_l, gf, past_len, final)
        for lst, a in zip(st_s, (nk.reshape(ns, buf, N_HEADS, HEAD_DIM), nv.reshape(ns, buf, N_HEADS, HEAD_DIM),
                                 na.reshape(ns, CONV_A_WIDTH - 1, gw), nc.reshape(ns, CONV_C_WIDTH - 1, gw),
                                 npool.reshape(ns, POOL_STATE, gw))):
            lst.append(a)

    y_prompt = xp.reshape(n, s, d)
    y_sample = xs.reshape(ns, ts, d)
    return (y_prompt, y_sample, *[jnp.stack(a, axis=0) for a in st_p], *[jnp.stack(a, axis=0) for a in st_s])
```

```python
import functools
import math

import jax
import jax.numpy as jnp
from jax import lax
from jax.experimental import pallas as pl
from jax.experimental.pallas import tpu as pltpu

F32 = jnp.float32
BF16 = jnp.bfloat16

GROUP_W = 256
HEAD_DIM = 64
N_HEADS = GROUP_W // HEAD_DIM
CONV_A_WIDTH = 31
CONV_C_WIDTH = 3
POOL_WINDOWS = (2, 4, 8, 16)
POOL_STATE = max(POOL_WINDOWS) - 1
DIL_CONFIGS = ((128, 1), (512, 4), (2048, 16))
Q_BLOCK = 128
PAST_LEN = 16384
ATTN_SCALE = 1.0 / math.sqrt(HEAD_DIM)
EPS = 1e-6
NEG = -1e30

VMEM_LIMIT_BYTES = 56 * 1024 * 1024


def _cparams(*sem):
    return pltpu.CompilerParams(dimension_semantics=sem, vmem_limit_bytes=VMEM_LIMIT_BYTES)


def _rmsnorm(x, g):
    return x * lax.rsqrt(jnp.mean(x * x, axis=-1, keepdims=True) + EPS) * g


def _layernorm(x, g, b):
    mu = jnp.mean(x, axis=-1, keepdims=True)
    xc = x - mu
    return xc * lax.rsqrt(jnp.mean(xc * xc, axis=-1, keepdims=True) + EPS) * g + b


def _dot(a, b):
    return jnp.dot(a, b, preferred_element_type=F32)


def _head_of_lane(shape, dim):
    return lax.broadcasted_iota(jnp.int32, shape, dim) // HEAD_DIM


def _inproj_kernel(x_ref, g_ref, w_ref, *refs, n_alias):
    ag_ref, q_ref, k_ref, v_ref, cd_ref, *t_refs = refs[n_alias:]
    h = _rmsnorm(x_ref[...], g_ref[...]).astype(BF16)
    gw = GROUP_W
    ag_ref[...] = _dot(h, w_ref[:, 0:2 * gw])
    q_ref[...] = _dot(h, w_ref[:, 2 * gw:3 * gw]) * ATTN_SCALE
    k = _dot(h, w_ref[:, 3 * gw:4 * gw])
    v = _dot(h, w_ref[:, 4 * gw:5 * gw])
    k_ref[...] = k
    v_ref[...] = v
    cd_ref[...] = _dot(h, w_ref[:, 5 * gw:9 * gw])
    if t_refs:
        t_refs[0][0, 0] = k.T
        t_refs[1][0, 0] = v.T


def _inproj(x, g, w_in, tm, state=None):
    t, d = x.shape
    gw = GROUP_W
    row = lambda w: pl.BlockSpec((tm, w), lambda i: (i, 0))
    full = lambda a: pl.BlockSpec(a.shape, lambda i: (0,) * a.ndim)
    in_specs = [row(d), full(g), full(w_in)]
    out_specs = [row(2 * gw), row(gw), row(gw), row(gw), row(4 * gw)]
    out_shape = [jax.ShapeDtypeStruct((t, w), F32) for w in (2 * gw, gw, gw, gw, 4 * gw)]
    args, aliases = [x, g, w_in], {}
    if state is not None:
        layer, depth, n_seq, kt_all, vt_all = state
        spb = t // n_seq // tm
        out_specs += [pl.BlockSpec((1, 1, gw, tm), lambda i: (layer, i // spb, 0, i % spb))] * 2
        out_shape += [jax.ShapeDtypeStruct((depth, n_seq, gw, t // n_seq), F32)] * 2
        if kt_all is not None:
            in_specs += [pl.BlockSpec(memory_space=pl.ANY)] * 2
            args += [kt_all, vt_all]
            aliases = {3: 5, 4: 6}
    return pl.pallas_call(
        functools.partial(_inproj_kernel, n_alias=len(aliases)),
        grid=(t // tm,),
        in_specs=in_specs,
        out_specs=out_specs,
        out_shape=out_shape,
        input_output_aliases=aliases,
        compiler_params=_cparams("parallel"),
        name="inproj",
    )(*args)


def _softmax_pv(s, vwin, head_rows):
    r = s.shape[0] // N_HEADS
    m = jnp.max(s, axis=-1, keepdims=True)
    p = jnp.exp(s - m)
    den = jnp.sum(p, axis=-1, keepdims=True)
    pv = _dot(p.astype(BF16), vwin) / den
    lse = m + jnp.log(den)
    o = jnp.zeros((r, GROUP_W), F32)
    l = jnp.zeros((r, GROUP_W), F32)
    for h in range(N_HEADS):
        sel = head_rows == h
        o = jnp.where(sel, pv[h * r:(h + 1) * r], o)
        l = jnp.where(sel, lse[h * r:(h + 1) * r], l)
    return o, l


def _attn_kernel(q_ref, k_ref, v_ref, ob_ref, q_scr, k_scr, v_scr, o_scr, l_scr, *, seq):
    qb = Q_BLOCK
    hw = GROUP_W // 2
    for src, dst in ((q_ref, q_scr), (k_ref, k_scr), (v_ref, v_scr)):
        for half in range(2):
            dst[half] = src[0, :, half * hw:(half + 1) * hw]

    head_q = _head_of_lane((qb, GROUP_W), 1)
    qi = lax.broadcasted_iota(jnp.int32, (N_HEADS * qb, 2 * qb), 0) % qb
    kk = lax.broadcasted_iota(jnp.int32, (N_HEADS * qb, 2 * qb), 1)
    in_prev = kk < qb
    band = jnp.logical_or(jnp.logical_and(in_prev, kk >= qi), jnp.logical_and(kk >= qb, kk - qb <= qi))

    for c, (_, dil) in enumerate(DIL_CONFIGS):
        n_blocks = seq // (dil * qb)

        def unit(u, carry, c=c, dil=dil, n_blocks=n_blocks):
            r = u // n_blocks
            b = u % n_blocks
            start = r + b * (qb * dil)
            start_prev = jnp.maximum(start - qb * dil, r)
            if dil == 1:
                start, start_prev = pl.multiple_of(start, qb), pl.multiple_of(start_prev, qb)
            rows = lambda s0: pl.ds(s0, qb, stride=dil) if dil > 1 else pl.ds(s0, qb)
            load = lambda scr, s0: jnp.concatenate([scr[0, rows(s0), :], scr[1, rows(s0), :]], axis=1)
            q = load(q_scr, start)
            kwin = jnp.concatenate([load(k_scr, start_prev), load(k_scr, start)], axis=0).astype(BF16)
            vwin = jnp.concatenate([load(v_scr, start_prev), load(v_scr, start)], axis=0).astype(BF16)
            qs = jnp.concatenate([jnp.where(head_q == h, q, 0.0) for h in range(N_HEADS)], axis=0).astype(BF16)
            s = lax.dot_general(qs, kwin, (((1,), (1,)), ((), ())), preferred_element_type=F32)
            valid = jnp.logical_and(band, jnp.logical_or(kk >= qb, b > 0))
            s = jnp.where(valid, s, NEG)
            o, l = _softmax_pv(s, vwin, head_q)
            for half in range(2):
                o_scr[c, half, rows(start), :] = o[:, half * hw:(half + 1) * hw]
                l_scr[c, half, rows(start), :] = l[:, half * hw:(half + 1) * hw]
            return carry

        lax.fori_loop(0, dil * n_blocks, unit, 0)

    for half in range(2):
        l0, l1, l2 = l_scr[0, half], l_scr[1, half], l_scr[2, half]
        m = jnp.maximum(jnp.maximum(l0, l1), l2)
        e0, e1, e2 = jnp.exp(l0 - m), jnp.exp(l1 - m), jnp.exp(l2 - m)
        ob_ref[0, :, half * hw:(half + 1) * hw] = (
            (e0 * o_scr[0, half] + e1 * o_scr[1, half] + e2 * o_scr[2, half]) / (e0 + e1 + e2))


def _attn_prompt(q, k, v):
    n, s, w = q.shape
    blk = pl.BlockSpec((1, s, w), lambda i: (i, 0, 0))
    return pl.pallas_call(
        functools.partial(_attn_kernel, seq=s),
        grid=(n,),
        in_specs=[blk, blk, blk],
        out_specs=blk,
        out_shape=jax.ShapeDtypeStruct((n, s, w), F32),
        scratch_shapes=[pltpu.VMEM((2, s, w // 2), F32)] * 3 + [pltpu.VMEM((len(DIL_CONFIGS), 2, s, w // 2), F32)] * 2,
        compiler_params=_cparams("parallel"),
        name="attn_prompt",
    )(q, k, v)


def _pool_select(s2, s4, s8, s16):
    grp = _head_of_lane(s2.shape, s2.ndim - 1)
    return jnp.where(grp == 0, s2, jnp.where(grp == 1, s4, jnp.where(grp == 2, s8, s16)))


def _pool_window_lanes(shape):
    grp = _head_of_lane(shape, len(shape) - 1)
    return jnp.where(grp == 0, 2, jnp.where(grp == 1, 4, jnp.where(grp == 2, 8, 16)))


def _out_proj(ya, ob, yc, yd, w_out_ref):
    gw = GROUP_W
    acc = _dot(ya.astype(BF16), w_out_ref[0:gw, :])
    acc += _dot(ob.astype(BF16), w_out_ref[gw:2 * gw, :])
    acc += _dot(yc.astype(BF16), w_out_ref[2 * gw:3 * gw, :])
    acc += _dot(yd.astype(BF16), w_out_ref[3 * gw:4 * gw, :])
    return acc


def _ffn(x, g2, w_gu_ref, w_down_ref, ff_chunk):
    ff = w_down_ref.shape[0]
    h = _rmsnorm(x, g2).astype(BF16)
    acc = jnp.zeros_like(x)
    for c in range(ff // ff_chunk):
        lo = c * ff_chunk
        g = _dot(h, w_gu_ref[:, lo:lo + ff_chunk])
        u = _dot(h, w_gu_ref[:, ff + lo:ff + lo + ff_chunk])
        acc += _dot((jax.nn.silu(g) * u).astype(BF16), w_down_ref[lo:lo + ff_chunk, :])
    return x + acc


A_PAD = 32
C_PAD = 8
P_PAD = 16


def _mix_kernel(ag_ref, cd_ref, ob_ref, x_ref, caw_ref, cab_ref, lng_ref, lnb_ref, ccw_ref, pw_ref, ps_ref, wo_ref,
                x1_ref, sta_ref, stc_ref, stp_ref, abuf, cbuf, pbuf, *, tm):
    gw = GROUP_W
    j = pl.program_id(1)

    @pl.when(j == 0)
    def _():
        abuf[0:A_PAD, :] = jnp.zeros((A_PAD, gw), F32)
        cbuf[0:C_PAD, :] = jnp.zeros((C_PAD, gw), F32)
        pbuf[0:P_PAD, :] = jnp.zeros((P_PAD, gw), F32)

    abuf[A_PAD:A_PAD + tm, :] = ag_ref[:, 0:gw] * jax.nn.sigmoid(ag_ref[:, gw:2 * gw])
    off = A_PAD - (CONV_A_WIDTH - 1)
    acc = jnp.zeros((tm, gw), F32)
    for t in range(CONV_A_WIDTH):
        acc += caw_ref[t:t + 1, :] * abuf[off + t:off + t + tm, :]
    ya = jax.nn.silu(_layernorm(acc + cab_ref[...], lng_ref[...], lnb_ref[...]))

    cbuf[C_PAD:C_PAD + tm, :] = cd_ref[:, 2 * gw:3 * gw] * cd_ref[:, 0:gw]
    off = C_PAD - (CONV_C_WIDTH - 1)
    acc = jnp.zeros((tm, gw), F32)
    for t in range(CONV_C_WIDTH):
        acc += ccw_ref[t:t + 1, :] * cbuf[off + t:off + t + tm, :]
    yc = cd_ref[:, gw:2 * gw] * acc

    u = cd_ref[:, 3 * gw:4 * gw]
    pbuf[P_PAD:P_PAD + tm, :] = u
    back = lambda i: pbuf[P_PAD - i:P_PAD - i + tm, :]
    s2 = u + back(1)
    s4 = s2 + back(2) + back(3)
    s8 = s4 + back(4) + back(5) + back(6) + back(7)
    s16 = s8
    for i in range(8, 16):
        s16 = s16 + back(i)
    pos = j * tm + lax.broadcasted_iota(jnp.int32, (tm, gw), 0)
    cnt = jnp.minimum(_pool_window_lanes((tm, gw)), pos + 1).astype(F32)
    yd = _dot((_pool_select(s2, s4, s8, s16) / cnt - u).astype(BF16), pw_ref[...]) * ps_ref[...]

    x1_ref[...] = x_ref[...] + _out_proj(ya, ob_ref[...], yc, yd, wo_ref)

    @pl.when(j == pl.num_programs(1) - 1)
    def _():
        sta_ref[0] = abuf[A_PAD + tm - (CONV_A_WIDTH - 1):A_PAD + tm, :]
        stc_ref[0] = cbuf[C_PAD + tm - (CONV_C_WIDTH - 1):C_PAD + tm, :]
        stp_ref[0] = pbuf[P_PAD + tm - POOL_STATE:P_PAD + tm, :]

    abuf[0:A_PAD, :] = abuf[tm:tm + A_PAD, :]
    cbuf[0:C_PAD, :] = cbuf[tm:tm + C_PAD, :]
    pbuf[0:P_PAD, :] = pbuf[tm:tm + P_PAD, :]


def _mix_prompt(ag, cd, ob, x, caw, cab, lng, lnb, ccw, pool_bd, ps, w_out, n, tm):
    t, d = x.shape
    gw = GROUP_W
    spb = t // n // tm
    row = lambda w: pl.BlockSpec((tm, w), lambda i, j: (i * spb + j, 0))
    full = lambda a: pl.BlockSpec(a.shape, lambda i, j: (0,) * a.ndim)
    st = lambda r: pl.BlockSpec((1, r, gw), lambda i, j: (i, 0, 0))
    st_rows = (CONV_A_WIDTH - 1, CONV_C_WIDTH - 1, POOL_STATE)
    return pl.pallas_call(
        functools.partial(_mix_kernel, tm=tm),
        grid=(n, spb),
        in_specs=[row(2 * gw), row(4 * gw), row(gw), row(d)] + [full(a) for a in (caw, cab, lng, lnb, ccw, pool_bd, ps, w_out)],
        out_specs=[row(d)] + [st(r) for r in st_rows],
        out_shape=[jax.ShapeDtypeStruct((t, d), F32)] + [jax.ShapeDtypeStruct((n, r, gw), F32) for r in st_rows],
        scratch_shapes=[pltpu.VMEM((A_PAD + tm, gw), F32), pltpu.VMEM((C_PAD + tm, gw), F32),
                        pltpu.VMEM((P_PAD + tm, gw), F32)],
        compiler_params=_cparams("parallel", "arbitrary"),
        name="mix_prompt",
    )(ag, cd, ob, x, caw, cab, lng, lnb, ccw, pool_bd, ps, w_out)


def _ffn_kernel(x_ref, g2_ref, wgu_ref, wd_ref, gf_ref, o_ref, *, ff_chunk, final):
    x2 = _ffn(x_ref[...], g2_ref[...], wgu_ref, wd_ref, ff_chunk)
    o_ref[...] = _rmsnorm(x2, gf_ref[...]) if final else x2


def _ffn_prompt(x, g2, w_gu, w_down, gf, tm, final):
    t, d = x.shape
    row = pl.BlockSpec((tm, d), lambda i: (i, 0))
    full = lambda a: pl.BlockSpec(a.shape, lambda i: (0,) * a.ndim)
    return pl.pallas_call(
        functools.partial(_ffn_kernel, ff_chunk=256, final=final),
        grid=(t // tm,),
        in_specs=[row, full(g2), full(w_gu), full(w_down), full(gf)],
        out_specs=row,
        out_shape=jax.ShapeDtypeStruct((t, d), F32),
        compiler_params=_cparams("parallel"),
        name="ffn_prompt",
    )(x, g2, w_gu, w_down, gf)


def _cache_kernel(q_ref, kn_ref, vn_ref, kb_ref, vb_ref, *refs, n_alias):
    ob_ref, ko_ref, vo_ref = refs[n_alias:]
    gw = GROUP_W
    buf = kb_ref.shape[-1]
    ident = lax.broadcasted_iota(jnp.int32, (gw, gw), 0) == lax.broadcasted_iota(jnp.int32, (gw, gw), 1)
    to_col = lambda row: jnp.sum(jnp.where(ident, row, 0.0), axis=1, keepdims=True)
    to_row = lambda col: jnp.sum(jnp.where(ident, col, 0.0), axis=0, keepdims=True)
    qc, kc, vc = to_col(q_ref[0]), to_col(kn_ref[0]), to_col(vn_ref[0])
    kb, vb = kb_ref[0, 0], vb_ref[0, 0]
    last = lax.broadcasted_iota(jnp.int32, (gw, buf), 1) == buf - 1
    ko_ref[0, 0] = jnp.where(last, kc, pltpu.roll(kb, buf - 1, 1))
    vo_ref[0, 0] = jnp.where(last, vc, pltpu.roll(vb, buf - 1, 1))

    def heads(a):
        return jnp.sum(a.reshape(N_HEADS, HEAD_DIM, a.shape[-1]), axis=1)

    def spread(a):
        return jnp.broadcast_to(a[:, None, :], (N_HEADS, HEAD_DIM, a.shape[-1])).reshape(gw, a.shape[-1])

    s_all = heads(kb * qc)
    s_new = heads(kc * qc)
    outs, lses = [], []
    for win, dil in DIL_CONFIGS:
        lo = buf - win
        s = s_all[:, lo:]
        if dil > 1:
            back = win - lax.broadcasted_iota(jnp.int32, s.shape, 1)
            s = jnp.where(back % dil == 0, s, NEG)
        m = jnp.maximum(jnp.max(s, axis=1, keepdims=True), s_new)
        p, p_new = jnp.exp(s - m), jnp.exp(s_new - m)
        den = jnp.sum(p, axis=1, keepdims=True) + p_new
        pv = jnp.sum(vb[:, lo:] * spread(p), axis=1, keepdims=True) + vc * spread(p_new)
        outs.append(pv / spread(den))
        lses.append(spread(m + jnp.log(den)))
    m = jnp.maximum(jnp.maximum(lses[0], lses[1]), lses[2])
    es = [jnp.exp(l - m) for l in lses]
    ob_ref[0] = to_row((es[0] * outs[0] + es[1] * outs[1] + es[2] * outs[2]) / (es[0] + es[1] + es[2]))


def _cache_step(q, kn, vn, kbuf, vbuf, layer, k_all=None, v_all=None):
    depth, n, w, buf = kbuf.shape
    assert max(win for win, _ in DIL_CONFIGS) <= buf
    tok = pl.BlockSpec((1, 1, w), lambda i: (i, 0, 0))
    blk = pl.BlockSpec((1, 1, w, buf), lambda i: (layer, i, 0, 0))
    in_specs, args, aliases = [tok, tok, tok, blk, blk], [q, kn, vn, kbuf, vbuf], {}
    if k_all is not None:
        in_specs += [pl.BlockSpec(memory_space=pl.ANY)] * 2
        args += [k_all, v_all]
        aliases = {5: 1, 6: 2}
    return pl.pallas_call(
        functools.partial(_cache_kernel, n_alias=len(aliases)),
        grid=(n,),
        in_specs=in_specs,
        out_specs=[tok, blk, blk],
        out_shape=[jax.ShapeDtypeStruct((n, 1, w), F32)] + [jax.ShapeDtypeStruct(kbuf.shape, F32)] * 2,
        input_output_aliases=aliases,
        compiler_params=_cparams("parallel"),
        name="cache_step",
    )(*args)


def _sample_rest_kernel(ag_ref, cd_ref, ob_ref, x_ref, sa_ref, sc_ref, sp_ref, caw_ref, cab_ref, lng_ref, lnb_ref,
                        ccw_ref, pw_ref, ps_ref, wo_ref, g2_ref, wgu_ref, wd_ref, gf_ref,
                        y_ref, na_ref, nc_ref, np_ref, *, pos0, final):
    gw = GROUP_W
    ga = ag_ref[:, 0:gw] * jax.nn.sigmoid(ag_ref[:, gw:2 * gw])
    na = CONV_A_WIDTH - 1
    acc = caw_ref[na:na + 1, :] * ga
    for t in range(na):
        acc += caw_ref[t:t + 1, :] * sa_ref[0, t]
    ya = jax.nn.silu(_layernorm(acc + cab_ref[...], lng_ref[...], lnb_ref[...]))
    na_ref[0:na - 1] = sa_ref[0, 1:na]
    na_ref[na - 1] = ga

    cx = cd_ref[:, 2 * gw:3 * gw] * cd_ref[:, 0:gw]
    nc = CONV_C_WIDTH - 1
    acc = ccw_ref[nc:nc + 1, :] * cx
    for t in range(nc):
        acc += ccw_ref[t:t + 1, :] * sc_ref[0, t]
    yc = cd_ref[:, gw:2 * gw] * acc
    nc_ref[0:nc - 1] = sc_ref[0, 1:nc]
    nc_ref[nc - 1] = cx

    u = cd_ref[:, 3 * gw:4 * gw]
    npl = POOL_STATE
    back = lambda i: sp_ref[0, npl - i]
    s2 = u + back(1)
    s4 = s2 + back(2) + back(3)
    s8 = s4 + back(4) + back(5) + back(6) + back(7)
    s16 = s8
    for i in range(8, 16):
        s16 = s16 + back(i)
    cnt = jnp.minimum(_pool_window_lanes(u.shape), pos0 + 1).astype(F32)
    yd = _dot((_pool_select(s2, s4, s8, s16) / cnt - u).astype(BF16), pw_ref[...]) * ps_ref[...]
    np_ref[0:npl - 1] = sp_ref[0, 1:npl]
    np_ref[npl - 1] = u

    x1 = x_ref[...] + _out_proj(ya, ob_ref[...], yc, yd, wo_ref)
    x2 = _ffn(x1, g2_ref[...], wgu_ref, wd_ref, 256)
    y_ref[...] = _rmsnorm(x2, gf_ref[...]) if final else x2


def _sample_rest(ag, cd, ob, x, sa, sc, sp, caw, cab, lng, lnb, ccw, pool_bd, ps, w_out, g2, w_gu, w_down, gf, layer, pos0, final):
    full = lambda a: pl.BlockSpec(a.shape, lambda i: (0,) * a.ndim)
    st_in = lambda a: pl.BlockSpec((1,) + a.shape[1:], lambda i: (layer, 0, 0, 0))
    st_out = lambda a: pl.BlockSpec(a.shape[1:], lambda i: (0, 0, 0))
    dense = (caw, cab, lng, lnb, ccw, pool_bd, ps, w_out, g2, w_gu, w_down, gf)
    return pl.pallas_call(
        functools.partial(_sample_rest_kernel, pos0=pos0, final=final),
        grid=(1,),
        in_specs=[full(a) for a in (ag, cd, ob, x)] + [st_in(a) for a in (sa, sc, sp)] + [full(a) for a in dense],
        out_specs=[full(x)] + [st_out(a) for a in (sa, sc, sp)],
        out_shape=[jax.ShapeDtypeStruct(x.shape, F32)] + [jax.ShapeDtypeStruct(a.shape[1:], F32) for a in (sa, sc, sp)],
        compiler_params=_cparams("arbitrary"),
        name="sample_rest",
    )(ag, cd, ob, x, sa, sc, sp, *dense)


def _block_diag(pool_w):
    g, c, e = pool_w.shape
    eye = jnp.eye(g, dtype=pool_w.dtype)
    return (pool_w[:, :, None, :] * eye[:, None, :, None]).reshape(g * c, g * e)


def kernel(x_prompt, x_sample, cache_win_k, cache_win_v, state_conv_a, state_conv_c, state_pool, w_in, conv_a_w, conv_a_b, ln_a_g, ln_a_b, conv_c_w, pool_w, pool_scale, w_out, norm1_g, norm2_g, w_gu, w_down, final_g):
    depth = w_in.shape[0]
    n, s, d = x_prompt.shape
    ns, ts, _ = x_sample.shape
    assert ts == 1
    gw = GROUP_W
    tm = 512

    row = lambda a: a.reshape(1, -1)
    to_cm = lambda a: jnp.transpose(a, (0, 1, 3, 4, 2)).reshape(a.shape[0], a.shape[1], gw, a.shape[2])
    from_cm = lambda a: jnp.transpose(a.reshape(a.shape[0], a.shape[1], N_HEADS, HEAD_DIM, a.shape[3]), (0, 1, 4, 2, 3))
    swap = lambda a: jnp.transpose(a, (0, 2, 1, 3))
    kbuf, vbuf = to_cm(cache_win_k), to_cm(cache_win_v)
    sa_all, sc_all, sp_all = swap(state_conv_a), swap(state_conv_c), swap(state_pool)

    xp = x_prompt.reshape(n * s, d)
    xs = x_sample.reshape(ns, d)
    gf = row(final_g)
    kt_p = vt_p = kt_s = vt_s = None
    st_p = [[] for _ in range(3)]
    st_s = [[] for _ in range(3)]
    for l in range(depth):
        w_in_l, w_out_l = w_in[l].astype(BF16), w_out[l].astype(BF16)
        w_gu_l, w_down_l = w_gu[l].astype(BF16), w_down[l].astype(BF16)
        pool_bd = _block_diag(pool_w[l]).astype(BF16)
        small = (conv_a_w[l], row(conv_a_b[l]), row(ln_a_g[l]), row(ln_a_b[l]), conv_c_w[l], pool_bd, row(pool_scale[l]))
        final = l == depth - 1

        ag, q, k, v, cd, kt_p, vt_p = _inproj(xp, row(norm1_g[l]), w_in_l, tm, state=(l, depth, n, kt_p, vt_p))
        ob = _attn_prompt(q.reshape(n, s, gw), k.reshape(n, s, gw), v.reshape(n, s, gw))
        xp, *states = _mix_prompt(ag, cd, ob.reshape(n * s, gw), xp, *small, w_out_l, n, tm)
        xp = _ffn_prompt(xp, row(norm2_g[l]), w_gu_l, w_down_l, gf, tm, final)
        for lst, a in zip(st_p, states):
            lst.append(a)

        ag, q, k, v, cd = _inproj(xs, row(norm1_g[l]), w_in_l, ns)
        ob, kt_s, vt_s = _cache_step(q.reshape(ns, 1, gw), k.reshape(ns, 1, gw), v.reshape(ns, 1, gw),
                                     kbuf, vbuf, l, kt_s, vt_s)
        xs, *states = _sample_rest(ag, cd, ob.reshape(ns, gw), xs, sa_all, sc_all, sp_all, *small, w_out_l,
                                   row(norm2_g[l]), w_gu_l, w_down_l, gf, l, PAST_LEN, final)
        for lst, a in zip(st_s, states):
            lst.append(a)

    y_prompt = xp.reshape(n, s, d)
    y_sample = xs.reshape(ns, ts, d)
    return (y_prompt, y_sample, from_cm(kt_p), from_cm(vt_p), *[jnp.stack(a, axis=0) for a in st_p],
            from_cm(kt_s), from_cm(vt_s), *[swap(jnp.stack(a, axis=0)) for a in st_s])
```

```python
import functools
import math

import jax
import jax.numpy as jnp
from jax import lax
from jax.experimental import pallas as pl
from jax.experimental.pallas import tpu as pltpu

F32 = jnp.float32
BF16 = jnp.bfloat16

GROUP_W = 256
HEAD_DIM = 64
N_HEADS = GROUP_W // HEAD_DIM
CONV_A_WIDTH = 31
CONV_C_WIDTH = 3
POOL_WINDOWS = (2, 4, 8, 16)
POOL_STATE = max(POOL_WINDOWS) - 1
DIL_CONFIGS = ((128, 1), (512, 4), (2048, 16))
Q_BLOCK = 128
PAST_LEN = 16384
ATTN_SCALE = 1.0 / math.sqrt(HEAD_DIM)
EPS = 1e-6
NEG = -1e30

VMEM_LIMIT_BYTES = 56 * 1024 * 1024


def _cparams(*sem):
    return pltpu.CompilerParams(dimension_semantics=sem, vmem_limit_bytes=VMEM_LIMIT_BYTES)


def _rmsnorm(x, g):
    return x * lax.rsqrt(jnp.mean(x * x, axis=-1, keepdims=True) + EPS) * g


def _layernorm(x, g, b):
    mu = jnp.mean(x, axis=-1, keepdims=True)
    xc = x - mu
    return xc * lax.rsqrt(jnp.mean(xc * xc, axis=-1, keepdims=True) + EPS) * g + b


def _dot(a, b):
    return jnp.dot(a, b, preferred_element_type=F32)


def _head_of_lane(shape, dim):
    return lax.broadcasted_iota(jnp.int32, shape, dim) // HEAD_DIM


def _inproj_kernel(x_ref, g_ref, w_ref, *refs, n_alias):
    ag_ref, q_ref, k_ref, v_ref, cd_ref, *t_refs = refs[n_alias:]
    h = _rmsnorm(x_ref[...], g_ref[...]).astype(BF16)
    gw = GROUP_W
    ag_ref[...] = _dot(h, w_ref[:, 0:2 * gw])
    q_ref[...] = _dot(h, w_ref[:, 2 * gw:3 * gw]) * ATTN_SCALE
    k = _dot(h, w_ref[:, 3 * gw:4 * gw])
    v = _dot(h, w_ref[:, 4 * gw:5 * gw])
    k_ref[...] = k
    v_ref[...] = v
    cd_ref[...] = _dot(h, w_ref[:, 5 * gw:9 * gw])
    if t_refs:
        t_refs[0][0, 0] = k.T
        t_refs[1][0, 0] = v.T


def _inproj(x, g, w_in, tm, state=None):
    t, d = x.shape
    gw = GROUP_W
    row = lambda w: pl.BlockSpec((tm, w), lambda i: (i, 0))
    full = lambda a: pl.BlockSpec(a.shape, lambda i: (0,) * a.ndim)
    in_specs = [row(d), full(g), full(w_in)]
    out_specs = [row(2 * gw), row(gw), row(gw), row(gw), row(4 * gw)]
    out_shape = [jax.ShapeDtypeStruct((t, w), F32) for w in (2 * gw, gw, gw, gw, 4 * gw)]
    args, aliases = [x, g, w_in], {}
    if state is not None:
        layer, depth, n_seq, kt_all, vt_all = state
        spb = t // n_seq // tm
        out_specs += [pl.BlockSpec((1, 1, gw, tm), lambda i: (layer, i // spb, 0, i % spb))] * 2
        out_shape += [jax.ShapeDtypeStruct((depth, n_seq, gw, t // n_seq), F32)] * 2
        if kt_all is not None:
            in_specs += [pl.BlockSpec(memory_space=pl.ANY)] * 2
            args += [kt_all, vt_all]
            aliases = {3: 5, 4: 6}
    return pl.pallas_call(
        functools.partial(_inproj_kernel, n_alias=len(aliases)),
        grid=(t // tm,),
        in_specs=in_specs,
        out_specs=out_specs,
        out_shape=out_shape,
        input_output_aliases=aliases,
        compiler_params=_cparams("parallel"),
        name="inproj",
    )(*args)


HEAD_GROUPS = ((0, 1), (2, 3))


def _attn_kernel(q_ref, k_ref, v_ref, ob_ref, q_scr, k_scr, v_scr, o_scr, l_scr, *, seq):
    qb = Q_BLOCK
    hw = GROUP_W // 2
    for src, dst in ((q_ref, q_scr), (k_ref, k_scr), (v_ref, v_scr)):
        for half in range(2):
            dst[half] = src[0, :, half * hw:(half + 1) * hw]

    gh = len(HEAD_GROUPS[0])
    head_q = _head_of_lane((qb, GROUP_W), 1)
    qi = lax.broadcasted_iota(jnp.int32, (gh * qb, 2 * qb), 0) % qb
    kk = lax.broadcasted_iota(jnp.int32, (gh * qb, 2 * qb), 1)
    band = jnp.logical_or(jnp.logical_and(kk < qb, kk >= qi), jnp.logical_and(kk >= qb, kk - qb <= qi))

    for c, (_, dil) in enumerate(DIL_CONFIGS):
        n_blocks = seq // (dil * qb)

        def unit(u, carry, c=c, dil=dil, n_blocks=n_blocks):
            r = u // n_blocks
            b = u % n_blocks
            start = r + b * (qb * dil)
            start_prev = jnp.maximum(start - qb * dil, r)
            if dil == 1:
                start, start_prev = pl.multiple_of(start, qb), pl.multiple_of(start_prev, qb)
            rows = lambda s0: pl.ds(s0, qb, stride=dil) if dil > 1 else pl.ds(s0, qb)
            load = lambda scr, s0: jnp.concatenate([scr[0, rows(s0), :], scr[1, rows(s0), :]], axis=1)
            q = load(q_scr, start)
            kwin = jnp.concatenate([load(k_scr, start_prev), load(k_scr, start)], axis=0).astype(BF16)
            vwin = jnp.concatenate([load(v_scr, start_prev), load(v_scr, start)], axis=0).astype(BF16)
            valid = jnp.logical_and(band, jnp.logical_or(kk >= qb, b > 0))
            scores = []
            for grp in HEAD_GROUPS:
                qs = jnp.concatenate([jnp.where(head_q == h, q, 0.0) for h in grp], axis=0).astype(BF16)
                scores.append(lax.dot_general(qs, kwin, (((1,), (1,)), ((), ())), preferred_element_type=F32))
            probs, dens, lses = [], [], []
            for s in scores:
                s = jnp.where(valid, s, NEG)
                m = jnp.max(s, axis=-1, keepdims=True)
                p = jnp.exp(s - m)
                den = jnp.sum(p, axis=-1, keepdims=True)
                probs.append(p.astype(BF16))
                dens.append(den)
                lses.append(m + jnp.log(den))
            pvs = [_dot(p, vwin) for p in probs]
            o = d = l = jnp.zeros((qb, GROUP_W), F32)
            for gi, grp in enumerate(HEAD_GROUPS):
                for j, h in enumerate(grp):
                    sel = head_q == h
                    o = jnp.where(sel, pvs[gi][j * qb:(j + 1) * qb], o)
                    d = jnp.where(sel, dens[gi][j * qb:(j + 1) * qb], d)
                    l = jnp.where(sel, lses[gi][j * qb:(j + 1) * qb], l)
            o = o / d
            for half in range(2):
                o_scr[c, half, rows(start), :] = o[:, half * hw:(half + 1) * hw]
                l_scr[c, half, rows(start), :] = l[:, half * hw:(half + 1) * hw]
            return carry

        lax.fori_loop(0, dil * n_blocks, unit, 0, unroll=2)

    for half in range(2):
        l0, l1, l2 = l_scr[0, half], l_scr[1, half], l_scr[2, half]
        m = jnp.maximum(jnp.maximum(l0, l1), l2)
        e0, e1, e2 = jnp.exp(l0 - m), jnp.exp(l1 - m), jnp.exp(l2 - m)
        ob_ref[0, :, half * hw:(half + 1) * hw] = (
            (e0 * o_scr[0, half] + e1 * o_scr[1, half] + e2 * o_scr[2, half]) / (e0 + e1 + e2))


def _attn_prompt(q, k, v):
    n, s, w = q.shape
    blk = pl.BlockSpec((1, s, w), lambda i: (i, 0, 0))
    return pl.pallas_call(
        functools.partial(_attn_kernel, seq=s),
        grid=(n,),
        in_specs=[blk, blk, blk],
        out_specs=blk,
        out_shape=jax.ShapeDtypeStruct((n, s, w), F32),
        scratch_shapes=[pltpu.VMEM((2, s, w // 2), F32)] * 3 + [pltpu.VMEM((len(DIL_CONFIGS), 2, s, w // 2), F32)] * 2,
        compiler_params=_cparams("parallel"),
        name="attn_prompt",
    )(q, k, v)


def _pool_select(s2, s4, s8, s16):
    grp = _head_of_lane(s2.shape, s2.ndim - 1)
    return jnp.where(grp == 0, s2, jnp.where(grp == 1, s4, jnp.where(grp == 2, s8, s16)))


def _pool_window_lanes(shape):
    grp = _head_of_lane(shape, len(shape) - 1)
    return jnp.where(grp == 0, 2, jnp.where(grp == 1, 4, jnp.where(grp == 2, 8, 16)))


def _out_proj(ya, ob, yc, yd, w_out_ref):
    gw = GROUP_W
    acc = _dot(ya.astype(BF16), w_out_ref[0:gw, :])
    acc += _dot(ob.astype(BF16), w_out_ref[gw:2 * gw, :])
    acc += _dot(yc.astype(BF16), w_out_ref[2 * gw:3 * gw, :])
    acc += _dot(yd.astype(BF16), w_out_ref[3 * gw:4 * gw, :])
    return acc


def _ffn(x, g2, w_gu_ref, w_down_ref, ff_chunk):
    ff = w_down_ref.shape[0]
    h = _rmsnorm(x, g2).astype(BF16)
    acc = jnp.zeros_like(x)
    for c in range(ff // ff_chunk):
        lo = c * ff_chunk
        g = _dot(h, w_gu_ref[:, lo:lo + ff_chunk])
        u = _dot(h, w_gu_ref[:, ff + lo:ff + lo + ff_chunk])
        acc += _dot((jax.nn.silu(g) * u).astype(BF16), w_down_ref[lo:lo + ff_chunk, :])
    return x + acc


A_PAD = 32
C_PAD = 8
P_PAD = 16
SUBLANES = 8


def _causal_conv(buf, w_ref, width, pad, tm):
    base = pad - SUBLANES
    rows = tm + SUBLANES
    y = None
    for a in range(min(SUBLANES, width)):
        z = None
        for lag in range(a, width, SUBLANES):
            term = w_ref[width - 1 - lag:width - lag, :] * buf[base - (lag - a):base - (lag - a) + rows, :]
            z = term if z is None else z + term
        z = pltpu.roll(z, a, 0) if a else z
        y = z if y is None else y + z
    return y[SUBLANES:SUBLANES + tm]


def _mix_kernel(ag_ref, cd_ref, ob_ref, x_ref, caw_ref, cab_ref, lng_ref, lnb_ref, ccw_ref, pw_ref, ps_ref, wo_ref,
                x1_ref, sta_ref, stc_ref, stp_ref, abuf, cbuf, pbuf, *, tm):
    gw = GROUP_W
    j = pl.program_id(1)

    @pl.when(j == 0)
    def _():
        abuf[0:A_PAD, :] = jnp.zeros((A_PAD, gw), F32)
        cbuf[0:C_PAD, :] = jnp.zeros((C_PAD, gw), F32)
        pbuf[0:P_PAD, :] = jnp.zeros((P_PAD, gw), F32)

    abuf[A_PAD:A_PAD + tm, :] = ag_ref[:, 0:gw] * jax.nn.sigmoid(ag_ref[:, gw:2 * gw])
    acc = _causal_conv(abuf, caw_ref, CONV_A_WIDTH, A_PAD, tm)
    ya = jax.nn.silu(_layernorm(acc + cab_ref[...], lng_ref[...], lnb_ref[...]))

    cbuf[C_PAD:C_PAD + tm, :] = cd_ref[:, 2 * gw:3 * gw] * cd_ref[:, 0:gw]
    yc = cd_ref[:, gw:2 * gw] * _causal_conv(cbuf, ccw_ref, CONV_C_WIDTH, C_PAD, tm)

    u = cd_ref[:, 3 * gw:4 * gw]
    pbuf[P_PAD:P_PAD + tm, :] = u
    s1 = pbuf[...]
    s2 = s1 + pltpu.roll(s1, 1, 0)
    s4 = s2 + pltpu.roll(s2, 2, 0)
    s8 = s4 + pltpu.roll(s4, 4, 0)
    s16 = s8 + pltpu.roll(s8, 8, 0)
    win = _pool_select(s2, s4, s8, s16)[P_PAD:P_PAD + tm]
    pos = j * tm + lax.broadcasted_iota(jnp.int32, (tm, gw), 0)
    cnt = jnp.minimum(_pool_window_lanes((tm, gw)), pos + 1).astype(F32)
    yd = _dot((win / cnt - u).astype(BF16), pw_ref[...]) * ps_ref[...]

    x1_ref[...] = x_ref[...] + _out_proj(ya, ob_ref[...], yc, yd, wo_ref)

    @pl.when(j == pl.num_programs(1) - 1)
    def _():
        sta_ref[0] = abuf[A_PAD + tm - (CONV_A_WIDTH - 1):A_PAD + tm, :]
        stc_ref[0] = cbuf[C_PAD + tm - (CONV_C_WIDTH - 1):C_PAD + tm, :]
        stp_ref[0] = pbuf[P_PAD + tm - POOL_STATE:P_PAD + tm, :]

    abuf[0:A_PAD, :] = abuf[tm:tm + A_PAD, :]
    cbuf[0:C_PAD, :] = cbuf[tm:tm + C_PAD, :]
    pbuf[0:P_PAD, :] = pbuf[tm:tm + P_PAD, :]


def _mix_prompt(ag, cd, ob, x, caw, cab, lng, lnb, ccw, pool_bd, ps, w_out, n, tm):
    t, d = x.shape
    gw = GROUP_W
    spb = t // n // tm
    row = lambda w: pl.BlockSpec((tm, w), lambda i, j: (i * spb + j, 0))
    full = lambda a: pl.BlockSpec(a.shape, lambda i, j: (0,) * a.ndim)
    st = lambda r: pl.BlockSpec((1, r, gw), lambda i, j: (i, 0, 0))
    st_rows = (CONV_A_WIDTH - 1, CONV_C_WIDTH - 1, POOL_STATE)
    return pl.pallas_call(
        functools.partial(_mix_kernel, tm=tm),
        grid=(n, spb),
        in_specs=[row(2 * gw), row(4 * gw), row(gw), row(d)] + [full(a) for a in (caw, cab, lng, lnb, ccw, pool_bd, ps, w_out)],
        out_specs=[row(d)] + [st(r) for r in st_rows],
        out_shape=[jax.ShapeDtypeStruct((t, d), F32)] + [jax.ShapeDtypeStruct((n, r, gw), F32) for r in st_rows],
        scratch_shapes=[pltpu.VMEM((A_PAD + tm, gw), F32), pltpu.VMEM((C_PAD + tm, gw), F32),
                        pltpu.VMEM((P_PAD + tm, gw), F32)],
        compiler_params=_cparams("parallel", "arbitrary"),
        name="mix_prompt",
    )(ag, cd, ob, x, caw, cab, lng, lnb, ccw, pool_bd, ps, w_out)


def _ffn_kernel(x_ref, g2_ref, wgu_ref, wd_ref, gf_ref, o_ref, *, ff_chunk, final):
    x2 = _ffn(x_ref[...], g2_ref[...], wgu_ref, wd_ref, ff_chunk)
    o_ref[...] = _rmsnorm(x2, gf_ref[...]) if final else x2


def _ffn_prompt(x, g2, w_gu, w_down, gf, tm, final):
    t, d = x.shape
    row = pl.BlockSpec((tm, d), lambda i: (i, 0))
    full = lambda a: pl.BlockSpec(a.shape, lambda i: (0,) * a.ndim)
    return pl.pallas_call(
        functools.partial(_ffn_kernel, ff_chunk=256, final=final),
        grid=(t // tm,),
        in_specs=[row, full(g2), full(w_gu), full(w_down), full(gf)],
        out_specs=row,
        out_shape=jax.ShapeDtypeStruct((t, d), F32),
        compiler_params=_cparams("parallel"),
        name="ffn_prompt",
    )(x, g2, w_gu, w_down, gf)


def _cache_kernel(q_ref, kn_ref, vn_ref, kb_ref, vb_ref, *refs, n_alias):
    ob_ref, ko_ref, vo_ref = refs[n_alias:]
    gw = GROUP_W
    buf = kb_ref.shape[-1]
    ident = lax.broadcasted_iota(jnp.int32, (gw, gw), 0) == lax.broadcasted_iota(jnp.int32, (gw, gw), 1)
    to_col = lambda row: jnp.sum(jnp.where(ident, row, 0.0), axis=1, keepdims=True)
    to_row = lambda col: jnp.sum(jnp.where(ident, col, 0.0), axis=0, keepdims=True)
    qc, kc, vc = to_col(q_ref[0]), to_col(kn_ref[0]), to_col(vn_ref[0])
    kb, vb = kb_ref[0, 0], vb_ref[0, 0]
    last = lax.broadcasted_iota(jnp.int32, (gw, buf), 1) == buf - 1
    ko_ref[0, 0] = jnp.where(last, kc, pltpu.roll(kb, buf - 1, 1))
    vo_ref[0, 0] = jnp.where(last, vc, pltpu.roll(vb, buf - 1, 1))

    def heads(a):
        return jnp.sum(a.reshape(N_HEADS, HEAD_DIM, a.shape[-1]), axis=1)

    def spread(a):
        return jnp.broadcast_to(a[:, None, :], (N_HEADS, HEAD_DIM, a.shape[-1])).reshape(gw, a.shape[-1])

    s_all = heads(kb * qc)
    s_new = heads(kc * qc)
    outs, lses = [], []
    for win, dil in DIL_CONFIGS:
        lo = buf - win
        s = s_all[:, lo:]
        if dil > 1:
            back = win - lax.broadcasted_iota(jnp.int32, s.shape, 1)
            s = jnp.where(back % dil == 0, s, NEG)
        m = jnp.maximum(jnp.max(s, axis=1, keepdims=True), s_new)
        p, p_new = jnp.exp(s - m), jnp.exp(s_new - m)
        den = jnp.sum(p, axis=1, keepdims=True) + p_new
        pv = jnp.sum(vb[:, lo:] * spread(p), axis=1, keepdims=True) + vc * spread(p_new)
        outs.append(pv / spread(den))
        lses.append(spread(m + jnp.log(den)))
    m = jnp.maximum(jnp.maximum(lses[0], lses[1]), lses[2])
    es = [jnp.exp(l - m) for l in lses]
    ob_ref[0] = to_row((es[0] * outs[0] + es[1] * outs[1] + es[2] * outs[2]) / (es[0] + es[1] + es[2]))


def _cache_step(q, kn, vn, kbuf, vbuf, layer, k_all=None, v_all=None):
    depth, n, w, buf = kbuf.shape
    assert max(win for win, _ in DIL_CONFIGS) <= buf
    tok = pl.BlockSpec((1, 1, w), lambda i: (i, 0, 0))
    blk = pl.BlockSpec((1, 1, w, buf), lambda i: (layer, i, 0, 0))
    in_specs, args, aliases = [tok, tok, tok, blk, blk], [q, kn, vn, kbuf, vbuf], {}
    if k_all is not None:
        in_specs += [pl.BlockSpec(memory_space=pl.ANY)] * 2
        args += [k_all, v_all]
        aliases = {5: 1, 6: 2}
    return pl.pallas_call(
        functools.partial(_cache_kernel, n_alias=len(aliases)),
        grid=(n,),
        in_specs=in_specs,
        out_specs=[tok, blk, blk],
        out_shape=[jax.ShapeDtypeStruct((n, 1, w), F32)] + [jax.ShapeDtypeStruct(kbuf.shape, F32)] * 2,
        input_output_aliases=aliases,
        compiler_params=_cparams("parallel"),
        name="cache_step",
    )(*args)


def _sample_rest_kernel(ag_ref, cd_ref, ob_ref, x_ref, sa_ref, sc_ref, sp_ref, caw_ref, cab_ref, lng_ref, lnb_ref,
                        ccw_ref, pw_ref, ps_ref, wo_ref, g2_ref, wgu_ref, wd_ref, gf_ref,
                        y_ref, na_ref, nc_ref, np_ref, *, pos0, final):
    gw = GROUP_W
    ga = ag_ref[:, 0:gw] * jax.nn.sigmoid(ag_ref[:, gw:2 * gw])
    na = CONV_A_WIDTH - 1
    acc = caw_ref[na:na + 1, :] * ga
    for t in range(na):
        acc += caw_ref[t:t + 1, :] * sa_ref[0, t]
    ya = jax.nn.silu(_layernorm(acc + cab_ref[...], lng_ref[...], lnb_ref[...]))
    na_ref[0:na - 1] = sa_ref[0, 1:na]
    na_ref[na - 1] = ga

    cx = cd_ref[:, 2 * gw:3 * gw] * cd_ref[:, 0:gw]
    nc = CONV_C_WIDTH - 1
    acc = ccw_ref[nc:nc + 1, :] * cx
    for t in range(nc):
        acc += ccw_ref[t:t + 1, :] * sc_ref[0, t]
    yc = cd_ref[:, gw:2 * gw] * acc
    nc_ref[0:nc - 1] = sc_ref[0, 1:nc]
    nc_ref[nc - 1] = cx

    u = cd_ref[:, 3 * gw:4 * gw]
    npl = POOL_STATE
    back = lambda i: sp_ref[0, npl - i]
    s2 = u + back(1)
    s4 = s2 + back(2) + back(3)
    s8 = s4 + back(4) + back(5) + back(6) + back(7)
    s16 = s8
    for i in range(8, 16):
        s16 = s16 + back(i)
    cnt = jnp.minimum(_pool_window_lanes(u.shape), pos0 + 1).astype(F32)
    yd = _dot((_pool_select(s2, s4, s8, s16) / cnt - u).astype(BF16), pw_ref[...]) * ps_ref[...]
    np_ref[0:npl - 1] = sp_ref[0, 1:npl]
    np_ref[npl - 1] = u

    x1 = x_ref[...] + _out_proj(ya, ob_ref[...], yc, yd, wo_ref)
    x2 = _ffn(x1, g2_ref[...], wgu_ref, wd_ref, 256)
    y_ref[...] = _rmsnorm(x2, gf_ref[...]) if final else x2


def _sample_rest(ag, cd, ob, x, sa, sc, sp, caw, cab, lng, lnb, ccw, pool_bd, ps, w_out, g2, w_gu, w_down, gf, layer, pos0, final):
    full = lambda a: pl.BlockSpec(a.shape, lambda i: (0,) * a.ndim)
    st_in = lambda a: pl.BlockSpec((1,) + a.shape[1:], lambda i: (layer, 0, 0, 0))
    st_out = lambda a: pl.BlockSpec(a.shape[1:], lambda i: (0, 0, 0))
    dense = (caw, cab, lng, lnb, ccw, pool_bd, ps, w_out, g2, w_gu, w_down, gf)
    return pl.pallas_call(
        functools.partial(_sample_rest_kernel, pos0=pos0, final=final),
        grid=(1,),
        in_specs=[full(a) for a in (ag, cd, ob, x)] + [st_in(a) for a in (sa, sc, sp)] + [full(a) for a in dense],
        out_specs=[full(x)] + [st_out(a) for a in (sa, sc, sp)],
        out_shape=[jax.ShapeDtypeStruct(x.shape, F32)] + [jax.ShapeDtypeStruct(a.shape[1:], F32) for a in (sa, sc, sp)],
        compiler_params=_cparams("arbitrary"),
        name="sample_rest",
    )(ag, cd, ob, x, sa, sc, sp, *dense)


def _block_diag(pool_w):
    g, c, e = pool_w.shape
    eye = jnp.eye(g, dtype=pool_w.dtype)
    return (pool_w[:, :, None, :] * eye[:, None, :, None]).reshape(g * c, g * e)


def kernel(x_prompt, x_sample, cache_win_k, cache_win_v, state_conv_a, state_conv_c, state_pool, w_in, conv_a_w, conv_a_b, ln_a_g, ln_a_b, conv_c_w, pool_w, pool_scale, w_out, norm1_g, norm2_g, w_gu, w_down, final_g):
    depth = w_in.shape[0]
    n, s, d = x_prompt.shape
    ns, ts, _ = x_sample.shape
    assert ts == 1
    gw = GROUP_W
    tm = 512

    row = lambda a: a.reshape(1, -1)
    to_cm = lambda a: jnp.transpose(a, (0, 1, 3, 4, 2)).reshape(a.shape[0], a.shape[1], gw, a.shape[2])
    from_cm = lambda a: jnp.transpose(a.reshape(a.shape[0], a.shape[1], N_HEADS, HEAD_DIM, a.shape[3]), (0, 1, 4, 2, 3))
    swap = lambda a: jnp.transpose(a, (0, 2, 1, 3))
    kbuf, vbuf = to_cm(cache_win_k), to_cm(cache_win_v)
    sa_all, sc_all, sp_all = swap(state_conv_a), swap(state_conv_c), swap(state_pool)

    xp = x_prompt.reshape(n * s, d)
    xs = x_sample.reshape(ns, d)
    gf = row(final_g)
    kt_p = vt_p = kt_s = vt_s = None
    st_p = [[] for _ in range(3)]
    st_s = [[] for _ in range(3)]
    for l in range(depth):
        w_in_l, w_out_l = w_in[l].astype(BF16), w_out[l].astype(BF16)
        w_gu_l, w_down_l = w_gu[l].astype(BF16), w_down[l].astype(BF16)
        pool_bd = _block_diag(pool_w[l]).astype(BF16)
        small = (conv_a_w[l], row(conv_a_b[l]), row(ln_a_g[l]), row(ln_a_b[l]), conv_c_w[l], pool_bd, row(pool_scale[l]))
        final = l == depth - 1

        ag, q, k, v, cd, kt_p, vt_p = _inproj(xp, row(norm1_g[l]), w_in_l, tm, state=(l, depth, n, kt_p, vt_p))
        ob = _attn_prompt(q.reshape(n, s, gw), k.reshape(n, s, gw), v.reshape(n, s, gw))
        xp, *states = _mix_prompt(ag, cd, ob.reshape(n * s, gw), xp, *small, w_out_l, n, tm)
        xp = _ffn_prompt(xp, row(norm2_g[l]), w_gu_l, w_down_l, gf, tm, final)
        for lst, a in zip(st_p, states):
            lst.append(a)

        ag, q, k, v, cd = _inproj(xs, row(norm1_g[l]), w_in_l, ns)
        ob, kt_s, vt_s = _cache_step(q.reshape(ns, 1, gw), k.reshape(ns, 1, gw), v.reshape(ns, 1, gw),
                                     kbuf, vbuf, l, kt_s, vt_s)
        xs, *states = _sample_rest(ag, cd, ob.reshape(ns, gw), xs, sa_all, sc_all, sp_all, *small, w_out_l,
                                   row(norm2_g[l]), w_gu_l, w_down_l, gf, l, PAST_LEN, final)
        for lst, a in zip(st_s, states):
            lst.append(a)

    y_prompt = xp.reshape(n, s, d)
    y_sample = xs.reshape(ns, ts, d)
    return (y_prompt, y_sample, from_cm(kt_p), from_cm(vt_p), *[jnp.stack(a, axis=0) for a in st_p],
            from_cm(kt_s), from_cm(vt_s), *[swap(jnp.stack(a, axis=0)) for a in st_s])
```

```python
import functools
import math

import jax
import jax.numpy as jnp
from jax import lax
from jax.experimental import pallas as pl
from jax.experimental.pallas import tpu as pltpu

F32 = jnp.float32
BF16 = jnp.bfloat16

GROUP_W = 256
HEAD_DIM = 64
N_HEADS = GROUP_W // HEAD_DIM
CONV_A_WIDTH = 31
CONV_C_WIDTH = 3
POOL_WINDOWS = (2, 4, 8, 16)
POOL_STATE = max(POOL_WINDOWS) - 1
DIL_CONFIGS = ((128, 1), (512, 4), (2048, 16))
Q_BLOCK = 128
PAST_LEN = 16384
ATTN_SCALE = 1.0 / math.sqrt(HEAD_DIM)
EPS = 1e-6
NEG = -1e30

VMEM_LIMIT_BYTES = 56 * 1024 * 1024


def _cparams(*sem):
    return pltpu.CompilerParams(dimension_semantics=sem, vmem_limit_bytes=VMEM_LIMIT_BYTES)


def _rmsnorm(x, g):
    return x * lax.rsqrt(jnp.mean(x * x, axis=-1, keepdims=True) + EPS) * g


def _layernorm(x, g, b):
    mu = jnp.mean(x, axis=-1, keepdims=True)
    xc = x - mu
    return xc * lax.rsqrt(jnp.mean(xc * xc, axis=-1, keepdims=True) + EPS) * g + b


def _layer_spec(a, layer):
    return pl.BlockSpec((1,) + a.shape[1:], lambda *_: (layer,) + (0,) * (a.ndim - 1))


def _dot(a, b):
    return jnp.dot(a, b, preferred_element_type=F32)


def _head_of_lane(shape, dim):
    return lax.broadcasted_iota(jnp.int32, shape, dim) // HEAD_DIM


def _inproj_kernel(x_ref, g_ref, w_ref, *refs, n_alias):
    ag_ref, q_ref, k_ref, v_ref, cd_ref, *t_refs = refs[n_alias:]
    w_ref = w_ref.at[0]
    h = _rmsnorm(x_ref[...], g_ref[...]).astype(BF16)
    gw = GROUP_W
    ag_ref[...] = _dot(h, w_ref[:, 0:2 * gw])
    q_ref[...] = _dot(h, w_ref[:, 2 * gw:3 * gw]) * ATTN_SCALE
    k = _dot(h, w_ref[:, 3 * gw:4 * gw])
    v = _dot(h, w_ref[:, 4 * gw:5 * gw])
    k_ref[...] = k
    v_ref[...] = v
    cd_ref[...] = _dot(h, w_ref[:, 5 * gw:9 * gw])
    if t_refs:
        t_refs[0][0, 0] = k.T
        t_refs[1][0, 0] = v.T


def _inproj(x, g, w_in, layer, tm, state=None):
    t, d = x.shape
    gw = GROUP_W
    row = lambda w: pl.BlockSpec((tm, w), lambda i: (i, 0))
    full = lambda a: pl.BlockSpec(a.shape, lambda i: (0,) * a.ndim)
    in_specs = [row(d), full(g), _layer_spec(w_in, layer)]
    out_specs = [row(2 * gw), row(gw), row(gw), row(gw), row(4 * gw)]
    out_shape = [jax.ShapeDtypeStruct((t, w), F32) for w in (2 * gw, gw, gw, gw, 4 * gw)]
    args, aliases = [x, g, w_in], {}
    if state is not None:
        depth, n_seq, kt_all, vt_all = state
        spb = t // n_seq // tm
        out_specs += [pl.BlockSpec((1, 1, gw, tm), lambda i: (layer, i // spb, 0, i % spb))] * 2
        out_shape += [jax.ShapeDtypeStruct((depth, n_seq, gw, t // n_seq), F32)] * 2
        if kt_all is not None:
            in_specs += [pl.BlockSpec(memory_space=pl.ANY)] * 2
            args += [kt_all, vt_all]
            aliases = {3: 5, 4: 6}
    return pl.pallas_call(
        functools.partial(_inproj_kernel, n_alias=len(aliases)),
        grid=(t // tm,),
        in_specs=in_specs,
        out_specs=out_specs,
        out_shape=out_shape,
        input_output_aliases=aliases,
        compiler_params=_cparams("parallel"),
        name="inproj",
    )(*args)


HEAD_GROUPS = ((0, 1), (2, 3))
UNITS_PER_STEP = 4


def _attn_kernel(q_ref, k_ref, v_ref, ob_ref, q_scr, k_scr, v_scr, qc, kc, vtc, o_scr, l_scr, *, seq):
    qb = Q_BLOCK
    hw = GROUP_W // 2
    for src, dst in ((q_ref, q_scr), (k_ref, k_scr), (v_ref, v_scr)):
        for half in range(2):
            dst[half] = src[0, :, half * hw:(half + 1) * hw]
    kc[0:qb, :] = jnp.zeros((qb, GROUP_W), BF16)
    vtc[:, 0:qb] = jnp.zeros((GROUP_W, qb), BF16)

    gh = len(HEAD_GROUPS[0])
    head_q = _head_of_lane((qb, GROUP_W), 1)
    kk = lax.broadcasted_iota(jnp.int32, (2 * qb, gh * qb), 0)
    qi = lax.broadcasted_iota(jnp.int32, (2 * qb, gh * qb), 1) % qb
    band = jnp.logical_or(jnp.logical_and(kk < qb, kk >= qi), jnp.logical_and(kk >= qb, kk - qb <= qi))

    for c, (_, dil) in enumerate(DIL_CONFIGS):
        n_blocks = seq // (dil * qb)
        n_steps = seq // qb // UNITS_PER_STEP
        rows = lambda s0, dil=dil: pl.ds(s0, qb, stride=dil) if dil > 1 else pl.ds(s0, qb)

        def block_start(u, dil=dil, n_blocks=n_blocks):
            start = u // n_blocks + (u % n_blocks) * (qb * dil)
            return pl.multiple_of(start, qb) if dil == 1 else start

        def stage(u, carry, rows=rows, block_start=block_start):
            start = block_start(u)
            load = lambda scr: jnp.concatenate([scr[0, rows(start), :], scr[1, rows(start), :]], axis=1)
            dst = pl.multiple_of(u * qb, qb)
            qc[pl.ds(dst, qb), :] = load(q_scr)
            kc[pl.ds(dst + qb, qb), :] = load(k_scr).astype(BF16)
            vtc[:, pl.ds(dst + qb, qb)] = load(v_scr).T.astype(BF16)
            return carry

        lax.fori_loop(0, seq // qb, stage, 0, unroll=4)

        def unit(u, carry, c=c, n_blocks=n_blocks, rows=rows, block_start=block_start):
            us = [u * UNITS_PER_STEP + i for i in range(UNITS_PER_STEP)]
            w0s = [pl.multiple_of(ui * qb, qb) for ui in us]
            chains = [(i, g) for i in range(UNITS_PER_STEP) for g in range(len(HEAD_GROUPS))]
            ch = gh * HEAD_DIM
            scores = []
            for i, g in chains:
                q = qc[pl.ds(w0s[i], qb), :]
                kwin = kc[pl.ds(w0s[i], 2 * qb), :]
                qm = jnp.concatenate([jnp.where(head_q == h, q, 0.0) for h in HEAD_GROUPS[g]], axis=0).astype(BF16)
                scores.append(lax.dot_general(kwin, qm, (((1,), (1,)), ((), ())), preferred_element_type=F32))
            probs, dens, lses = [], [], []
            for (i, g), s in zip(chains, scores):
                valid = jnp.logical_and(band, jnp.logical_or(kk >= qb, us[i] % n_blocks > 0))
                s = jnp.where(valid, s, NEG)
                m = jnp.max(s, axis=0, keepdims=True)
                p = jnp.exp(s - m)
                den = jnp.sum(p, axis=0, keepdims=True)
                probs.append(p.astype(BF16))
                dens.append(den)
                lses.append(m + jnp.log(den))
            nums = [_dot(vtc[g * ch:(g + 1) * ch, pl.ds(w0s[i], 2 * qb)], p) for (i, g), p in zip(chains, probs)]
            for i in range(UNITS_PER_STEP):
                o_t, l_t = [], []
                for g in range(len(HEAD_GROUPS)):
                    n = chains.index((i, g))
                    for j in range(gh):
                        cols = slice(j * qb, (j + 1) * qb)
                        o_t.append(nums[n][j * HEAD_DIM:(j + 1) * HEAD_DIM, cols] / dens[n][:, cols])
                        l_t.append(jnp.broadcast_to(lses[n][:, cols], (HEAD_DIM, qb)))
                o = jnp.concatenate(o_t, axis=0).T
                l = jnp.concatenate(l_t, axis=0).T
                start = block_start(us[i])
                for half in range(2):
                    o_scr[c, half, rows(start), :] = o[:, half * hw:(half + 1) * hw]
                    l_scr[c, half, rows(start), :] = l[:, half * hw:(half + 1) * hw]
            return carry

        lax.fori_loop(0, n_steps, unit, 0)

    for half in range(2):
        l0, l1, l2 = l_scr[0, half], l_scr[1, half], l_scr[2, half]
        m = jnp.maximum(jnp.maximum(l0, l1), l2)
        e0, e1, e2 = jnp.exp(l0 - m), jnp.exp(l1 - m), jnp.exp(l2 - m)
        ob_ref[0, :, half * hw:(half + 1) * hw] = (
            (e0 * o_scr[0, half] + e1 * o_scr[1, half] + e2 * o_scr[2, half]) / (e0 + e1 + e2))


def _attn_prompt(q, k, v):
    n, s, w = q.shape
    blk = pl.BlockSpec((1, s, w), lambda i: (i, 0, 0))
    return pl.pallas_call(
        functools.partial(_attn_kernel, seq=s),
        grid=(n,),
        in_specs=[blk, blk, blk],
        out_specs=blk,
        out_shape=jax.ShapeDtypeStruct((n, s, w), F32),
        scratch_shapes=[pltpu.VMEM((2, s, w // 2), F32)] * 3
        + [pltpu.VMEM((s, w), F32), pltpu.VMEM((Q_BLOCK + s, w), BF16), pltpu.VMEM((w, Q_BLOCK + s), BF16)]
        + [pltpu.VMEM((len(DIL_CONFIGS), 2, s, w // 2), F32)] * 2,
        compiler_params=_cparams("parallel"),
        name="attn_prompt",
    )(q, k, v)


def _pool_select(s2, s4, s8, s16):
    grp = _head_of_lane(s2.shape, s2.ndim - 1)
    return jnp.where(grp == 0, s2, jnp.where(grp == 1, s4, jnp.where(grp == 2, s8, s16)))


def _pool_window_lanes(shape):
    grp = _head_of_lane(shape, len(shape) - 1)
    return jnp.where(grp == 0, 2, jnp.where(grp == 1, 4, jnp.where(grp == 2, 8, 16)))


def _out_proj(ya, ob, yc, yd, w_out_ref):
    gw = GROUP_W
    acc = _dot(ya.astype(BF16), w_out_ref[0:gw, :])
    acc += _dot(ob.astype(BF16), w_out_ref[gw:2 * gw, :])
    acc += _dot(yc.astype(BF16), w_out_ref[2 * gw:3 * gw, :])
    acc += _dot(yd.astype(BF16), w_out_ref[3 * gw:4 * gw, :])
    return acc


def _ffn(x, g2, w_gu_ref, w_down_ref, ff_chunk):
    ff = w_down_ref.shape[0]
    h = _rmsnorm(x, g2).astype(BF16)
    acc = jnp.zeros_like(x)
    for c in range(ff // ff_chunk):
        lo = c * ff_chunk
        g = _dot(h, w_gu_ref[:, lo:lo + ff_chunk])
        u = _dot(h, w_gu_ref[:, ff + lo:ff + lo + ff_chunk])
        acc += _dot((jax.nn.silu(g) * u).astype(BF16), w_down_ref[lo:lo + ff_chunk, :])
    return x + acc


A_PAD = 32
C_PAD = 8
P_PAD = 16
SUBLANES = 8


def _causal_conv(buf, w_ref, width, pad, tm):
    base = pad - SUBLANES
    rows = tm + SUBLANES
    y = None
    for a in range(min(SUBLANES, width)):
        z = None
        for lag in range(a, width, SUBLANES):
            term = w_ref[width - 1 - lag:width - lag, :] * buf[base - (lag - a):base - (lag - a) + rows, :]
            z = term if z is None else z + term
        z = pltpu.roll(z, a, 0) if a else z
        y = z if y is None else y + z
    return y[SUBLANES:SUBLANES + tm]


def _mix_kernel(ag_ref, cd_ref, ob_ref, x_ref, caw_ref, cab_ref, lng_ref, lnb_ref, ccw_ref, pw_ref, ps_ref, wo_ref,
                x1_ref, sta_ref, stc_ref, stp_ref, abuf, cbuf, pbuf, *, tm):
    gw = GROUP_W
    wo_ref = wo_ref.at[0]
    j = pl.program_id(1)

    @pl.when(j == 0)
    def _():
        abuf[0:A_PAD, :] = jnp.zeros((A_PAD, gw), F32)
        cbuf[0:C_PAD, :] = jnp.zeros((C_PAD, gw), F32)
        pbuf[0:P_PAD, :] = jnp.zeros((P_PAD, gw), F32)

    abuf[A_PAD:A_PAD + tm, :] = ag_ref[:, 0:gw] * jax.nn.sigmoid(ag_ref[:, gw:2 * gw])
    acc = _causal_conv(abuf, caw_ref, CONV_A_WIDTH, A_PAD, tm)
    ya = jax.nn.silu(_layernorm(acc + cab_ref[...], lng_ref[...], lnb_ref[...]))

    cbuf[C_PAD:C_PAD + tm, :] = cd_ref[:, 2 * gw:3 * gw] * cd_ref[:, 0:gw]
    yc = cd_ref[:, gw:2 * gw] * _causal_conv(cbuf, ccw_ref, CONV_C_WIDTH, C_PAD, tm)

    u = cd_ref[:, 3 * gw:4 * gw]
    pbuf[P_PAD:P_PAD + tm, :] = u
    s1 = pbuf[...]
    s2 = s1 + pltpu.roll(s1, 1, 0)
    s4 = s2 + pltpu.roll(s2, 2, 0)
    s8 = s4 + pltpu.roll(s4, 4, 0)
    s16 = s8 + pltpu.roll(s8, 8, 0)
    win = _pool_select(s2, s4, s8, s16)[P_PAD:P_PAD + tm]
    pos = j * tm + lax.broadcasted_iota(jnp.int32, (tm, gw), 0)
    cnt = jnp.minimum(_pool_window_lanes((tm, gw)), pos + 1).astype(F32)
    yd = _dot((win / cnt - u).astype(BF16), pw_ref[...]) * ps_ref[...]

    x1_ref[...] = x_ref[...] + _out_proj(ya, ob_ref[...], yc, yd, wo_ref)

    @pl.when(j == pl.num_programs(1) - 1)
    def _():
        sta_ref[0] = abuf[A_PAD + tm - (CONV_A_WIDTH - 1):A_PAD + tm, :]
        stc_ref[0] = cbuf[C_PAD + tm - (CONV_C_WIDTH - 1):C_PAD + tm, :]
        stp_ref[0] = pbuf[P_PAD + tm - POOL_STATE:P_PAD + tm, :]

    abuf[0:A_PAD, :] = abuf[tm:tm + A_PAD, :]
    cbuf[0:C_PAD, :] = cbuf[tm:tm + C_PAD, :]
    pbuf[0:P_PAD, :] = pbuf[tm:tm + P_PAD, :]


def _mix_prompt(ag, cd, ob, x, caw, cab, lng, lnb, ccw, pool_bd, ps, w_out, layer, n, tm):
    t, d = x.shape
    gw = GROUP_W
    spb = t // n // tm
    row = lambda w: pl.BlockSpec((tm, w), lambda i, j: (i * spb + j, 0))
    full = lambda a: pl.BlockSpec(a.shape, lambda i, j: (0,) * a.ndim)
    st = lambda r: pl.BlockSpec((1, r, gw), lambda i, j: (i, 0, 0))
    st_rows = (CONV_A_WIDTH - 1, CONV_C_WIDTH - 1, POOL_STATE)
    return pl.pallas_call(
        functools.partial(_mix_kernel, tm=tm),
        grid=(n, spb),
        in_specs=[row(2 * gw), row(4 * gw), row(gw), row(d)] + [full(a) for a in (caw, cab, lng, lnb, ccw, pool_bd, ps)]
        + [_layer_spec(w_out, layer)],
        out_specs=[row(d)] + [st(r) for r in st_rows],
        out_shape=[jax.ShapeDtypeStruct((t, d), F32)] + [jax.ShapeDtypeStruct((n, r, gw), F32) for r in st_rows],
        scratch_shapes=[pltpu.VMEM((A_PAD + tm, gw), F32), pltpu.VMEM((C_PAD + tm, gw), F32),
                        pltpu.VMEM((P_PAD + tm, gw), F32)],
        compiler_params=_cparams("parallel", "arbitrary"),
        name="mix_prompt",
    )(ag, cd, ob, x, caw, cab, lng, lnb, ccw, pool_bd, ps, w_out)


def _ffn_kernel(x_ref, g2_ref, wgu_ref, wd_ref, gf_ref, o_ref, *, ff_chunk, final):
    x2 = _ffn(x_ref[...], g2_ref[...], wgu_ref.at[0], wd_ref.at[0], ff_chunk)
    o_ref[...] = _rmsnorm(x2, gf_ref[...]) if final else x2


def _ffn_prompt(x, g2, w_gu, w_down, gf, layer, tm, final):
    t, d = x.shape
    row = pl.BlockSpec((tm, d), lambda i: (i, 0))
    full = lambda a: pl.BlockSpec(a.shape, lambda i: (0,) * a.ndim)
    return pl.pallas_call(
        functools.partial(_ffn_kernel, ff_chunk=256, final=final),
        grid=(t // tm,),
        in_specs=[row, full(g2), _layer_spec(w_gu, layer), _layer_spec(w_down, layer), full(gf)],
        out_specs=row,
        out_shape=jax.ShapeDtypeStruct((t, d), F32),
        compiler_params=_cparams("parallel"),
        name="ffn_prompt",
    )(x, g2, w_gu, w_down, gf)


def _cache_kernel(q_ref, kn_ref, vn_ref, kb_ref, vb_ref, *refs, n_alias):
    ob_ref, ko_ref, vo_ref = refs[n_alias:]
    gw = GROUP_W
    buf = kb_ref.shape[-1]
    ident = lax.broadcasted_iota(jnp.int32, (gw, gw), 0) == lax.broadcasted_iota(jnp.int32, (gw, gw), 1)
    to_col = lambda row: jnp.sum(jnp.where(ident, row, 0.0), axis=1, keepdims=True)
    to_row = lambda col: jnp.sum(jnp.where(ident, col, 0.0), axis=0, keepdims=True)
    qc, kc, vc = to_col(q_ref[0]), to_col(kn_ref[0]), to_col(vn_ref[0])
    kb, vb = kb_ref[0, 0], vb_ref[0, 0]
    last = lax.broadcasted_iota(jnp.int32, (gw, buf), 1) == buf - 1
    ko_ref[0, 0] = jnp.where(last, kc, pltpu.roll(kb, buf - 1, 1))
    vo_ref[0, 0] = jnp.where(last, vc, pltpu.roll(vb, buf - 1, 1))

    def heads(a):
        return jnp.sum(a.reshape(N_HEADS, HEAD_DIM, a.shape[-1]), axis=1)

    def spread(a):
        return jnp.broadcast_to(a[:, None, :], (N_HEADS, HEAD_DIM, a.shape[-1])).reshape(gw, a.shape[-1])

    s_all = heads(kb * qc)
    s_new = heads(kc * qc)
    outs, lses = [], []
    for win, dil in DIL_CONFIGS:
        lo = buf - win
        s = s_all[:, lo:]
        if dil > 1:
            back = win - lax.broadcasted_iota(jnp.int32, s.shape, 1)
            s = jnp.where(back % dil == 0, s, NEG)
        m = jnp.maximum(jnp.max(s, axis=1, keepdims=True), s_new)
        p, p_new = jnp.exp(s - m), jnp.exp(s_new - m)
        den = jnp.sum(p, axis=1, keepdims=True) + p_new
        pv = jnp.sum(vb[:, lo:] * spread(p), axis=1, keepdims=True) + vc * spread(p_new)
        outs.append(pv / spread(den))
        lses.append(spread(m + jnp.log(den)))
    m = jnp.maximum(jnp.maximum(lses[0], lses[1]), lses[2])
    es = [jnp.exp(l - m) for l in lses]
    ob_ref[0] = to_row((es[0] * outs[0] + es[1] * outs[1] + es[2] * outs[2]) / (es[0] + es[1] + es[2]))


def _cache_step(q, kn, vn, kbuf, vbuf, layer, k_all=None, v_all=None):
    depth, n, w, buf = kbuf.shape
    assert max(win for win, _ in DIL_CONFIGS) <= buf
    tok = pl.BlockSpec((1, 1, w), lambda i: (i, 0, 0))
    blk = pl.BlockSpec((1, 1, w, buf), lambda i: (layer, i, 0, 0))
    in_specs, args, aliases = [tok, tok, tok, blk, blk], [q, kn, vn, kbuf, vbuf], {}
    if k_all is not None:
        in_specs += [pl.BlockSpec(memory_space=pl.ANY)] * 2
        args += [k_all, v_all]
        aliases = {5: 1, 6: 2}
    return pl.pallas_call(
        functools.partial(_cache_kernel, n_alias=len(aliases)),
        grid=(n,),
        in_specs=in_specs,
        out_specs=[tok, blk, blk],
        out_shape=[jax.ShapeDtypeStruct((n, 1, w), F32)] + [jax.ShapeDtypeStruct(kbuf.shape, F32)] * 2,
        input_output_aliases=aliases,
        compiler_params=_cparams("parallel"),
        name="cache_step",
    )(*args)


def _sample_rest_kernel(ag_ref, cd_ref, ob_ref, x_ref, sa_ref, sc_ref, sp_ref, caw_ref, cab_ref, lng_ref, lnb_ref,
                        ccw_ref, pw_ref, ps_ref, wo_ref, g2_ref, wgu_ref, wd_ref, gf_ref,
                        y_ref, na_ref, nc_ref, np_ref, *, pos0, final):
    gw = GROUP_W
    ga = ag_ref[:, 0:gw] * jax.nn.sigmoid(ag_ref[:, gw:2 * gw])
    na = CONV_A_WIDTH - 1
    acc = caw_ref[na:na + 1, :] * ga
    for t in range(na):
        acc += caw_ref[t:t + 1, :] * sa_ref[0, t]
    ya = jax.nn.silu(_layernorm(acc + cab_ref[...], lng_ref[...], lnb_ref[...]))
    na_ref[0:na - 1] = sa_ref[0, 1:na]
    na_ref[na - 1] = ga

    cx = cd_ref[:, 2 * gw:3 * gw] * cd_ref[:, 0:gw]
    nc = CONV_C_WIDTH - 1
    acc = ccw_ref[nc:nc + 1, :] * cx
    for t in range(nc):
        acc += ccw_ref[t:t + 1, :] * sc_ref[0, t]
    yc = cd_ref[:, gw:2 * gw] * acc
    nc_ref[0:nc - 1] = sc_ref[0, 1:nc]
    nc_ref[nc - 1] = cx

    u = cd_ref[:, 3 * gw:4 * gw]
    npl = POOL_STATE
    back = lambda i: sp_ref[0, npl - i]
    s2 = u + back(1)
    s4 = s2 + back(2) + back(3)
    s8 = s4 + back(4) + back(5) + back(6) + back(7)
    s16 = s8
    for i in range(8, 16):
        s16 = s16 + back(i)
    cnt = jnp.minimum(_pool_window_lanes(u.shape), pos0 + 1).astype(F32)
    yd = _dot((_pool_select(s2, s4, s8, s16) / cnt - u).astype(BF16), pw_ref[...]) * ps_ref[...]
    np_ref[0:npl - 1] = sp_ref[0, 1:npl]
    np_ref[npl - 1] = u

    x1 = x_ref[...] + _out_proj(ya, ob_ref[...], yc, yd, wo_ref.at[0])
    x2 = _ffn(x1, g2_ref[...], wgu_ref.at[0], wd_ref.at[0], 256)
    y_ref[...] = _rmsnorm(x2, gf_ref[...]) if final else x2


def _sample_rest(ag, cd, ob, x, sa, sc, sp, caw, cab, lng, lnb, ccw, pool_bd, ps, w_out, g2, w_gu, w_down, gf, layer, pos0, final):
    full = lambda a: pl.BlockSpec(a.shape, lambda i: (0,) * a.ndim)
    st_in = lambda a: pl.BlockSpec((1,) + a.shape[1:], lambda i: (layer, 0, 0, 0))
    st_out = lambda a: pl.BlockSpec(a.shape[1:], lambda i: (0, 0, 0))
    dense = (caw, cab, lng, lnb, ccw, pool_bd, ps, w_out, g2, w_gu, w_down, gf)
    stacked = (w_out, w_gu, w_down)
    dense_spec = lambda a: _layer_spec(a, layer) if any(a is w for w in stacked) else full(a)
    return pl.pallas_call(
        functools.partial(_sample_rest_kernel, pos0=pos0, final=final),
        grid=(1,),
        in_specs=[full(a) for a in (ag, cd, ob, x)] + [st_in(a) for a in (sa, sc, sp)] + [dense_spec(a) for a in dense],
        out_specs=[full(x)] + [st_out(a) for a in (sa, sc, sp)],
        out_shape=[jax.ShapeDtypeStruct(x.shape, F32)] + [jax.ShapeDtypeStruct(a.shape[1:], F32) for a in (sa, sc, sp)],
        compiler_params=_cparams("arbitrary"),
        name="sample_rest",
    )(ag, cd, ob, x, sa, sc, sp, *dense)


def _block_diag(pool_w):
    g, c, e = pool_w.shape
    eye = jnp.eye(g, dtype=pool_w.dtype)
    return (pool_w[:, :, None, :] * eye[:, None, :, None]).reshape(g * c, g * e)


def kernel(x_prompt, x_sample, cache_win_k, cache_win_v, state_conv_a, state_conv_c, state_pool, w_in, conv_a_w, conv_a_b, ln_a_g, ln_a_b, conv_c_w, pool_w, pool_scale, w_out, norm1_g, norm2_g, w_gu, w_down, final_g):
    depth = w_in.shape[0]
    n, s, d = x_prompt.shape
    ns, ts, _ = x_sample.shape
    assert ts == 1
    gw = GROUP_W
    tm = 512

    row = lambda a: a.reshape(1, -1)
    to_cm = lambda a: jnp.transpose(a, (0, 1, 3, 4, 2)).reshape(a.shape[0], a.shape[1], gw, a.shape[2])
    from_cm = lambda a: jnp.transpose(a.reshape(a.shape[0], a.shape[1], N_HEADS, HEAD_DIM, a.shape[3]), (0, 1, 4, 2, 3))
    swap = lambda a: jnp.transpose(a, (0, 2, 1, 3))
    kbuf, vbuf = to_cm(cache_win_k), to_cm(cache_win_v)
    sa_all, sc_all, sp_all = swap(state_conv_a), swap(state_conv_c), swap(state_pool)

    xp = x_prompt.reshape(n * s, d)
    xs = x_sample.reshape(ns, d)
    gf = row(final_g)
    kt_p = vt_p = kt_s = vt_s = None
    st_p = [[] for _ in range(3)]
    st_s = [[] for _ in range(3)]
    w_in, w_out, w_gu, w_down = (w.astype(BF16) for w in (w_in, w_out, w_gu, w_down))
    for l in range(depth):
        pool_bd = _block_diag(pool_w[l]).astype(BF16)
        small = (conv_a_w[l], row(conv_a_b[l]), row(ln_a_g[l]), row(ln_a_b[l]), conv_c_w[l], pool_bd, row(pool_scale[l]))
        final = l == depth - 1

        ag, q, k, v, cd, kt_p, vt_p = _inproj(xp, row(norm1_g[l]), w_in, l, tm, state=(depth, n, kt_p, vt_p))
        ob = _attn_prompt(q.reshape(n, s, gw), k.reshape(n, s, gw), v.reshape(n, s, gw))
        xp, *states = _mix_prompt(ag, cd, ob.reshape(n * s, gw), xp, *small, w_out, l, n, tm)
        xp = _ffn_prompt(xp, row(norm2_g[l]), w_gu, w_down, gf, l, tm, final)
        for lst, a in zip(st_p, states):
            lst.append(a)

        ag, q, k, v, cd = _inproj(xs, row(norm1_g[l]), w_in, l, ns)
        ob, kt_s, vt_s = _cache_step(q.reshape(ns, 1, gw), k.reshape(ns, 1, gw), v.reshape(ns, 1, gw),
                                     kbuf, vbuf, l, kt_s, vt_s)
        xs, *states = _sample_rest(ag, cd, ob.reshape(ns, gw), xs, sa_all, sc_all, sp_all, *small, w_out,
                                   row(norm2_g[l]), w_gu, w_down, gf, l, PAST_LEN, final)
        for lst, a in zip(st_s, states):
            lst.append(a)

    y_prompt = xp.reshape(n, s, d)
    y_sample = xs.reshape(ns, ts, d)
    return (y_prompt, y_sample, from_cm(kt_p), from_cm(vt_p), *[jnp.stack(a, axis=0) for a in st_p],
            from_cm(kt_s), from_cm(vt_s), *[swap(jnp.stack(a, axis=0)) for a in st_s])
```

```python
import functools
import math

import jax
import jax.numpy as jnp
from jax import lax
from jax.experimental import pallas as pl
from jax.experimental.pallas import tpu as pltpu

F32 = jnp.float32
BF16 = jnp.bfloat16

GROUP_W = 256
HEAD_DIM = 64
N_HEADS = GROUP_W // HEAD_DIM
CONV_A_WIDTH = 31
CONV_C_WIDTH = 3
POOL_WINDOWS = (2, 4, 8, 16)
POOL_STATE = max(POOL_WINDOWS) - 1
DIL_CONFIGS = ((128, 1), (512, 4), (2048, 16))
Q_BLOCK = 128
PAST_LEN = 16384
ATTN_SCALE = 1.0 / math.sqrt(HEAD_DIM)
EPS = 1e-6
NEG = -1e30

VMEM_LIMIT_BYTES = 56 * 1024 * 1024


def _cparams(*sem):
    return pltpu.CompilerParams(dimension_semantics=sem, vmem_limit_bytes=VMEM_LIMIT_BYTES)


def _rmsnorm(x, g):
    return x * lax.rsqrt(jnp.mean(x * x, axis=-1, keepdims=True) + EPS) * g


def _layernorm(x, g, b):
    mu = jnp.mean(x, axis=-1, keepdims=True)
    xc = x - mu
    return xc * lax.rsqrt(jnp.mean(xc * xc, axis=-1, keepdims=True) + EPS) * g + b


def _layer_spec(a, layer):
    return pl.BlockSpec((1,) + a.shape[1:], lambda *_: (layer,) + (0,) * (a.ndim - 1))


def _dot(a, b):
    return jnp.dot(a, b, preferred_element_type=F32)


def _head_of_lane(shape, dim):
    return lax.broadcasted_iota(jnp.int32, shape, dim) // HEAD_DIM


def _inproj_kernel(x_ref, g_ref, w_ref, ag_ref, q_ref, k_ref, v_ref, cd_ref):
    w_ref = w_ref.at[0]
    h = _rmsnorm(x_ref[...], g_ref[...]).astype(BF16)
    gw = GROUP_W
    ag_ref[...] = _dot(h, w_ref[:, 0:2 * gw])
    q_ref[...] = _dot(h, w_ref[:, 2 * gw:3 * gw]) * ATTN_SCALE
    k_ref[...] = _dot(h, w_ref[:, 3 * gw:4 * gw])
    v_ref[...] = _dot(h, w_ref[:, 4 * gw:5 * gw])
    cd_ref[...] = _dot(h, w_ref[:, 5 * gw:9 * gw])


def _inproj(x, g, w_in, layer):
    t, d = x.shape
    gw = GROUP_W
    full = lambda a: pl.BlockSpec(a.shape, lambda i: (0,) * a.ndim)
    widths = (2 * gw, gw, gw, gw, 4 * gw)
    return pl.pallas_call(
        _inproj_kernel,
        grid=(1,),
        in_specs=[full(x), full(g), _layer_spec(w_in, layer)],
        out_specs=[pl.BlockSpec((t, w), lambda i: (0, 0)) for w in widths],
        out_shape=[jax.ShapeDtypeStruct((t, w), F32) for w in widths],
        compiler_params=_cparams("arbitrary"),
        name="inproj",
    )(x, g, w_in)


HEAD_GROUPS = ((0, 1), (2, 3))
UNITS_PER_STEP = 4


def _attn_kernel(q_ref, k_ref, v_ref, ob_ref, q_scr, k_scr, v_scr, qc, kc, vtc, o_scr, l_scr, *, seq):
    qb = Q_BLOCK
    hw = GROUP_W // 2
    for src, dst in ((q_ref, q_scr), (k_ref, k_scr), (v_ref, v_scr)):
        for half in range(2):
            dst[half] = src[0, :, half * hw:(half + 1) * hw]
    kc[0:qb, :] = jnp.zeros((qb, GROUP_W), BF16)
    vtc[:, 0:qb] = jnp.zeros((GROUP_W, qb), BF16)

    gh = len(HEAD_GROUPS[0])
    head_q = _head_of_lane((qb, GROUP_W), 1)
    kk = lax.broadcasted_iota(jnp.int32, (2 * qb, gh * qb), 0)
    qi = lax.broadcasted_iota(jnp.int32, (2 * qb, gh * qb), 1) % qb
    band = jnp.logical_or(jnp.logical_and(kk < qb, kk >= qi), jnp.logical_and(kk >= qb, kk - qb <= qi))

    for c, (_, dil) in enumerate(DIL_CONFIGS):
        n_blocks = seq // (dil * qb)
        n_steps = seq // qb // UNITS_PER_STEP
        rows = lambda s0, dil=dil: pl.ds(s0, qb, stride=dil) if dil > 1 else pl.ds(s0, qb)

        def block_start(u, dil=dil, n_blocks=n_blocks):
            start = u // n_blocks + (u % n_blocks) * (qb * dil)
            return pl.multiple_of(start, qb) if dil == 1 else start

        def stage(u, carry, rows=rows, block_start=block_start):
            start = block_start(u)
            load = lambda scr: jnp.concatenate([scr[0, rows(start), :], scr[1, rows(start), :]], axis=1)
            dst = pl.multiple_of(u * qb, qb)
            qc[pl.ds(dst, qb), :] = load(q_scr)
            kc[pl.ds(dst + qb, qb), :] = load(k_scr).astype(BF16)
            vtc[:, pl.ds(dst + qb, qb)] = load(v_scr).T.astype(BF16)
            return carry

        lax.fori_loop(0, seq // qb, stage, 0, unroll=4)

        def unit(u, carry, c=c, n_blocks=n_blocks, rows=rows, block_start=block_start):
            us = [u * UNITS_PER_STEP + i for i in range(UNITS_PER_STEP)]
            w0s = [pl.multiple_of(ui * qb, qb) for ui in us]
            chains = [(i, g) for i in range(UNITS_PER_STEP) for g in range(len(HEAD_GROUPS))]
            ch = gh * HEAD_DIM
            scores = []
            for i, g in chains:
                q = qc[pl.ds(w0s[i], qb), :]
                kwin = kc[pl.ds(w0s[i], 2 * qb), :]
                qm = jnp.concatenate([jnp.where(head_q == h, q, 0.0) for h in HEAD_GROUPS[g]], axis=0).astype(BF16)
                scores.append(lax.dot_general(kwin, qm, (((1,), (1,)), ((), ())), preferred_element_type=F32))
            probs, dens, lses = [], [], []
            for (i, g), s in zip(chains, scores):
                valid = jnp.logical_and(band, jnp.logical_or(kk >= qb, us[i] % n_blocks > 0))
                s = jnp.where(valid, s, NEG)
                m = jnp.max(s, axis=0, keepdims=True)
                p = jnp.exp(s - m)
                den = jnp.sum(p, axis=0, keepdims=True)
                probs.append(p.astype(BF16))
                dens.append(den)
                lses.append(m + jnp.log(den))
            nums = [_dot(vtc[g * ch:(g + 1) * ch, pl.ds(w0s[i], 2 * qb)], p) for (i, g), p in zip(chains, probs)]
            for i in range(UNITS_PER_STEP):
                o_t, l_t = [], []
                for g in range(len(HEAD_GROUPS)):
                    n = chains.index((i, g))
                    for j in range(gh):
                        cols = slice(j * qb, (j + 1) * qb)
                        o_t.append(nums[n][j * HEAD_DIM:(j + 1) * HEAD_DIM, cols] / dens[n][:, cols])
                        l_t.append(jnp.broadcast_to(lses[n][:, cols], (HEAD_DIM, qb)))
                o = jnp.concatenate(o_t, axis=0).T
                l = jnp.concatenate(l_t, axis=0).T
                start = block_start(us[i])
                for half in range(2):
                    o_scr[c, half, rows(start), :] = o[:, half * hw:(half + 1) * hw]
                    l_scr[c, half, rows(start), :] = l[:, half * hw:(half + 1) * hw]
            return carry

        lax.fori_loop(0, n_steps, unit, 0)

    for half in range(2):
        l0, l1, l2 = l_scr[0, half], l_scr[1, half], l_scr[2, half]
        m = jnp.maximum(jnp.maximum(l0, l1), l2)
        e0, e1, e2 = jnp.exp(l0 - m), jnp.exp(l1 - m), jnp.exp(l2 - m)
        ob_ref[0, :, half * hw:(half + 1) * hw] = (
            (e0 * o_scr[0, half] + e1 * o_scr[1, half] + e2 * o_scr[2, half]) / (e0 + e1 + e2)).astype(ob_ref.dtype)


def _attn_prompt(q, k, v):
    n, s, w = q.shape
    blk = pl.BlockSpec((1, s, w), lambda i: (i, 0, 0))
    return pl.pallas_call(
        functools.partial(_attn_kernel, seq=s),
        grid=(n,),
        in_specs=[blk, blk, blk],
        out_specs=blk,
        out_shape=jax.ShapeDtypeStruct((n, s, w), BF16),
        scratch_shapes=[pltpu.VMEM((2, s, w // 2), F32)] * 3
        + [pltpu.VMEM((s, w), F32), pltpu.VMEM((Q_BLOCK + s, w), BF16), pltpu.VMEM((w, Q_BLOCK + s), BF16)]
        + [pltpu.VMEM((len(DIL_CONFIGS), 2, s, w // 2), F32)] * 2,
        compiler_params=_cparams("parallel"),
        name="attn_prompt",
    )(q, k, v)


def _pool_select(s2, s4, s8, s16):
    grp = _head_of_lane(s2.shape, s2.ndim - 1)
    return jnp.where(grp == 0, s2, jnp.where(grp == 1, s4, jnp.where(grp == 2, s8, s16)))


def _pool_window_lanes(shape):
    grp = _head_of_lane(shape, len(shape) - 1)
    return jnp.where(grp == 0, 2, jnp.where(grp == 1, 4, jnp.where(grp == 2, 8, 16)))


def _out_proj(ya, ob, yc, yd, w_out_ref):
    gw = GROUP_W
    acc = _dot(ya.astype(BF16), w_out_ref[0:gw, :])
    acc += _dot(ob.astype(BF16), w_out_ref[gw:2 * gw, :])
    acc += _dot(yc.astype(BF16), w_out_ref[2 * gw:3 * gw, :])
    acc += _dot(yd.astype(BF16), w_out_ref[3 * gw:4 * gw, :])
    return acc


def _ffn(x, g2, w_gu_ref, w_down_ref, ff_chunk):
    ff = w_down_ref.shape[0]
    h = _rmsnorm(x, g2).astype(BF16)
    acc = jnp.zeros_like(x)
    for c in range(ff // ff_chunk):
        lo = c * ff_chunk
        g = _dot(h, w_gu_ref[:, lo:lo + ff_chunk])
        u = _dot(h, w_gu_ref[:, ff + lo:ff + lo + ff_chunk])
        acc += _dot((jax.nn.silu(g) * u).astype(BF16), w_down_ref[lo:lo + ff_chunk, :])
    return x + acc


A_PAD = 32
C_PAD = 8
P_PAD = 16
SUBLANES = 8


def _causal_conv(buf, w_ref, width, pad, tm):
    base = pad - SUBLANES
    rows = tm + SUBLANES
    y = None
    for a in range(min(SUBLANES, width)):
        z = None
        for lag in range(a, width, SUBLANES):
            term = w_ref[width - 1 - lag:width - lag, :] * buf[base - (lag - a):base - (lag - a) + rows, :]
            z = term if z is None else z + term
        z = pltpu.roll(z, a, 0) if a else z
        y = z if y is None else y + z
    return y[SUBLANES:SUBLANES + tm]


def _inmix_kernel(x_ref, g_ref, w_ref, caw_ref, cab_ref, lng_ref, lnb_ref, ccw_ref, pw_ref, ps_ref, *refs, n_alias, tm):
    (q_ref, k_ref, v_ref, ya_ref, yc_ref, yd_ref, sta_ref, stc_ref, stp_ref, kt_ref, vt_ref,
     abuf, cbuf, pbuf) = refs[n_alias:]
    gw = GROUP_W
    w_ref = w_ref.at[0]
    j = pl.program_id(1)

    @pl.when(j == 0)
    def _():
        abuf[0:A_PAD, :] = jnp.zeros((A_PAD, gw), F32)
        cbuf[0:C_PAD, :] = jnp.zeros((C_PAD, gw), F32)
        pbuf[0:P_PAD, :] = jnp.zeros((P_PAD, gw), F32)

    h = _rmsnorm(x_ref[...], g_ref[...]).astype(BF16)

    abuf[A_PAD:A_PAD + tm, :] = _dot(h, w_ref[:, 0:gw]) * jax.nn.sigmoid(_dot(h, w_ref[:, gw:2 * gw]))
    acc = _causal_conv(abuf, caw_ref, CONV_A_WIDTH, A_PAD, tm)
    ya_ref[...] = jax.nn.silu(_layernorm(acc + cab_ref[...], lng_ref[...], lnb_ref[...])).astype(BF16)

    q_ref[...] = _dot(h, w_ref[:, 2 * gw:3 * gw]) * ATTN_SCALE
    k = _dot(h, w_ref[:, 3 * gw:4 * gw])
    v = _dot(h, w_ref[:, 4 * gw:5 * gw])
    k_ref[...] = k
    v_ref[...] = v
    kt_ref[0, 0] = k.T
    vt_ref[0, 0] = v.T

    cbuf[C_PAD:C_PAD + tm, :] = _dot(h, w_ref[:, 7 * gw:8 * gw]) * _dot(h, w_ref[:, 5 * gw:6 * gw])
    yc_ref[...] = (_dot(h, w_ref[:, 6 * gw:7 * gw]) * _causal_conv(cbuf, ccw_ref, CONV_C_WIDTH, C_PAD, tm)).astype(BF16)

    u = _dot(h, w_ref[:, 8 * gw:9 * gw])
    pbuf[P_PAD:P_PAD + tm, :] = u
    s1 = pbuf[...]
    s2 = s1 + pltpu.roll(s1, 1, 0)
    s4 = s2 + pltpu.roll(s2, 2, 0)
    s8 = s4 + pltpu.roll(s4, 4, 0)
    s16 = s8 + pltpu.roll(s8, 8, 0)
    win = _pool_select(s2, s4, s8, s16)[P_PAD:P_PAD + tm]
    pos = j * tm + lax.broadcasted_iota(jnp.int32, (tm, gw), 0)
    cnt = jnp.minimum(_pool_window_lanes((tm, gw)), pos + 1).astype(F32)
    yd_ref[...] = (_dot((win / cnt - u).astype(BF16), pw_ref[...]) * ps_ref[...]).astype(BF16)

    @pl.when(j == pl.num_programs(1) - 1)
    def _():
        sta_ref[0] = abuf[A_PAD + tm - (CONV_A_WIDTH - 1):A_PAD + tm, :]
        stc_ref[0] = cbuf[C_PAD + tm - (CONV_C_WIDTH - 1):C_PAD + tm, :]
        stp_ref[0] = pbuf[P_PAD + tm - POOL_STATE:P_PAD + tm, :]

    abuf[0:A_PAD, :] = abuf[tm:tm + A_PAD, :]
    cbuf[0:C_PAD, :] = cbuf[tm:tm + C_PAD, :]
    pbuf[0:P_PAD, :] = pbuf[tm:tm + P_PAD, :]


def _inmix_prompt(x, g, w_in, caw, cab, lng, lnb, ccw, pool_bd, ps, layer, n, tm, kt_all, vt_all):
    t, d = x.shape
    gw = GROUP_W
    depth = w_in.shape[0]
    spb = t // n // tm
    row = lambda w: pl.BlockSpec((tm, w), lambda i, j: (i * spb + j, 0))
    full = lambda a: pl.BlockSpec(a.shape, lambda i, j: (0,) * a.ndim)
    st = lambda r: pl.BlockSpec((1, r, gw), lambda i, j: (i, 0, 0))
    st_rows = (CONV_A_WIDTH - 1, CONV_C_WIDTH - 1, POOL_STATE)
    cm = pl.BlockSpec((1, 1, gw, tm), lambda i, j: (layer, i, 0, j))
    small = (caw, cab, lng, lnb, ccw, pool_bd, ps)
    in_specs = [row(d), full(g), _layer_spec(w_in, layer)] + [full(a) for a in small]
    args, aliases = [x, g, w_in, *small], {}
    if kt_all is not None:
        in_specs += [pl.BlockSpec(memory_space=pl.ANY)] * 2
        aliases = {len(args): 9, len(args) + 1: 10}
        args += [kt_all, vt_all]
    return pl.pallas_call(
        functools.partial(_inmix_kernel, n_alias=len(aliases), tm=tm),
        grid=(n, spb),
        in_specs=in_specs,
        out_specs=[row(gw)] * 6 + [st(r) for r in st_rows] + [cm, cm],
        out_shape=[jax.ShapeDtypeStruct((t, gw), F32)] * 3 + [jax.ShapeDtypeStruct((t, gw), BF16)] * 3
        + [jax.ShapeDtypeStruct((n, r, gw), F32) for r in st_rows]
        + [jax.ShapeDtypeStruct((depth, n, gw, t // n), F32)] * 2,
        input_output_aliases=aliases,
        scratch_shapes=[pltpu.VMEM((A_PAD + tm, gw), F32), pltpu.VMEM((C_PAD + tm, gw), F32),
                        pltpu.VMEM((P_PAD + tm, gw), F32)],
        compiler_params=_cparams("parallel", "arbitrary"),
        name="inmix_prompt",
    )(*args)


def _outffn_kernel(x_ref, ya_ref, ob_ref, yc_ref, yd_ref, wo_ref, g2_ref, wgu_ref, wd_ref, gf_ref, o_ref, *, ff_chunk, final):
    x1 = x_ref[...] + _out_proj(ya_ref[...], ob_ref[...], yc_ref[...], yd_ref[...], wo_ref.at[0])
    x2 = _ffn(x1, g2_ref[...], wgu_ref.at[0], wd_ref.at[0], ff_chunk)
    o_ref[...] = _rmsnorm(x2, gf_ref[...]) if final else x2


def _outffn_prompt(x, ya, ob, yc, yd, w_out, g2, w_gu, w_down, gf, layer, tm, final):
    t, d = x.shape
    row = lambda a: pl.BlockSpec((tm, a.shape[1]), lambda i: (i, 0))
    full = lambda a: pl.BlockSpec(a.shape, lambda i: (0,) * a.ndim)
    return pl.pallas_call(
        functools.partial(_outffn_kernel, ff_chunk=256, final=final),
        grid=(t // tm,),
        in_specs=[row(a) for a in (x, ya, ob, yc, yd)]
        + [_layer_spec(w_out, layer), full(g2), _layer_spec(w_gu, layer), _layer_spec(w_down, layer), full(gf)],
        out_specs=row(x),
        out_shape=jax.ShapeDtypeStruct((t, d), F32),
        compiler_params=_cparams("parallel"),
        name="outffn_prompt",
    )(x, ya, ob, yc, yd, w_out, g2, w_gu, w_down, gf)


def _cache_kernel(q_ref, kn_ref, vn_ref, kb_ref, vb_ref, *refs, n_alias):
    ob_ref, ko_ref, vo_ref = refs[n_alias:]
    gw = GROUP_W
    buf = kb_ref.shape[-1]
    ident = lax.broadcasted_iota(jnp.int32, (gw, gw), 0) == lax.broadcasted_iota(jnp.int32, (gw, gw), 1)
    to_col = lambda row: jnp.sum(jnp.where(ident, row, 0.0), axis=1, keepdims=True)
    to_row = lambda col: jnp.sum(jnp.where(ident, col, 0.0), axis=0, keepdims=True)
    qc, kc, vc = to_col(q_ref[0]), to_col(kn_ref[0]), to_col(vn_ref[0])
    kb, vb = kb_ref[0, 0], vb_ref[0, 0]
    last = lax.broadcasted_iota(jnp.int32, (gw, buf), 1) == buf - 1
    ko_ref[0, 0] = jnp.where(last, kc, pltpu.roll(kb, buf - 1, 1))
    vo_ref[0, 0] = jnp.where(last, vc, pltpu.roll(vb, buf - 1, 1))

    def heads(a):
        return jnp.sum(a.reshape(N_HEADS, HEAD_DIM, a.shape[-1]), axis=1)

    def spread(a):
        return jnp.broadcast_to(a[:, None, :], (N_HEADS, HEAD_DIM, a.shape[-1])).reshape(gw, a.shape[-1])

    s_all = heads(kb * qc)
    s_new = heads(kc * qc)
    outs, lses = [], []
    for win, dil in DIL_CONFIGS:
        lo = buf - win
        s = s_all[:, lo:]
        if dil > 1:
            back = win - lax.broadcasted_iota(jnp.int32, s.shape, 1)
            s = jnp.where(back % dil == 0, s, NEG)
        m = jnp.maximum(jnp.max(s, axis=1, keepdims=True), s_new)
        p, p_new = jnp.exp(s - m), jnp.exp(s_new - m)
        den = jnp.sum(p, axis=1, keepdims=True) + p_new
        pv = jnp.sum(vb[:, lo:] * spread(p), axis=1, keepdims=True) + vc * spread(p_new)
        outs.append(pv / spread(den))
        lses.append(spread(m + jnp.log(den)))
    m = jnp.maximum(jnp.maximum(lses[0], lses[1]), lses[2])
    es = [jnp.exp(l - m) for l in lses]
    ob_ref[0] = to_row((es[0] * outs[0] + es[1] * outs[1] + es[2] * outs[2]) / (es[0] + es[1] + es[2]))


def _cache_step(q, kn, vn, kbuf, vbuf, layer, k_all=None, v_all=None):
    depth, n, w, buf = kbuf.shape
    assert max(win for win, _ in DIL_CONFIGS) <= buf
    tok = pl.BlockSpec((1, 1, w), lambda i: (i, 0, 0))
    blk = pl.BlockSpec((1, 1, w, buf), lambda i: (layer, i, 0, 0))
    in_specs, args, aliases = [tok, tok, tok, blk, blk], [q, kn, vn, kbuf, vbuf], {}
    if k_all is not None:
        in_specs += [pl.BlockSpec(memory_space=pl.ANY)] * 2
        args += [k_all, v_all]
        aliases = {5: 1, 6: 2}
    return pl.pallas_call(
        functools.partial(_cache_kernel, n_alias=len(aliases)),
        grid=(n,),
        in_specs=in_specs,
        out_specs=[tok, blk, blk],
        out_shape=[jax.ShapeDtypeStruct((n, 1, w), F32)] + [jax.ShapeDtypeStruct(kbuf.shape, F32)] * 2,
        input_output_aliases=aliases,
        compiler_params=_cparams("parallel"),
        name="cache_step",
    )(*args)


def _sample_rest_kernel(ag_ref, cd_ref, ob_ref, x_ref, sa_ref, sc_ref, sp_ref, caw_ref, cab_ref, lng_ref, lnb_ref,
                        ccw_ref, pw_ref, ps_ref, wo_ref, g2_ref, wgu_ref, wd_ref, gf_ref,
                        y_ref, na_ref, nc_ref, np_ref, *, pos0, final):
    gw = GROUP_W
    ga = ag_ref[:, 0:gw] * jax.nn.sigmoid(ag_ref[:, gw:2 * gw])
    na = CONV_A_WIDTH - 1
    acc = caw_ref[na:na + 1, :] * ga
    for t in range(na):
        acc += caw_ref[t:t + 1, :] * sa_ref[0, t]
    ya = jax.nn.silu(_layernorm(acc + cab_ref[...], lng_ref[...], lnb_ref[...]))
    na_ref[0:na - 1] = sa_ref[0, 1:na]
    na_ref[na - 1] = ga

    cx = cd_ref[:, 2 * gw:3 * gw] * cd_ref[:, 0:gw]
    nc = CONV_C_WIDTH - 1
    acc = ccw_ref[nc:nc + 1, :] * cx
    for t in range(nc):
        acc += ccw_ref[t:t + 1, :] * sc_ref[0, t]
    yc = cd_ref[:, gw:2 * gw] * acc
    nc_ref[0:nc - 1] = sc_ref[0, 1:nc]
    nc_ref[nc - 1] = cx

    u = cd_ref[:, 3 * gw:4 * gw]
    npl = POOL_STATE
    back = lambda i: sp_ref[0, npl - i]
    s2 = u + back(1)
    s4 = s2 + back(2) + back(3)
    s8 = s4 + back(4) + back(5) + back(6) + back(7)
    s16 = s8
    for i in range(8, 16):
        s16 = s16 + back(i)
    cnt = jnp.minimum(_pool_window_lanes(u.shape), pos0 + 1).astype(F32)
    yd = _dot((_pool_select(s2, s4, s8, s16) / cnt - u).astype(BF16), pw_ref[...]) * ps_ref[...]
    np_ref[0:npl - 1] = sp_ref[0, 1:npl]
    np_ref[npl - 1] = u

    x1 = x_ref[...] + _out_proj(ya, ob_ref[...], yc, yd, wo_ref.at[0])
    x2 = _ffn(x1, g2_ref[...], wgu_ref.at[0], wd_ref.at[0], 256)
    y_ref[...] = _rmsnorm(x2, gf_ref[...]) if final else x2


def _sample_rest(ag, cd, ob, x, sa, sc, sp, caw, cab, lng, lnb, ccw, pool_bd, ps, w_out, g2, w_gu, w_down, gf, layer, pos0, final):
    full = lambda a: pl.BlockSpec(a.shape, lambda i: (0,) * a.ndim)
    st_in = lambda a: pl.BlockSpec((1,) + a.shape[1:], lambda i: (layer, 0, 0, 0))
    st_out = lambda a: pl.BlockSpec(a.shape[1:], lambda i: (0, 0, 0))
    dense = (caw, cab, lng, lnb, ccw, pool_bd, ps, w_out, g2, w_gu, w_down, gf)
    stacked = (w_out, w_gu, w_down)
    dense_spec = lambda a: _layer_spec(a, layer) if any(a is w for w in stacked) else full(a)
    return pl.pallas_call(
        functools.partial(_sample_rest_kernel, pos0=pos0, final=final),
        grid=(1,),
        in_specs=[full(a) for a in (ag, cd, ob, x)] + [st_in(a) for a in (sa, sc, sp)] + [dense_spec(a) for a in dense],
        out_specs=[full(x)] + [st_out(a) for a in (sa, sc, sp)],
        out_shape=[jax.ShapeDtypeStruct(x.shape, F32)] + [jax.ShapeDtypeStruct(a.shape[1:], F32) for a in (sa, sc, sp)],
        compiler_params=_cparams("arbitrary"),
        name="sample_rest",
    )(ag, cd, ob, x, sa, sc, sp, *dense)


def _block_diag(pool_w):
    g, c, e = pool_w.shape
    eye = jnp.eye(g, dtype=pool_w.dtype)
    return (pool_w[:, :, None, :] * eye[:, None, :, None]).reshape(g * c, g * e)


def kernel(x_prompt, x_sample, cache_win_k, cache_win_v, state_conv_a, state_conv_c, state_pool, w_in, conv_a_w, conv_a_b, ln_a_g, ln_a_b, conv_c_w, pool_w, pool_scale, w_out, norm1_g, norm2_g, w_gu, w_down, final_g):
    depth = w_in.shape[0]
    n, s, d = x_prompt.shape
    ns, ts, _ = x_sample.shape
    assert ts == 1
    gw = GROUP_W
    tm = 512

    row = lambda a: a.reshape(1, -1)
    to_cm = lambda a: jnp.transpose(a, (0, 1, 3, 4, 2)).reshape(a.shape[0], a.shape[1], gw, a.shape[2])
    from_cm = lambda a: jnp.transpose(a.reshape(a.shape[0], a.shape[1], N_HEADS, HEAD_DIM, a.shape[3]), (0, 1, 4, 2, 3))
    swap = lambda a: jnp.transpose(a, (0, 2, 1, 3))
    kbuf, vbuf = to_cm(cache_win_k), to_cm(cache_win_v)
    sa_all, sc_all, sp_all = swap(state_conv_a), swap(state_conv_c), swap(state_pool)

    xp = x_prompt.reshape(n * s, d)
    xs = x_sample.reshape(ns, d)
    gf = row(final_g)
    kt_p = vt_p = kt_s = vt_s = None
    st_p = [[] for _ in range(3)]
    st_s = [[] for _ in range(3)]
    w_in, w_out, w_gu, w_down = (w.astype(BF16) for w in (w_in, w_out, w_gu, w_down))
    for l in range(depth):
        pool_bd = _block_diag(pool_w[l]).astype(BF16)
        small = (conv_a_w[l], row(conv_a_b[l]), row(ln_a_g[l]), row(ln_a_b[l]), conv_c_w[l], pool_bd, row(pool_scale[l]))
        final = l == depth - 1

        q, k, v, ya, yc, yd, *states, kt_p, vt_p = _inmix_prompt(xp, row(norm1_g[l]), w_in, *small, l, n, tm, kt_p, vt_p)
        ob = _attn_prompt(q.reshape(n, s, gw), k.reshape(n, s, gw), v.reshape(n, s, gw)).reshape(n * s, gw)
        xp = _outffn_prompt(xp, ya, ob, yc, yd, w_out, row(norm2_g[l]), w_gu, w_down, gf, l, tm, final)
        for lst, a in zip(st_p, states):
            lst.append(a)

        ag, q, k, v, cd = _inproj(xs, row(norm1_g[l]), w_in, l)
        ob, kt_s, vt_s = _cache_step(q.reshape(ns, 1, gw), k.reshape(ns, 1, gw), v.reshape(ns, 1, gw),
                                     kbuf, vbuf, l, kt_s, vt_s)
        xs, *states = _sample_rest(ag, cd, ob.reshape(ns, gw), xs, sa_all, sc_all, sp_all, *small, w_out,
                                   row(norm2_g[l]), w_gu, w_down, gf, l, PAST_LEN, final)
        for lst, a in zip(st_s, states):
            lst.append(a)

    y_prompt = xp.reshape(n, s, d)
    y_sample = xs.reshape(ns, ts, d)
    return (y_prompt, y_sample, from_cm(kt_p), from_cm(vt_p), *[jnp.stack(a, axis=0) for a in st_p],
            from_cm(kt_s), from_cm(vt_s), *[swap(jnp.stack(a, axis=0)) for a in st_s])
```

```python
import functools
import math

import jax
import jax.numpy as jnp
from jax import lax
from jax.experimental import pallas as pl
from jax.experimental.pallas import tpu as pltpu

F32 = jnp.float32
BF16 = jnp.bfloat16

GROUP_W = 256
HEAD_DIM = 64
N_HEADS = GROUP_W // HEAD_DIM
CONV_A_WIDTH = 31
CONV_C_WIDTH = 3
POOL_WINDOWS = (2, 4, 8, 16)
POOL_STATE = max(POOL_WINDOWS) - 1
DIL_CONFIGS = ((128, 1), (512, 4), (2048, 16))
Q_BLOCK = 128
PAST_LEN = 16384
ATTN_SCALE = 1.0 / math.sqrt(HEAD_DIM)
EPS = 1e-6
NEG = -1e30

VMEM_LIMIT_BYTES = 56 * 1024 * 1024


def _cparams(*sem):
    return pltpu.CompilerParams(dimension_semantics=sem, vmem_limit_bytes=VMEM_LIMIT_BYTES)


def _rmsnorm(x, g):
    return x * lax.rsqrt(jnp.mean(x * x, axis=-1, keepdims=True) + EPS) * g


def _layernorm(x, g, b):
    mu = jnp.mean(x, axis=-1, keepdims=True)
    xc = x - mu
    return xc * lax.rsqrt(jnp.mean(xc * xc, axis=-1, keepdims=True) + EPS) * g + b


def _layer_spec(a, layer):
    return pl.BlockSpec((1,) + a.shape[1:], lambda *_: (layer,) + (0,) * (a.ndim - 1))


def _dot(a, b):
    return jnp.dot(a, b, preferred_element_type=F32)


def _head_of_lane(shape, dim):
    return lax.broadcasted_iota(jnp.int32, shape, dim) // HEAD_DIM


def _inproj_kernel(x_ref, g_ref, w_ref, ag_ref, q_ref, k_ref, v_ref, cd_ref):
    w_ref = w_ref.at[0]
    h = _rmsnorm(x_ref[...], g_ref[...]).astype(BF16)
    gw = GROUP_W
    ag_ref[...] = _dot(h, w_ref[:, 0:2 * gw])
    q_ref[...] = _dot(h, w_ref[:, 2 * gw:3 * gw]) * ATTN_SCALE
    k_ref[...] = _dot(h, w_ref[:, 3 * gw:4 * gw])
    v_ref[...] = _dot(h, w_ref[:, 4 * gw:5 * gw])
    cd_ref[...] = _dot(h, w_ref[:, 5 * gw:9 * gw])


def _inproj(x, g, w_in, layer):
    t, d = x.shape
    gw = GROUP_W
    full = lambda a: pl.BlockSpec(a.shape, lambda i: (0,) * a.ndim)
    widths = (2 * gw, gw, gw, gw, 4 * gw)
    return pl.pallas_call(
        _inproj_kernel,
        grid=(1,),
        in_specs=[full(x), full(g), _layer_spec(w_in, layer)],
        out_specs=[pl.BlockSpec((t, w), lambda i: (0, 0)) for w in widths],
        out_shape=[jax.ShapeDtypeStruct((t, w), F32) for w in widths],
        compiler_params=_cparams("arbitrary"),
        name="inproj",
    )(x, g, w_in)


HEAD_GROUPS = ((0, 1), (2, 3))
UNITS_PER_STEP = 4


def _attn_kernel(q_ref, k_ref, v_ref, ob_ref, q_scr, k_scr, v_scr, qc, kc, vtc, o_scr, l_scr, *, seq):
    qb = Q_BLOCK
    hw = GROUP_W // 2
    for src, dst in ((q_ref, q_scr), (k_ref, k_scr), (v_ref, v_scr)):
        for half in range(2):
            dst[half] = src[0, :, half * hw:(half + 1) * hw]
    kc[0:qb, :] = jnp.zeros((qb, GROUP_W), BF16)
    vtc[:, 0:qb] = jnp.zeros((GROUP_W, qb), BF16)

    gh = len(HEAD_GROUPS[0])
    head_q = _head_of_lane((qb, GROUP_W), 1)
    kk = lax.broadcasted_iota(jnp.int32, (2 * qb, gh * qb), 0)
    qi = lax.broadcasted_iota(jnp.int32, (2 * qb, gh * qb), 1) % qb
    band = jnp.logical_or(jnp.logical_and(kk < qb, kk >= qi), jnp.logical_and(kk >= qb, kk - qb <= qi))

    for c, (_, dil) in enumerate(DIL_CONFIGS):
        n_blocks = seq // (dil * qb)
        n_steps = seq // qb // UNITS_PER_STEP
        rows = lambda s0, dil=dil: pl.ds(s0, qb, stride=dil) if dil > 1 else pl.ds(s0, qb)

        def block_start(u, dil=dil, n_blocks=n_blocks):
            start = u // n_blocks + (u % n_blocks) * (qb * dil)
            return pl.multiple_of(start, qb) if dil == 1 else start

        def stage(u, carry, rows=rows, block_start=block_start):
            start = block_start(u)
            load = lambda scr: jnp.concatenate([scr[0, rows(start), :], scr[1, rows(start), :]], axis=1)
            dst = pl.multiple_of(u * qb, qb)
            qc[pl.ds(dst, qb), :] = load(q_scr)
            kc[pl.ds(dst + qb, qb), :] = load(k_scr).astype(BF16)
            vtc[:, pl.ds(dst + qb, qb)] = load(v_scr).T.astype(BF16)
            return carry

        lax.fori_loop(0, seq // qb, stage, 0, unroll=4)

        def unit(u, carry, c=c, n_blocks=n_blocks, rows=rows, block_start=block_start):
            us = [u * UNITS_PER_STEP + i for i in range(UNITS_PER_STEP)]
            w0s = [pl.multiple_of(ui * qb, qb) for ui in us]
            chains = [(i, g) for i in range(UNITS_PER_STEP) for g in range(len(HEAD_GROUPS))]
            ch = gh * HEAD_DIM
            scores = []
            for i, g in chains:
                q = qc[pl.ds(w0s[i], qb), :]
                kwin = kc[pl.ds(w0s[i], 2 * qb), :]
                qm = jnp.concatenate([jnp.where(head_q == h, q, 0.0) for h in HEAD_GROUPS[g]], axis=0).astype(BF16)
                scores.append(lax.dot_general(kwin, qm, (((1,), (1,)), ((), ())), preferred_element_type=F32))
            probs, dens, lses = [], [], []
            for (i, g), s in zip(chains, scores):
                valid = jnp.logical_and(band, jnp.logical_or(kk >= qb, us[i] % n_blocks > 0))
                s = jnp.where(valid, s, NEG)
                m = jnp.max(s, axis=0, keepdims=True)
                p = jnp.exp(s - m)
                den = jnp.sum(p, axis=0, keepdims=True)
                probs.append(p.astype(BF16))
                dens.append(den)
                lses.append(m + jnp.log(den))
            nums = [_dot(vtc[g * ch:(g + 1) * ch, pl.ds(w0s[i], 2 * qb)], p) for (i, g), p in zip(chains, probs)]
            for i in range(UNITS_PER_STEP):
                o_t, l_t = [], []
                for g in range(len(HEAD_GROUPS)):
                    n = chains.index((i, g))
                    for j in range(gh):
                        cols = slice(j * qb, (j + 1) * qb)
                        o_t.append(nums[n][j * HEAD_DIM:(j + 1) * HEAD_DIM, cols] / dens[n][:, cols])
                        l_t.append(jnp.broadcast_to(lses[n][:, cols], (HEAD_DIM, qb)))
                o = jnp.concatenate(o_t, axis=0).T
                l = jnp.concatenate(l_t, axis=0).T
                start = block_start(us[i])
                for half in range(2):
                    o_scr[c, half, rows(start), :] = o[:, half * hw:(half + 1) * hw]
                    l_scr[c, half, rows(start), :] = l[:, half * hw:(half + 1) * hw]
            return carry

        lax.fori_loop(0, n_steps, unit, 0)

    for half in range(2):
        l0, l1, l2 = l_scr[0, half], l_scr[1, half], l_scr[2, half]
        m = jnp.maximum(jnp.maximum(l0, l1), l2)
        e0, e1, e2 = jnp.exp(l0 - m), jnp.exp(l1 - m), jnp.exp(l2 - m)
        ob_ref[0, :, half * hw:(half + 1) * hw] = (
            (e0 * o_scr[0, half] + e1 * o_scr[1, half] + e2 * o_scr[2, half]) / (e0 + e1 + e2)).astype(ob_ref.dtype)


def _attn_prompt(q, k, v):
    n, s, w = q.shape
    blk = pl.BlockSpec((1, s, w), lambda i: (i, 0, 0))
    return pl.pallas_call(
        functools.partial(_attn_kernel, seq=s),
        grid=(n,),
        in_specs=[blk, blk, blk],
        out_specs=blk,
        out_shape=jax.ShapeDtypeStruct((n, s, w), BF16),
        scratch_shapes=[pltpu.VMEM((2, s, w // 2), F32)] * 3
        + [pltpu.VMEM((s, w), F32), pltpu.VMEM((Q_BLOCK + s, w), BF16), pltpu.VMEM((w, Q_BLOCK + s), BF16)]
        + [pltpu.VMEM((len(DIL_CONFIGS), 2, s, w // 2), F32)] * 2,
        compiler_params=_cparams("parallel"),
        name="attn_prompt",
    )(q, k, v)


def _pool_select(s2, s4, s8, s16):
    grp = _head_of_lane(s2.shape, s2.ndim - 1)
    return jnp.where(grp == 0, s2, jnp.where(grp == 1, s4, jnp.where(grp == 2, s8, s16)))


def _pool_window_lanes(shape):
    grp = _head_of_lane(shape, len(shape) - 1)
    return jnp.where(grp == 0, 2, jnp.where(grp == 1, 4, jnp.where(grp == 2, 8, 16)))


def _out_proj(ya, ob, yc, yd, w_out_ref):
    gw = GROUP_W
    acc = _dot(ya.astype(BF16), w_out_ref[0:gw, :])
    acc += _dot(ob.astype(BF16), w_out_ref[gw:2 * gw, :])
    acc += _dot(yc.astype(BF16), w_out_ref[2 * gw:3 * gw, :])
    acc += _dot(yd.astype(BF16), w_out_ref[3 * gw:4 * gw, :])
    return acc


def _ffn(x, g2, w_gu_ref, w_down_ref, ff_chunk):
    ff = w_down_ref.shape[0]
    h = _rmsnorm(x, g2).astype(BF16)
    acc = jnp.zeros_like(x)
    for c in range(ff // ff_chunk):
        lo = c * ff_chunk
        g = _dot(h, w_gu_ref[:, lo:lo + ff_chunk])
        u = _dot(h, w_gu_ref[:, ff + lo:ff + lo + ff_chunk])
        acc += _dot((jax.nn.silu(g) * u).astype(BF16), w_down_ref[lo:lo + ff_chunk, :])
    return x + acc


A_PAD = 32
C_PAD = 8
P_PAD = 16
SUBLANES = 8


def _causal_conv(buf, w_ref, width, pad, tm):
    base = pad - SUBLANES
    rows = tm + SUBLANES
    y = None
    for a in range(min(SUBLANES, width)):
        z = None
        for lag in range(a, width, SUBLANES):
            term = w_ref[width - 1 - lag:width - lag, :] * buf[base - (lag - a):base - (lag - a) + rows, :]
            z = term if z is None else z + term
        z = pltpu.roll(z, a, 0) if a else z
        y = z if y is None else y + z
    return y[SUBLANES:SUBLANES + tm]


def _inmix_kernel(x_ref, g_ref, w_ref, caw_ref, cab_ref, lng_ref, lnb_ref, ccw_ref, pw_ref, ps_ref, *refs, n_alias, tm):
    (q_ref, k_ref, v_ref, ya_ref, yc_ref, yd_ref, sta_ref, stc_ref, stp_ref, kt_ref, vt_ref,
     abuf, cbuf, pbuf) = refs[n_alias:]
    gw = GROUP_W
    w_ref = w_ref.at[0]
    j = pl.program_id(1)

    @pl.when(j == 0)
    def _():
        abuf[0:A_PAD, :] = jnp.zeros((A_PAD, gw), F32)
        cbuf[0:C_PAD, :] = jnp.zeros((C_PAD, gw), F32)
        pbuf[0:P_PAD, :] = jnp.zeros((P_PAD, gw), F32)

    h = _rmsnorm(x_ref[...], g_ref[...]).astype(BF16)

    abuf[A_PAD:A_PAD + tm, :] = _dot(h, w_ref[:, 0:gw]) * jax.nn.sigmoid(_dot(h, w_ref[:, gw:2 * gw]))
    acc = _causal_conv(abuf, caw_ref, CONV_A_WIDTH, A_PAD, tm)
    ya_ref[...] = jax.nn.silu(_layernorm(acc + cab_ref[...], lng_ref[...], lnb_ref[...])).astype(BF16)

    q_ref[...] = _dot(h, w_ref[:, 2 * gw:3 * gw]) * ATTN_SCALE
    k = _dot(h, w_ref[:, 3 * gw:4 * gw])
    v = _dot(h, w_ref[:, 4 * gw:5 * gw])
    k_ref[...] = k
    v_ref[...] = v
    kt_ref[0, 0] = k.T
    vt_ref[0, 0] = v.T

    cbuf[C_PAD:C_PAD + tm, :] = _dot(h, w_ref[:, 7 * gw:8 * gw]) * _dot(h, w_ref[:, 5 * gw:6 * gw])
    yc_ref[...] = (_dot(h, w_ref[:, 6 * gw:7 * gw]) * _causal_conv(cbuf, ccw_ref, CONV_C_WIDTH, C_PAD, tm)).astype(BF16)

    u = _dot(h, w_ref[:, 8 * gw:9 * gw])
    pbuf[P_PAD:P_PAD + tm, :] = u
    s1 = pbuf[...]
    s2 = s1 + pltpu.roll(s1, 1, 0)
    s4 = s2 + pltpu.roll(s2, 2, 0)
    s8 = s4 + pltpu.roll(s4, 4, 0)
    s16 = s8 + pltpu.roll(s8, 8, 0)
    win = _pool_select(s2, s4, s8, s16)[P_PAD:P_PAD + tm]
    pos = j * tm + lax.broadcasted_iota(jnp.int32, (tm, gw), 0)
    cnt = jnp.minimum(_pool_window_lanes((tm, gw)), pos + 1).astype(F32)
    yd_ref[...] = (_dot((win / cnt - u).astype(BF16), pw_ref[...]) * ps_ref[...]).astype(BF16)

    @pl.when(j == pl.num_programs(1) - 1)
    def _():
        sta_ref[0] = abuf[A_PAD + tm - (CONV_A_WIDTH - 1):A_PAD + tm, :]
        stc_ref[0] = cbuf[C_PAD + tm - (CONV_C_WIDTH - 1):C_PAD + tm, :]
        stp_ref[0] = pbuf[P_PAD + tm - POOL_STATE:P_PAD + tm, :]

    abuf[0:A_PAD, :] = abuf[tm:tm + A_PAD, :]
    cbuf[0:C_PAD, :] = cbuf[tm:tm + C_PAD, :]
    pbuf[0:P_PAD, :] = pbuf[tm:tm + P_PAD, :]


def _inmix_prompt(x, g, w_in, caw, cab, lng, lnb, ccw, pool_bd, ps, layer, n, tm, kt_all, vt_all):
    t, d = x.shape
    gw = GROUP_W
    depth = w_in.shape[0]
    spb = t // n // tm
    row = lambda w: pl.BlockSpec((tm, w), lambda i, j: (i * spb + j, 0))
    full = lambda a: pl.BlockSpec(a.shape, lambda i, j: (0,) * a.ndim)
    st = lambda r: pl.BlockSpec((1, r, gw), lambda i, j: (i, 0, 0))
    st_rows = (CONV_A_WIDTH - 1, CONV_C_WIDTH - 1, POOL_STATE)
    cm = pl.BlockSpec((1, 1, gw, tm), lambda i, j: (layer, i, 0, j))
    small = (caw, cab, lng, lnb, ccw, pool_bd, ps)
    in_specs = [row(d), full(g), _layer_spec(w_in, layer)] + [full(a) for a in small]
    args, aliases = [x, g, w_in, *small], {}
    if kt_all is not None:
        in_specs += [pl.BlockSpec(memory_space=pl.ANY)] * 2
        aliases = {len(args): 9, len(args) + 1: 10}
        args += [kt_all, vt_all]
    return pl.pallas_call(
        functools.partial(_inmix_kernel, n_alias=len(aliases), tm=tm),
        grid=(n, spb),
        in_specs=in_specs,
        out_specs=[row(gw)] * 6 + [st(r) for r in st_rows] + [cm, cm],
        out_shape=[jax.ShapeDtypeStruct((t, gw), F32)] * 3 + [jax.ShapeDtypeStruct((t, gw), BF16)] * 3
        + [jax.ShapeDtypeStruct((n, r, gw), F32) for r in st_rows]
        + [jax.ShapeDtypeStruct((depth, n, gw, t // n), F32)] * 2,
        input_output_aliases=aliases,
        scratch_shapes=[pltpu.VMEM((A_PAD + tm, gw), F32), pltpu.VMEM((C_PAD + tm, gw), F32),
                        pltpu.VMEM((P_PAD + tm, gw), F32)],
        compiler_params=_cparams("parallel", "arbitrary"),
        name="inmix_prompt",
    )(*args)


def _outffn_kernel(x_ref, ya_ref, ob_ref, yc_ref, yd_ref, wo_ref, g2_ref, wgu_ref, wd_ref, gf_ref,
                   qs_ref, kn_ref, vn_ref, kb_ref, vb_ref, *refs, n_alias, ff_chunk, final):
    o_ref, obs_ref, ko_ref, vo_ref = refs[n_alias:]
    _cache_job(qs_ref, kn_ref, vn_ref, kb_ref, vb_ref, obs_ref, ko_ref, vo_ref)
    x1 = x_ref[...] + _out_proj(ya_ref[...], ob_ref[...], yc_ref[...], yd_ref[...], wo_ref.at[0])
    x2 = _ffn(x1, g2_ref[...], wgu_ref.at[0], wd_ref.at[0], ff_chunk)
    o_ref[...] = _rmsnorm(x2, gf_ref[...]) if final else x2


def _outffn_prompt(x, ya, ob, yc, yd, w_out, g2, w_gu, w_down, gf, qs, kn, vn, kbuf, vbuf, k_all, v_all, layer, tm, final):
    t, d = x.shape
    depth, ns, w, buf = kbuf.shape
    assert t // tm == ns
    row = lambda a: pl.BlockSpec((tm, a.shape[1]), lambda i: (i, 0))
    full = lambda a: pl.BlockSpec(a.shape, lambda i: (0,) * a.ndim)
    tok = pl.BlockSpec((1, 1, w), lambda i: (i, 0, 0))
    blk = pl.BlockSpec((1, 1, w, buf), lambda i: (layer, i, 0, 0))
    in_specs = ([row(a) for a in (x, ya, ob, yc, yd)]
                + [_layer_spec(w_out, layer), full(g2), _layer_spec(w_gu, layer), _layer_spec(w_down, layer), full(gf)]
                + [tok, tok, tok, blk, blk])
    args, aliases = [x, ya, ob, yc, yd, w_out, g2, w_gu, w_down, gf, qs, kn, vn, kbuf, vbuf], {}
    if k_all is not None:
        in_specs += [pl.BlockSpec(memory_space=pl.ANY)] * 2
        aliases = {len(args): 2, len(args) + 1: 3}
        args += [k_all, v_all]
    return pl.pallas_call(
        functools.partial(_outffn_kernel, n_alias=len(aliases), ff_chunk=256, final=final),
        grid=(t // tm,),
        in_specs=in_specs,
        out_specs=[row(x), tok, blk, blk],
        out_shape=[jax.ShapeDtypeStruct((t, d), F32), jax.ShapeDtypeStruct((ns, 1, w), F32)]
        + [jax.ShapeDtypeStruct(kbuf.shape, F32)] * 2,
        input_output_aliases=aliases,
        compiler_params=_cparams("parallel"),
        name="outffn_prompt",
    )(*args)


def _cache_job(q_ref, kn_ref, vn_ref, kb_ref, vb_ref, ob_ref, ko_ref, vo_ref):
    gw = GROUP_W
    buf = kb_ref.shape[-1]
    assert max(win for win, _ in DIL_CONFIGS) <= buf
    ident = lax.broadcasted_iota(jnp.int32, (gw, gw), 0) == lax.broadcasted_iota(jnp.int32, (gw, gw), 1)
    to_col = lambda row: jnp.sum(jnp.where(ident, row, 0.0), axis=1, keepdims=True)
    to_row = lambda col: jnp.sum(jnp.where(ident, col, 0.0), axis=0, keepdims=True)
    qc, kc, vc = to_col(q_ref[0]), to_col(kn_ref[0]), to_col(vn_ref[0])
    kb, vb = kb_ref[0, 0], vb_ref[0, 0]
    last = lax.broadcasted_iota(jnp.int32, (gw, buf), 1) == buf - 1
    ko_ref[0, 0] = jnp.where(last, kc, pltpu.roll(kb, buf - 1, 1))
    vo_ref[0, 0] = jnp.where(last, vc, pltpu.roll(vb, buf - 1, 1))

    def heads(a):
        return jnp.sum(a.reshape(N_HEADS, HEAD_DIM, a.shape[-1]), axis=1)

    def spread(a):
        return jnp.broadcast_to(a[:, None, :], (N_HEADS, HEAD_DIM, a.shape[-1])).reshape(gw, a.shape[-1])

    s_all = heads(kb * qc)
    s_new = heads(kc * qc)
    outs, lses = [], []
    for win, dil in DIL_CONFIGS:
        lo = buf - win
        s = s_all[:, lo:]
        if dil > 1:
            back = win - lax.broadcasted_iota(jnp.int32, s.shape, 1)
            s = jnp.where(back % dil == 0, s, NEG)
        m = jnp.maximum(jnp.max(s, axis=1, keepdims=True), s_new)
        p, p_new = jnp.exp(s - m), jnp.exp(s_new - m)
        den = jnp.sum(p, axis=1, keepdims=True) + p_new
        pv = jnp.sum(vb[:, lo:] * spread(p), axis=1, keepdims=True) + vc * spread(p_new)
        outs.append(pv / spread(den))
        lses.append(spread(m + jnp.log(den)))
    m = jnp.maximum(jnp.maximum(lses[0], lses[1]), lses[2])
    es = [jnp.exp(l - m) for l in lses]
    ob_ref[0] = to_row((es[0] * outs[0] + es[1] * outs[1] + es[2] * outs[2]) / (es[0] + es[1] + es[2]))


def _sample_rest_kernel(ag_ref, cd_ref, ob_ref, x_ref, sa_ref, sc_ref, sp_ref, caw_ref, cab_ref, lng_ref, lnb_ref,
                        ccw_ref, pw_ref, ps_ref, wo_ref, g2_ref, wgu_ref, wd_ref, gf_ref,
                        y_ref, na_ref, nc_ref, np_ref, *, pos0, final):
    gw = GROUP_W
    ga = ag_ref[:, 0:gw] * jax.nn.sigmoid(ag_ref[:, gw:2 * gw])
    na = CONV_A_WIDTH - 1
    acc = caw_ref[na:na + 1, :] * ga
    for t in range(na):
        acc += caw_ref[t:t + 1, :] * sa_ref[0, t]
    ya = jax.nn.silu(_layernorm(acc + cab_ref[...], lng_ref[...], lnb_ref[...]))
    na_ref[0:na - 1] = sa_ref[0, 1:na]
    na_ref[na - 1] = ga

    cx = cd_ref[:, 2 * gw:3 * gw] * cd_ref[:, 0:gw]
    nc = CONV_C_WIDTH - 1
    acc = ccw_ref[nc:nc + 1, :] * cx
    for t in range(nc):
        acc += ccw_ref[t:t + 1, :] * sc_ref[0, t]
    yc = cd_ref[:, gw:2 * gw] * acc
    nc_ref[0:nc - 1] = sc_ref[0, 1:nc]
    nc_ref[nc - 1] = cx

    u = cd_ref[:, 3 * gw:4 * gw]
    npl = POOL_STATE
    back = lambda i: sp_ref[0, npl - i]
    s2 = u + back(1)
    s4 = s2 + back(2) + back(3)
    s8 = s4 + back(4) + back(5) + back(6) + back(7)
    s16 = s8
    for i in range(8, 16):
        s16 = s16 + back(i)
    cnt = jnp.minimum(_pool_window_lanes(u.shape), pos0 + 1).astype(F32)
    yd = _dot((_pool_select(s2, s4, s8, s16) / cnt - u).astype(BF16), pw_ref[...]) * ps_ref[...]
    np_ref[0:npl - 1] = sp_ref[0, 1:npl]
    np_ref[npl - 1] = u

    x1 = x_ref[...] + _out_proj(ya, ob_ref[...], yc, yd, wo_ref.at[0])
    x2 = _ffn(x1, g2_ref[...], wgu_ref.at[0], wd_ref.at[0], 256)
    y_ref[...] = _rmsnorm(x2, gf_ref[...]) if final else x2


def _sample_rest(ag, cd, ob, x, sa, sc, sp, caw, cab, lng, lnb, ccw, pool_bd, ps, w_out, g2, w_gu, w_down, gf, layer, pos0, final):
    full = lambda a: pl.BlockSpec(a.shape, lambda i: (0,) * a.ndim)
    st_in = lambda a: pl.BlockSpec((1,) + a.shape[1:], lambda i: (layer, 0, 0, 0))
    st_out = lambda a: pl.BlockSpec(a.shape[1:], lambda i: (0, 0, 0))
    dense = (caw, cab, lng, lnb, ccw, pool_bd, ps, w_out, g2, w_gu, w_down, gf)
    stacked = (w_out, w_gu, w_down)
    dense_spec = lambda a: _layer_spec(a, layer) if any(a is w for w in stacked) else full(a)
    return pl.pallas_call(
        functools.partial(_sample_rest_kernel, pos0=pos0, final=final),
        grid=(1,),
        in_specs=[full(a) for a in (ag, cd, ob, x)] + [st_in(a) for a in (sa, sc, sp)] + [dense_spec(a) for a in dense],
        out_specs=[full(x)] + [st_out(a) for a in (sa, sc, sp)],
        out_shape=[jax.ShapeDtypeStruct(x.shape, F32)] + [jax.ShapeDtypeStruct(a.shape[1:], F32) for a in (sa, sc, sp)],
        compiler_params=_cparams("arbitrary"),
        name="sample_rest",
    )(ag, cd, ob, x, sa, sc, sp, *dense)


def _block_diag(pool_w):
    g, c, e = pool_w.shape
    eye = jnp.eye(g, dtype=pool_w.dtype)
    return (pool_w[:, :, None, :] * eye[:, None, :, None]).reshape(g * c, g * e)


def kernel(x_prompt, x_sample, cache_win_k, cache_win_v, state_conv_a, state_conv_c, state_pool, w_in, conv_a_w, conv_a_b, ln_a_g, ln_a_b, conv_c_w, pool_w, pool_scale, w_out, norm1_g, norm2_g, w_gu, w_down, final_g):
    depth = w_in.shape[0]
    n, s, d = x_prompt.shape
    ns, ts, _ = x_sample.shape
    assert ts == 1
    gw = GROUP_W
    tm = 512

    row = lambda a: a.reshape(1, -1)
    to_cm = lambda a: jnp.transpose(a, (0, 1, 3, 4, 2)).reshape(a.shape[0], a.shape[1], gw, a.shape[2])
    from_cm = lambda a: jnp.transpose(a.reshape(a.shape[0], a.shape[1], N_HEADS, HEAD_DIM, a.shape[3]), (0, 1, 4, 2, 3))
    swap = lambda a: jnp.transpose(a, (0, 2, 1, 3))
    kbuf, vbuf = to_cm(cache_win_k), to_cm(cache_win_v)
    sa_all, sc_all, sp_all = swap(state_conv_a), swap(state_conv_c), swap(state_pool)

    xp = x_prompt.reshape(n * s, d)
    xs = x_sample.reshape(ns, d)
    gf = row(final_g)
    kt_p = vt_p = kt_s = vt_s = None
    st_p = [[] for _ in range(3)]
    st_s = [[] for _ in range(3)]
    w_in, w_out, w_gu, w_down = (w.astype(BF16) for w in (w_in, w_out, w_gu, w_down))
    for l in range(depth):
        pool_bd = _block_diag(pool_w[l]).astype(BF16)
        small = (conv_a_w[l], row(conv_a_b[l]), row(ln_a_g[l]), row(ln_a_b[l]), conv_c_w[l], pool_bd, row(pool_scale[l]))
        final = l == depth - 1

        ag, qs, ks, vs, cd = _inproj(xs, row(norm1_g[l]), w_in, l)
        q, k, v, ya, yc, yd, *states, kt_p, vt_p = _inmix_prompt(xp, row(norm1_g[l]), w_in, *small, l, n, tm, kt_p, vt_p)
        ob = _attn_prompt(q.reshape(n, s, gw), k.reshape(n, s, gw), v.reshape(n, s, gw)).reshape(n * s, gw)
        xp, obs, kt_s, vt_s = _outffn_prompt(
            xp, ya, ob, yc, yd, w_out, row(norm2_g[l]), w_gu, w_down, gf,
            qs.reshape(ns, 1, gw), ks.reshape(ns, 1, gw), vs.reshape(ns, 1, gw), kbuf, vbuf, kt_s, vt_s, l, tm, final)
        for lst, a in zip(st_p, states):
            lst.append(a)
        xs, *states = _sample_rest(ag, cd, obs.reshape(ns, gw), xs, sa_all, sc_all, sp_all, *small, w_out,
                                   row(norm2_g[l]), w_gu, w_down, gf, l, PAST_LEN, final)
        for lst, a in zip(st_s, states):
            lst.append(a)

    y_prompt = xp.reshape(n, s, d)
    y_sample = xs.reshape(ns, ts, d)
    return (y_prompt, y_sample, from_cm(kt_p), from_cm(vt_p), *[jnp.stack(a, axis=0) for a in st_p],
            from_cm(kt_s), from_cm(vt_s), *[swap(jnp.stack(a, axis=0)) for a in st_s])
```

```python
import functools
import math

import jax
import jax.numpy as jnp
from jax import lax
from jax.experimental import pallas as pl
from jax.experimental.pallas import tpu as pltpu

F32 = jnp.float32
BF16 = jnp.bfloat16

GROUP_W = 256
HEAD_DIM = 64
N_HEADS = GROUP_W // HEAD_DIM
CONV_A_WIDTH = 31
CONV_C_WIDTH = 3
POOL_WINDOWS = (2, 4, 8, 16)
POOL_STATE = max(POOL_WINDOWS) - 1
DIL_CONFIGS = ((128, 1), (512, 4), (2048, 16))
Q_BLOCK = 128
PAST_LEN = 16384
ATTN_SCALE = 1.0 / math.sqrt(HEAD_DIM)
EPS = 1e-6
NEG = -1e30

VMEM_LIMIT_BYTES = 56 * 1024 * 1024


def _cparams(*sem):
    return pltpu.CompilerParams(dimension_semantics=sem, vmem_limit_bytes=VMEM_LIMIT_BYTES)


def _rmsnorm(x, g):
    return x * lax.rsqrt(jnp.mean(x * x, axis=-1, keepdims=True) + EPS) * g


def _layernorm(x, g, b):
    mu = jnp.mean(x, axis=-1, keepdims=True)
    xc = x - mu
    return xc * lax.rsqrt(jnp.mean(xc * xc, axis=-1, keepdims=True) + EPS) * g + b


def _layer_spec(a, layer):
    return pl.BlockSpec((1,) + a.shape[1:], lambda *_: (layer,) + (0,) * (a.ndim - 1))


def _dot(a, b):
    return jnp.dot(a, b, preferred_element_type=F32)


def _head_of_lane(shape, dim):
    return lax.broadcasted_iota(jnp.int32, shape, dim) // HEAD_DIM


def _inproj_kernel(x_ref, g_ref, w_ref, ag_ref, q_ref, k_ref, v_ref, cd_ref):
    w_ref = w_ref.at[0]
    h = _rmsnorm(x_ref[...], g_ref[...]).astype(BF16)
    gw = GROUP_W
    ag_ref[...] = _dot(h, w_ref[:, 0:2 * gw])
    q_ref[...] = _dot(h, w_ref[:, 2 * gw:3 * gw]) * ATTN_SCALE
    k_ref[...] = _dot(h, w_ref[:, 3 * gw:4 * gw])
    v_ref[...] = _dot(h, w_ref[:, 4 * gw:5 * gw])
    cd_ref[...] = _dot(h, w_ref[:, 5 * gw:9 * gw])


def _inproj(x, g, w_in, layer):
    t, d = x.shape
    gw = GROUP_W
    full = lambda a: pl.BlockSpec(a.shape, lambda i: (0,) * a.ndim)
    widths = (2 * gw, gw, gw, gw, 4 * gw)
    return pl.pallas_call(
        _inproj_kernel,
        grid=(1,),
        in_specs=[full(x), full(g), _layer_spec(w_in, layer)],
        out_specs=[pl.BlockSpec((t, w), lambda i: (0, 0)) for w in widths],
        out_shape=[jax.ShapeDtypeStruct((t, w), F32) for w in widths],
        compiler_params=_cparams("arbitrary"),
        name="inproj",
    )(x, g, w_in)


HEAD_GROUPS = ((0, 1), (2, 3))
UNITS_PER_STEP = 16


def _attn_kernel(q_ref, k_ref, v_ref, ob_ref, q_scr, k_scr, v_scr, qc, kc, vtc, o_scr, l_scr, *, seq):
    qb = Q_BLOCK
    hw = GROUP_W // 2
    for src, dst in ((q_ref, q_scr), (k_ref, k_scr), (v_ref, v_scr)):
        for half in range(2):
            dst[half] = src[0, :, half * hw:(half + 1) * hw]
    kc[0:qb, :] = jnp.zeros((qb, GROUP_W), BF16)
    vtc[:, 0:qb] = jnp.zeros((GROUP_W, qb), BF16)

    gh = len(HEAD_GROUPS[0])
    head_q = _head_of_lane((qb, GROUP_W), 1)
    kk = lax.broadcasted_iota(jnp.int32, (2 * qb, gh * qb), 0)
    qi = lax.broadcasted_iota(jnp.int32, (2 * qb, gh * qb), 1) % qb
    band = jnp.logical_or(jnp.logical_and(kk < qb, kk >= qi), jnp.logical_and(kk >= qb, kk - qb <= qi))

    for c, (_, dil) in enumerate(DIL_CONFIGS):
        n_blocks = seq // (dil * qb)
        n_steps = seq // qb // UNITS_PER_STEP
        rows = lambda s0, dil=dil: pl.ds(s0, qb, stride=dil) if dil > 1 else pl.ds(s0, qb)

        def block_start(u, dil=dil, n_blocks=n_blocks):
            start = u // n_blocks + (u % n_blocks) * (qb * dil)
            return pl.multiple_of(start, qb) if dil == 1 else start

        def stage(u, carry, rows=rows, block_start=block_start):
            start = block_start(u)
            load = lambda scr: jnp.concatenate([scr[0, rows(start), :], scr[1, rows(start), :]], axis=1)
            dst = pl.multiple_of(u * qb, qb)
            qc[pl.ds(dst, qb), :] = load(q_scr)
            kc[pl.ds(dst + qb, qb), :] = load(k_scr).astype(BF16)
            vtc[:, pl.ds(dst + qb, qb)] = load(v_scr).T.astype(BF16)
            return carry

        lax.fori_loop(0, seq // qb, stage, 0, unroll=4)

        def unit(u, carry, c=c, n_blocks=n_blocks, rows=rows, block_start=block_start):
            us = [u * UNITS_PER_STEP + i for i in range(UNITS_PER_STEP)]
            w0s = [pl.multiple_of(ui * qb, qb) for ui in us]
            chains = [(i, g) for i in range(UNITS_PER_STEP) for g in range(len(HEAD_GROUPS))]
            ch = gh * HEAD_DIM
            scores = []
            for i, g in chains:
                q = qc[pl.ds(w0s[i], qb), :]
                kwin = kc[pl.ds(w0s[i], 2 * qb), :]
                qm = jnp.concatenate([jnp.where(head_q == h, q, 0.0) for h in HEAD_GROUPS[g]], axis=0).astype(BF16)
                scores.append(lax.dot_general(kwin, qm, (((1,), (1,)), ((), ())), preferred_element_type=F32))
            probs, dens, lses = [], [], []
            for (i, g), s in zip(chains, scores):
                valid = jnp.logical_and(band, jnp.logical_or(kk >= qb, us[i] % n_blocks > 0))
                s = jnp.where(valid, s, NEG)
                m = jnp.max(s, axis=0, keepdims=True)
                p = jnp.exp(s - m)
                den = jnp.sum(p, axis=0, keepdims=True)
                probs.append(p.astype(BF16))
                dens.append(den)
                lses.append(m + jnp.log(den))
            nums = [_dot(vtc[g * ch:(g + 1) * ch, pl.ds(w0s[i], 2 * qb)], p) for (i, g), p in zip(chains, probs)]
            for i in range(UNITS_PER_STEP):
                o_t, l_t = [], []
                for g in range(len(HEAD_GROUPS)):
                    n = chains.index((i, g))
                    for j in range(gh):
                        cols = slice(j * qb, (j + 1) * qb)
                        o_t.append(nums[n][j * HEAD_DIM:(j + 1) * HEAD_DIM, cols] / dens[n][:, cols])
                        l_t.append(jnp.broadcast_to(lses[n][:, cols], (HEAD_DIM, qb)))
                o = jnp.concatenate(o_t, axis=0).T
                l = jnp.concatenate(l_t, axis=0).T
                start = block_start(us[i])
                for half in range(2):
                    o_scr[c, half, rows(start), :] = o[:, half * hw:(half + 1) * hw]
                    l_scr[c, half, rows(start), :] = l[:, half * hw:(half + 1) * hw]
            return carry

        lax.fori_loop(0, n_steps, unit, 0)

    for half in range(2):
        l0, l1, l2 = l_scr[0, half], l_scr[1, half], l_scr[2, half]
        m = jnp.maximum(jnp.maximum(l0, l1), l2)
        e0, e1, e2 = jnp.exp(l0 - m), jnp.exp(l1 - m), jnp.exp(l2 - m)
        ob_ref[0, :, half * hw:(half + 1) * hw] = (
            (e0 * o_scr[0, half] + e1 * o_scr[1, half] + e2 * o_scr[2, half]) / (e0 + e1 + e2)).astype(ob_ref.dtype)


def _attn_prompt(q, k, v):
    n, s, w = q.shape
    blk = pl.BlockSpec((1, s, w), lambda i: (i, 0, 0))
    return pl.pallas_call(
        functools.partial(_attn_kernel, seq=s),
        grid=(n,),
        in_specs=[blk, blk, blk],
        out_specs=blk,
        out_shape=jax.ShapeDtypeStruct((n, s, w), BF16),
        scratch_shapes=[pltpu.VMEM((2, s, w // 2), F32)] * 3
        + [pltpu.VMEM((s, w), F32), pltpu.VMEM((Q_BLOCK + s, w), BF16), pltpu.VMEM((w, Q_BLOCK + s), BF16)]
        + [pltpu.VMEM((len(DIL_CONFIGS), 2, s, w // 2), F32)] * 2,
        compiler_params=_cparams("parallel"),
        name="attn_prompt",
    )(q, k, v)


def _pool_select(s2, s4, s8, s16):
    grp = _head_of_lane(s2.shape, s2.ndim - 1)
    return jnp.where(grp == 0, s2, jnp.where(grp == 1, s4, jnp.where(grp == 2, s8, s16)))


def _pool_window_lanes(shape):
    grp = _head_of_lane(shape, len(shape) - 1)
    return jnp.where(grp == 0, 2, jnp.where(grp == 1, 4, jnp.where(grp == 2, 8, 16)))


def _out_proj(ya, ob, yc, yd, w_out_ref):
    gw = GROUP_W
    acc = _dot(ya.astype(BF16), w_out_ref[0:gw, :])
    acc += _dot(ob.astype(BF16), w_out_ref[gw:2 * gw, :])
    acc += _dot(yc.astype(BF16), w_out_ref[2 * gw:3 * gw, :])
    acc += _dot(yd.astype(BF16), w_out_ref[3 * gw:4 * gw, :])
    return acc


def _ffn(x, g2, w_gu_ref, w_down_ref, ff_chunk):
    ff = w_down_ref.shape[0]
    h = _rmsnorm(x, g2).astype(BF16)
    acc = jnp.zeros_like(x)
    for c in range(ff // ff_chunk):
        lo = c * ff_chunk
        g = _dot(h, w_gu_ref[:, lo:lo + ff_chunk])
        u = _dot(h, w_gu_ref[:, ff + lo:ff + lo + ff_chunk])
        acc += _dot((jax.nn.silu(g) * u).astype(BF16), w_down_ref[lo:lo + ff_chunk, :])
    return x + acc


A_PAD = 32
C_PAD = 8
P_PAD = 16
SUBLANES = 8


def _causal_conv(buf, w_ref, width, pad, tm):
    base = pad - SUBLANES
    rows = tm + SUBLANES
    y = None
    for a in range(min(SUBLANES, width)):
        z = None
        for lag in range(a, width, SUBLANES):
            term = w_ref[width - 1 - lag:width - lag, :] * buf[base - (lag - a):base - (lag - a) + rows, :]
            z = term if z is None else z + term
        z = pltpu.roll(z, a, 0) if a else z
        y = z if y is None else y + z
    return y[SUBLANES:SUBLANES + tm]


def _inmix_kernel(x_ref, g_ref, w_ref, caw_ref, cab_ref, lng_ref, lnb_ref, ccw_ref, pw_ref, ps_ref, *refs, n_alias, tm):
    (q_ref, k_ref, v_ref, ya_ref, yc_ref, yd_ref, sta_ref, stc_ref, stp_ref, kt_ref, vt_ref,
     abuf, cbuf, pbuf) = refs[n_alias:]
    gw = GROUP_W
    w_ref = w_ref.at[0]
    j = pl.program_id(1)

    @pl.when(j == 0)
    def _():
        abuf[0:A_PAD, :] = jnp.zeros((A_PAD, gw), F32)
        cbuf[0:C_PAD, :] = jnp.zeros((C_PAD, gw), F32)
        pbuf[0:P_PAD, :] = jnp.zeros((P_PAD, gw), F32)

    h = _rmsnorm(x_ref[...], g_ref[...]).astype(BF16)

    abuf[A_PAD:A_PAD + tm, :] = _dot(h, w_ref[:, 0:gw]) * jax.nn.sigmoid(_dot(h, w_ref[:, gw:2 * gw]))
    acc = _causal_conv(abuf, caw_ref, CONV_A_WIDTH, A_PAD, tm)
    ya_ref[...] = jax.nn.silu(_layernorm(acc + cab_ref[...], lng_ref[...], lnb_ref[...])).astype(BF16)

    q_ref[...] = _dot(h, w_ref[:, 2 * gw:3 * gw]) * ATTN_SCALE
    k = _dot(h, w_ref[:, 3 * gw:4 * gw])
    v = _dot(h, w_ref[:, 4 * gw:5 * gw])
    k_ref[...] = k
    v_ref[...] = v
    kt_ref[0, 0] = k.T
    vt_ref[0, 0] = v.T

    cbuf[C_PAD:C_PAD + tm, :] = _dot(h, w_ref[:, 7 * gw:8 * gw]) * _dot(h, w_ref[:, 5 * gw:6 * gw])
    yc_ref[...] = (_dot(h, w_ref[:, 6 * gw:7 * gw]) * _causal_conv(cbuf, ccw_ref, CONV_C_WIDTH, C_PAD, tm)).astype(BF16)

    u = _dot(h, w_ref[:, 8 * gw:9 * gw])
    pbuf[P_PAD:P_PAD + tm, :] = u
    s1 = pbuf[...]
    s2 = s1 + pltpu.roll(s1, 1, 0)
    s4 = s2 + pltpu.roll(s2, 2, 0)
    s8 = s4 + pltpu.roll(s4, 4, 0)
    s16 = s8 + pltpu.roll(s8, 8, 0)
    win = _pool_select(s2, s4, s8, s16)[P_PAD:P_PAD + tm]
    pos = j * tm + lax.broadcasted_iota(jnp.int32, (tm, gw), 0)
    cnt = jnp.minimum(_pool_window_lanes((tm, gw)), pos + 1).astype(F32)
    yd_ref[...] = (_dot((win / cnt - u).astype(BF16), pw_ref[...]) * ps_ref[...]).astype(BF16)

    @pl.when(j == pl.num_programs(1) - 1)
    def _():
        sta_ref[0] = abuf[A_PAD + tm - (CONV_A_WIDTH - 1):A_PAD + tm, :]
        stc_ref[0] = cbuf[C_PAD + tm - (CONV_C_WIDTH - 1):C_PAD + tm, :]
        stp_ref[0] = pbuf[P_PAD + tm - POOL_STATE:P_PAD + tm, :]

    abuf[0:A_PAD, :] = abuf[tm:tm + A_PAD, :]
    cbuf[0:C_PAD, :] = cbuf[tm:tm + C_PAD, :]
    pbuf[0:P_PAD, :] = pbuf[tm:tm + P_PAD, :]


def _inmix_prompt(x, g, w_in, caw, cab, lng, lnb, ccw, pool_bd, ps, layer, n, tm, kt_all, vt_all):
    t, d = x.shape
    gw = GROUP_W
    depth = w_in.shape[0]
    spb = t // n // tm
    row = lambda w: pl.BlockSpec((tm, w), lambda i, j: (i * spb + j, 0))
    full = lambda a: pl.BlockSpec(a.shape, lambda i, j: (0,) * a.ndim)
    st = lambda r: pl.BlockSpec((1, r, gw), lambda i, j: (i, 0, 0))
    st_rows = (CONV_A_WIDTH - 1, CONV_C_WIDTH - 1, POOL_STATE)
    cm = pl.BlockSpec((1, 1, gw, tm), lambda i, j: (layer, i, 0, j))
    small = (caw, cab, lng, lnb, ccw, pool_bd, ps)
    in_specs = [row(d), full(g), _layer_spec(w_in, layer)] + [full(a) for a in small]
    args, aliases = [x, g, w_in, *small], {}
    if kt_all is not None:
        in_specs += [pl.BlockSpec(memory_space=pl.ANY)] * 2
        aliases = {len(args): 9, len(args) + 1: 10}
        args += [kt_all, vt_all]
    return pl.pallas_call(
        functools.partial(_inmix_kernel, n_alias=len(aliases), tm=tm),
        grid=(n, spb),
        in_specs=in_specs,
        out_specs=[row(gw)] * 6 + [st(r) for r in st_rows] + [cm, cm],
        out_shape=[jax.ShapeDtypeStruct((t, gw), F32)] * 3 + [jax.ShapeDtypeStruct((t, gw), BF16)] * 3
        + [jax.ShapeDtypeStruct((n, r, gw), F32) for r in st_rows]
        + [jax.ShapeDtypeStruct((depth, n, gw, t // n), F32)] * 2,
        input_output_aliases=aliases,
        scratch_shapes=[pltpu.VMEM((A_PAD + tm, gw), F32), pltpu.VMEM((C_PAD + tm, gw), F32),
                        pltpu.VMEM((P_PAD + tm, gw), F32)],
        compiler_params=_cparams("parallel", "arbitrary"),
        name="inmix_prompt",
    )(*args)


def _outffn_kernel(x_ref, ya_ref, ob_ref, yc_ref, yd_ref, wo_ref, g2_ref, wgu_ref, wd_ref, gf_ref,
                   qs_ref, kn_ref, vn_ref, kb_ref, vb_ref, *refs, n_alias, ff_chunk, final):
    o_ref, obs_ref, ko_ref, vo_ref = refs[n_alias:]
    _cache_job(qs_ref, kn_ref, vn_ref, kb_ref, vb_ref, obs_ref, ko_ref, vo_ref)
    x1 = x_ref[...] + _out_proj(ya_ref[...], ob_ref[...], yc_ref[...], yd_ref[...], wo_ref.at[0])
    x2 = _ffn(x1, g2_ref[...], wgu_ref.at[0], wd_ref.at[0], ff_chunk)
    o_ref[...] = _rmsnorm(x2, gf_ref[...]) if final else x2


def _outffn_prompt(x, ya, ob, yc, yd, w_out, g2, w_gu, w_down, gf, qs, kn, vn, kbuf, vbuf, k_all, v_all, layer, tm, final):
    t, d = x.shape
    depth, ns, w, buf = kbuf.shape
    assert t // tm == ns
    row = lambda a: pl.BlockSpec((tm, a.shape[1]), lambda i: (i, 0))
    full = lambda a: pl.BlockSpec(a.shape, lambda i: (0,) * a.ndim)
    tok = pl.BlockSpec((1, 1, w), lambda i: (i, 0, 0))
    blk = pl.BlockSpec((1, 1, w, buf), lambda i: (layer, i, 0, 0))
    in_specs = ([row(a) for a in (x, ya, ob, yc, yd)]
                + [_layer_spec(w_out, layer), full(g2), _layer_spec(w_gu, layer), _layer_spec(w_down, layer), full(gf)]
                + [tok, tok, tok, blk, blk])
    args, aliases = [x, ya, ob, yc, yd, w_out, g2, w_gu, w_down, gf, qs, kn, vn, kbuf, vbuf], {}
    if k_all is not None:
        in_specs += [pl.BlockSpec(memory_space=pl.ANY)] * 2
        aliases = {len(args): 2, len(args) + 1: 3}
        args += [k_all, v_all]
    return pl.pallas_call(
        functools.partial(_outffn_kernel, n_alias=len(aliases), ff_chunk=256, final=final),
        grid=(t // tm,),
        in_specs=in_specs,
        out_specs=[row(x), tok, blk, blk],
        out_shape=[jax.ShapeDtypeStruct((t, d), F32), jax.ShapeDtypeStruct((ns, 1, w), F32)]
        + [jax.ShapeDtypeStruct(kbuf.shape, F32)] * 2,
        input_output_aliases=aliases,
        compiler_params=_cparams("parallel"),
        name="outffn_prompt",
    )(*args)


def _cache_job(q_ref, kn_ref, vn_ref, kb_ref, vb_ref, ob_ref, ko_ref, vo_ref):
    gw = GROUP_W
    buf = kb_ref.shape[-1]
    assert max(win for win, _ in DIL_CONFIGS) <= buf
    ident = lax.broadcasted_iota(jnp.int32, (gw, gw), 0) == lax.broadcasted_iota(jnp.int32, (gw, gw), 1)
    to_col = lambda row: jnp.sum(jnp.where(ident, row, 0.0), axis=1, keepdims=True)
    to_row = lambda col: jnp.sum(jnp.where(ident, col, 0.0), axis=0, keepdims=True)
    qc, kc, vc = to_col(q_ref[0]), to_col(kn_ref[0]), to_col(vn_ref[0])
    kb, vb = kb_ref[0, 0], vb_ref[0, 0]
    last = lax.broadcasted_iota(jnp.int32, (gw, buf), 1) == buf - 1
    ko_ref[0, 0] = jnp.where(last, kc, pltpu.roll(kb, buf - 1, 1))
    vo_ref[0, 0] = jnp.where(last, vc, pltpu.roll(vb, buf - 1, 1))

    def heads(a):
        return jnp.sum(a.reshape(N_HEADS, HEAD_DIM, a.shape[-1]), axis=1)

    def spread(a):
        return jnp.broadcast_to(a[:, None, :], (N_HEADS, HEAD_DIM, a.shape[-1])).reshape(gw, a.shape[-1])

    s_all = heads(kb * qc)
    s_new = heads(kc * qc)
    outs, lses = [], []
    for win, dil in DIL_CONFIGS:
        lo = buf - win
        s = s_all[:, lo:]
        if dil > 1:
            back = win - lax.broadcasted_iota(jnp.int32, s.shape, 1)
            s = jnp.where(back % dil == 0, s, NEG)
        m = jnp.maximum(jnp.max(s, axis=1, keepdims=True), s_new)
        p, p_new = jnp.exp(s - m), jnp.exp(s_new - m)
        den = jnp.sum(p, axis=1, keepdims=True) + p_new
        pv = jnp.sum(vb[:, lo:] * spread(p), axis=1, keepdims=True) + vc * spread(p_new)
        outs.append(pv / spread(den))
        lses.append(spread(m + jnp.log(den)))
    m = jnp.maximum(jnp.maximum(lses[0], lses[1]), lses[2])
    es = [jnp.exp(l - m) for l in lses]
    ob_ref[0] = to_row((es[0] * outs[0] + es[1] * outs[1] + es[2] * outs[2]) / (es[0] + es[1] + es[2]))


def _sample_rest_kernel(ag_ref, cd_ref, ob_ref, x_ref, sa_ref, sc_ref, sp_ref, caw_ref, cab_ref, lng_ref, lnb_ref,
                        ccw_ref, pw_ref, ps_ref, wo_ref, g2_ref, wgu_ref, wd_ref, gf_ref,
                        y_ref, na_ref, nc_ref, np_ref, *, pos0, final):
    gw = GROUP_W
    ga = ag_ref[:, 0:gw] * jax.nn.sigmoid(ag_ref[:, gw:2 * gw])
    na = CONV_A_WIDTH - 1
    acc = caw_ref[na:na + 1, :] * ga
    for t in range(na):
        acc += caw_ref[t:t + 1, :] * sa_ref[0, t]
    ya = jax.nn.silu(_layernorm(acc + cab_ref[...], lng_ref[...], lnb_ref[...]))
    na_ref[0:na - 1] = sa_ref[0, 1:na]
    na_ref[na - 1] = ga

    cx = cd_ref[:, 2 * gw:3 * gw] * cd_ref[:, 0:gw]
    nc = CONV_C_WIDTH - 1
    acc = ccw_ref[nc:nc + 1, :] * cx
    for t in range(nc):
        acc += ccw_ref[t:t + 1, :] * sc_ref[0, t]
    yc = cd_ref[:, gw:2 * gw] * acc
    nc_ref[0:nc - 1] = sc_ref[0, 1:nc]
    nc_ref[nc - 1] = cx

    u = cd_ref[:, 3 * gw:4 * gw]
    npl = POOL_STATE
    back = lambda i: sp_ref[0, npl - i]
    s2 = u + back(1)
    s4 = s2 + back(2) + back(3)
    s8 = s4 + back(4) + back(5) + back(6) + back(7)
    s16 = s8
    for i in range(8, 16):
        s16 = s16 + back(i)
    cnt = jnp.minimum(_pool_window_lanes(u.shape), pos0 + 1).astype(F32)
    yd = _dot((_pool_select(s2, s4, s8, s16) / cnt - u).astype(BF16), pw_ref[...]) * ps_ref[...]
    np_ref[0:npl - 1] = sp_ref[0, 1:npl]
    np_ref[npl - 1] = u

    x1 = x_ref[...] + _out_proj(ya, ob_ref[...], yc, yd, wo_ref.at[0])
    x2 = _ffn(x1, g2_ref[...], wgu_ref.at[0], wd_ref.at[0], 256)
    y_ref[...] = _rmsnorm(x2, gf_ref[...]) if final else x2


def _sample_rest(ag, cd, ob, x, sa, sc, sp, caw, cab, lng, lnb, ccw, pool_bd, ps, w_out, g2, w_gu, w_down, gf, layer, pos0, final):
    full = lambda a: pl.BlockSpec(a.shape, lambda i: (0,) * a.ndim)
    st_in = lambda a: pl.BlockSpec((1,) + a.shape[1:], lambda i: (layer, 0, 0, 0))
    st_out = lambda a: pl.BlockSpec(a.shape[1:], lambda i: (0, 0, 0))
    dense = (caw, cab, lng, lnb, ccw, pool_bd, ps, w_out, g2, w_gu, w_down, gf)
    stacked = (w_out, w_gu, w_down)
    dense_spec = lambda a: _layer_spec(a, layer) if any(a is w for w in stacked) else full(a)
    return pl.pallas_call(
        functools.partial(_sample_rest_kernel, pos0=pos0, final=final),
        grid=(1,),
        in_specs=[full(a) for a in (ag, cd, ob, x)] + [st_in(a) for a in (sa, sc, sp)] + [dense_spec(a) for a in dense],
        out_specs=[full(x)] + [st_out(a) for a in (sa, sc, sp)],
        out_shape=[jax.ShapeDtypeStruct(x.shape, F32)] + [jax.ShapeDtypeStruct(a.shape[1:], F32) for a in (sa, sc, sp)],
        compiler_params=_cparams("arbitrary"),
        name="sample_rest",
    )(ag, cd, ob, x, sa, sc, sp, *dense)


def _block_diag(pool_w):
    g, c, e = pool_w.shape
    eye = jnp.eye(g, dtype=pool_w.dtype)
    return (pool_w[:, :, None, :] * eye[:, None, :, None]).reshape(g * c, g * e)


def kernel(x_prompt, x_sample, cache_win_k, cache_win_v, state_conv_a, state_conv_c, state_pool, w_in, conv_a_w, conv_a_b, ln_a_g, ln_a_b, conv_c_w, pool_w, pool_scale, w_out, norm1_g, norm2_g, w_gu, w_down, final_g):
    depth = w_in.shape[0]
    n, s, d = x_prompt.shape
    ns, ts, _ = x_sample.shape
    assert ts == 1
    gw = GROUP_W
    tm = 512

    row = lambda a: a.reshape(1, -1)
    to_cm = lambda a: jnp.transpose(a, (0, 1, 3, 4, 2)).reshape(a.shape[0], a.shape[1], gw, a.shape[2])
    from_cm = lambda a: jnp.transpose(a.reshape(a.shape[0], a.shape[1], N_HEADS, HEAD_DIM, a.shape[3]), (0, 1, 4, 2, 3))
    swap = lambda a: jnp.transpose(a, (0, 2, 1, 3))
    kbuf, vbuf = to_cm(cache_win_k), to_cm(cache_win_v)
    sa_all, sc_all, sp_all = swap(state_conv_a), swap(state_conv_c), swap(state_pool)

    xp = x_prompt.reshape(n * s, d)
    xs = x_sample.reshape(ns, d)
    gf = row(final_g)
    kt_p = vt_p = kt_s = vt_s = None
    st_p = [[] for _ in range(3)]
    st_s = [[] for _ in range(3)]
    w_in, w_out, w_gu, w_down = (w.astype(BF16) for w in (w_in, w_out, w_gu, w_down))
    for l in range(depth):
        pool_bd = _block_diag(pool_w[l]).astype(BF16)
        small = (conv_a_w[l], row(conv_a_b[l]), row(ln_a_g[l]), row(ln_a_b[l]), conv_c_w[l], pool_bd, row(pool_scale[l]))
        final = l == depth - 1

        ag, qs, ks, vs, cd = _inproj(xs, row(norm1_g[l]), w_in, l)
        q, k, v, ya, yc, yd, *states, kt_p, vt_p = _inmix_prompt(xp, row(norm1_g[l]), w_in, *small, l, n, tm, kt_p, vt_p)
        ob = _attn_prompt(q.reshape(n, s, gw), k.reshape(n, s, gw), v.reshape(n, s, gw)).reshape(n * s, gw)
        xp, obs, kt_s, vt_s = _outffn_prompt(
            xp, ya, ob, yc, yd, w_out, row(norm2_g[l]), w_gu, w_down, gf,
            qs.reshape(ns, 1, gw), ks.reshape(ns, 1, gw), vs.reshape(ns, 1, gw), kbuf, vbuf, kt_s, vt_s, l, tm, final)
        for lst, a in zip(st_p, states):
            lst.append(a)
        xs, *states = _sample_rest(ag, cd, obs.reshape(ns, gw), xs, sa_all, sc_all, sp_all, *small, w_out,
                                   row(norm2_g[l]), w_gu, w_down, gf, l, PAST_LEN, final)
        for lst, a in zip(st_s, states):
            lst.append(a)

    y_prompt = xp.reshape(n, s, d)
    y_sample = xs.reshape(ns, ts, d)
    return (y_prompt, y_sample, from_cm(kt_p), from_cm(vt_p), *[jnp.stack(a, axis=0) for a in st_p],
            from_cm(kt_s), from_cm(vt_s), *[swap(jnp.stack(a, axis=0)) for a in st_s])
```

```python
import functools
import math

import jax
import jax.numpy as jnp
from jax import lax
from jax.experimental import pallas as pl
from jax.experimental.pallas import tpu as pltpu

F32 = jnp.float32
BF16 = jnp.bfloat16

GROUP_W = 256
HEAD_DIM = 64
N_HEADS = GROUP_W // HEAD_DIM
CONV_A_WIDTH = 31
CONV_C_WIDTH = 3
POOL_WINDOWS = (2, 4, 8, 16)
POOL_STATE = max(POOL_WINDOWS) - 1
DIL_CONFIGS = ((128, 1), (512, 4), (2048, 16))
Q_BLOCK = 128
PAST_LEN = 16384
ATTN_SCALE = 1.0 / math.sqrt(HEAD_DIM)
EPS = 1e-6
NEG = -1e30

VMEM_LIMIT_BYTES = 56 * 1024 * 1024


def _cparams(*sem):
    return pltpu.CompilerParams(dimension_semantics=sem, vmem_limit_bytes=VMEM_LIMIT_BYTES)


def _rmsnorm(x, g):
    return x * lax.rsqrt(jnp.mean(x * x, axis=-1, keepdims=True) + EPS) * g


def _layernorm(x, g, b):
    mu = jnp.mean(x, axis=-1, keepdims=True)
    xc = x - mu
    return xc * lax.rsqrt(jnp.mean(xc * xc, axis=-1, keepdims=True) + EPS) * g + b


def _layer_spec(a, layer):
    return pl.BlockSpec((1,) + a.shape[1:], lambda *_: (layer,) + (0,) * (a.ndim - 1))


def _dot(a, b):
    return jnp.dot(a, b, preferred_element_type=F32)


def _head_of_lane(shape, dim):
    return lax.broadcasted_iota(jnp.int32, shape, dim) // HEAD_DIM


def _inproj_kernel(x_ref, g_ref, w_ref, ag_ref, q_ref, k_ref, v_ref, cd_ref):
    w_ref = w_ref.at[0]
    h = _rmsnorm(x_ref[...], g_ref[...]).astype(BF16)
    gw = GROUP_W
    ag_ref[...] = _dot(h, w_ref[:, 0:2 * gw])
    q_ref[...] = _dot(h, w_ref[:, 2 * gw:3 * gw]) * ATTN_SCALE
    k_ref[...] = _dot(h, w_ref[:, 3 * gw:4 * gw])
    v_ref[...] = _dot(h, w_ref[:, 4 * gw:5 * gw])
    cd_ref[...] = _dot(h, w_ref[:, 5 * gw:9 * gw])


def _inproj(x, g, w_in, layer):
    t, d = x.shape
    gw = GROUP_W
    full = lambda a: pl.BlockSpec(a.shape, lambda i: (0,) * a.ndim)
    widths = (2 * gw, gw, gw, gw, 4 * gw)
    return pl.pallas_call(
        _inproj_kernel,
        grid=(1,),
        in_specs=[full(x), full(g), _layer_spec(w_in, layer)],
        out_specs=[pl.BlockSpec((t, w), lambda i: (0, 0)) for w in widths],
        out_shape=[jax.ShapeDtypeStruct((t, w), F32) for w in widths],
        compiler_params=_cparams("arbitrary"),
        name="inproj",
    )(x, g, w_in)


HEAD_GROUPS = ((0, 1), (2, 3))
MID_DIL = 4
UNITS_PER_STEP = 16


def _attn_kernel(q_ref, k_ref, v_ref, ob_ref, q_scr, k_scr, v_scr, q_mid, k_mid, v_mid, qc, kc, vtc, o_scr, l_scr, *, seq):
    qb = Q_BLOCK
    hw = GROUP_W // 2
    for src, dst in ((q_ref, q_scr), (k_ref, k_scr), (v_ref, v_scr)):
        for half in range(2):
            dst[half] = src[0, :, half * hw:(half + 1) * hw]
    kc[0:qb, :] = jnp.zeros((qb, GROUP_W), BF16)
    vtc[:, 0:qb] = jnp.zeros((GROUP_W, qb), BF16)

    gh = len(HEAD_GROUPS[0])
    head_q = _head_of_lane((qb, GROUP_W), 1)
    kk = lax.broadcasted_iota(jnp.int32, (2 * qb, gh * qb), 0)
    qi = lax.broadcasted_iota(jnp.int32, (2 * qb, gh * qb), 1) % qb
    band = jnp.logical_or(jnp.logical_and(kk < qb, kk >= qi), jnp.logical_and(kk >= qb, kk - qb <= qi))

    for c, (_, dil) in enumerate(DIL_CONFIGS):
        n_blocks = seq // (dil * qb)
        n_steps = seq // qb // UNITS_PER_STEP
        rows = lambda s0, dil=dil: pl.ds(s0, qb, stride=dil) if dil > 1 else pl.ds(s0, qb)

        def block_start(u, dil=dil, n_blocks=n_blocks):
            start = u // n_blocks + (u % n_blocks) * (qb * dil)
            return pl.multiple_of(start, qb) if dil == 1 else start

        two_hop = dil > MID_DIL and dil % MID_DIL == 0
        assert not two_hop or MID_DIL in [d for _, d in DIL_CONFIGS[:c]]

        def stage(u, carry, dil=dil, n_blocks=n_blocks, rows=rows, block_start=block_start, two_hop=two_hop):
            if two_hop:
                r, b = u // n_blocks, u % n_blocks
                start = (r % MID_DIL) * (seq // MID_DIL) + r // MID_DIL + b * (qb * dil // MID_DIL)
                src_rows = pl.ds(start, qb, stride=dil // MID_DIL)
                srcs = (q_mid, k_mid, v_mid)
            else:
                src_rows = rows(block_start(u))
                srcs = (q_scr, k_scr, v_scr)
            q, k, v = (jnp.concatenate([scr[0, src_rows, :], scr[1, src_rows, :]], axis=1) for scr in srcs)
            dst = pl.multiple_of(u * qb, qb)
            qc[pl.ds(dst, qb), :] = q
            kc[pl.ds(dst + qb, qb), :] = k.astype(BF16)
            vtc[:, pl.ds(dst + qb, qb)] = v.T.astype(BF16)
            if dil == MID_DIL:
                for half in range(2):
                    for val, mid in ((q, q_mid), (k, k_mid), (v, v_mid)):
                        mid[half, pl.ds(dst, qb), :] = val[:, half * hw:(half + 1) * hw]
            return carry

        lax.fori_loop(0, seq // qb, stage, 0, unroll=4)

        def unit(u, carry, c=c, n_blocks=n_blocks, rows=rows, block_start=block_start):
            us = [u * UNITS_PER_STEP + i for i in range(UNITS_PER_STEP)]
            w0s = [pl.multiple_of(ui * qb, qb) for ui in us]
            chains = [(i, g) for i in range(UNITS_PER_STEP) for g in range(len(HEAD_GROUPS))]
            ch = gh * HEAD_DIM
            scores = []
            for i, g in chains:
                q = qc[pl.ds(w0s[i], qb), :]
                kwin = kc[pl.ds(w0s[i], 2 * qb), :]
                qm = jnp.concatenate([jnp.where(head_q == h, q, 0.0) for h in HEAD_GROUPS[g]], axis=0).astype(BF16)
                scores.append(lax.dot_general(kwin, qm, (((1,), (1,)), ((), ())), preferred_element_type=F32))
            probs, dens, lses = [], [], []
            for (i, g), s in zip(chains, scores):
                valid = jnp.logical_and(band, jnp.logical_or(kk >= qb, us[i] % n_blocks > 0))
                s = jnp.where(valid, s, NEG)
                m = jnp.max(s, axis=0, keepdims=True)
                p = jnp.exp(s - m)
                den = jnp.sum(p, axis=0, keepdims=True)
                probs.append(p.astype(BF16))
                dens.append(den)
                lses.append(m + jnp.log(den))
            nums = [_dot(vtc[g * ch:(g + 1) * ch, pl.ds(w0s[i], 2 * qb)], p) for (i, g), p in zip(chains, probs)]
            for i in range(UNITS_PER_STEP):
                o_t, l_t = [], []
                for g in range(len(HEAD_GROUPS)):
                    n = chains.index((i, g))
                    for j in range(gh):
                        cols = slice(j * qb, (j + 1) * qb)
                        o_t.append(nums[n][j * HEAD_DIM:(j + 1) * HEAD_DIM, cols] / dens[n][:, cols])
                        l_t.append(jnp.broadcast_to(lses[n][:, cols], (HEAD_DIM, qb)))
                o = jnp.concatenate(o_t, axis=0).T
                l = jnp.concatenate(l_t, axis=0).T
                start = block_start(us[i])
                for half in range(2):
                    o_scr[c, half, rows(start), :] = o[:, half * hw:(half + 1) * hw]
                    l_scr[c, half, rows(start), :] = l[:, half * hw:(half + 1) * hw]
            return carry

        lax.fori_loop(0, n_steps, unit, 0)

    for half in range(2):
        l0, l1, l2 = l_scr[0, half], l_scr[1, half], l_scr[2, half]
        m = jnp.maximum(jnp.maximum(l0, l1), l2)
        e0, e1, e2 = jnp.exp(l0 - m), jnp.exp(l1 - m), jnp.exp(l2 - m)
        ob_ref[0, :, half * hw:(half + 1) * hw] = (
            (e0 * o_scr[0, half] + e1 * o_scr[1, half] + e2 * o_scr[2, half]) / (e0 + e1 + e2)).astype(ob_ref.dtype)


def _attn_prompt(q, k, v):
    n, s, w = q.shape
    blk = pl.BlockSpec((1, s, w), lambda i: (i, 0, 0))
    return pl.pallas_call(
        functools.partial(_attn_kernel, seq=s),
        grid=(n,),
        in_specs=[blk, blk, blk],
        out_specs=blk,
        out_shape=jax.ShapeDtypeStruct((n, s, w), BF16),
        scratch_shapes=[pltpu.VMEM((2, s, w // 2), F32)] * 6
        + [pltpu.VMEM((s, w), F32), pltpu.VMEM((Q_BLOCK + s, w), BF16), pltpu.VMEM((w, Q_BLOCK + s), BF16)]
        + [pltpu.VMEM((len(DIL_CONFIGS), 2, s, w // 2), F32)] * 2,
        compiler_params=_cparams("parallel"),
        name="attn_prompt",
    )(q, k, v)


def _pool_select(s2, s4, s8, s16):
    grp = _head_of_lane(s2.shape, s2.ndim - 1)
    return jnp.where(grp == 0, s2, jnp.where(grp == 1, s4, jnp.where(grp == 2, s8, s16)))


def _pool_window_lanes(shape):
    grp = _head_of_lane(shape, len(shape) - 1)
    return jnp.where(grp == 0, 2, jnp.where(grp == 1, 4, jnp.where(grp == 2, 8, 16)))


def _out_proj(ya, ob, yc, yd, w_out_ref):
    gw = GROUP_W
    acc = _dot(ya.astype(BF16), w_out_ref[0:gw, :])
    acc += _dot(ob.astype(BF16), w_out_ref[gw:2 * gw, :])
    acc += _dot(yc.astype(BF16), w_out_ref[2 * gw:3 * gw, :])
    acc += _dot(yd.astype(BF16), w_out_ref[3 * gw:4 * gw, :])
    return acc


def _ffn(x, g2, w_gu_ref, w_down_ref, ff_chunk):
    ff = w_down_ref.shape[0]
    h = _rmsnorm(x, g2).astype(BF16)
    acc = jnp.zeros_like(x)
    for c in range(ff // ff_chunk):
        lo = c * ff_chunk
        g = _dot(h, w_gu_ref[:, lo:lo + ff_chunk])
        u = _dot(h, w_gu_ref[:, ff + lo:ff + lo + ff_chunk])
        acc += _dot((jax.nn.silu(g) * u).astype(BF16), w_down_ref[lo:lo + ff_chunk, :])
    return x + acc


A_PAD = 32
C_PAD = 8
P_PAD = 16
SUBLANES = 8


def _causal_conv(buf, w_ref, width, pad, tm):
    base = pad - SUBLANES
    rows = tm + SUBLANES
    y = None
    for a in range(min(SUBLANES, width)):
        z = None
        for lag in range(a, width, SUBLANES):
            term = w_ref[width - 1 - lag:width - lag, :] * buf[base - (lag - a):base - (lag - a) + rows, :]
            z = term if z is None else z + term
        z = pltpu.roll(z, a, 0) if a else z
        y = z if y is None else y + z
    return y[SUBLANES:SUBLANES + tm]


def _inmix_kernel(x_ref, g_ref, w_ref, caw_ref, cab_ref, lng_ref, lnb_ref, ccw_ref, pw_ref, ps_ref, *refs, n_alias, tm):
    (q_ref, k_ref, v_ref, ya_ref, yc_ref, yd_ref, sta_ref, stc_ref, stp_ref, kt_ref, vt_ref,
     abuf, cbuf, pbuf) = refs[n_alias:]
    gw = GROUP_W
    w_ref = w_ref.at[0]
    j = pl.program_id(1)

    @pl.when(j == 0)
    def _():
        abuf[0:A_PAD, :] = jnp.zeros((A_PAD, gw), F32)
        cbuf[0:C_PAD, :] = jnp.zeros((C_PAD, gw), F32)
        pbuf[0:P_PAD, :] = jnp.zeros((P_PAD, gw), F32)

    h = _rmsnorm(x_ref[...], g_ref[...]).astype(BF16)

    abuf[A_PAD:A_PAD + tm, :] = _dot(h, w_ref[:, 0:gw]) * jax.nn.sigmoid(_dot(h, w_ref[:, gw:2 * gw]))
    acc = _causal_conv(abuf, caw_ref, CONV_A_WIDTH, A_PAD, tm)
    ya_ref[...] = jax.nn.silu(_layernorm(acc + cab_ref[...], lng_ref[...], lnb_ref[...])).astype(BF16)

    q_ref[...] = _dot(h, w_ref[:, 2 * gw:3 * gw]) * ATTN_SCALE
    k = _dot(h, w_ref[:, 3 * gw:4 * gw])
    v = _dot(h, w_ref[:, 4 * gw:5 * gw])
    k_ref[...] = k
    v_ref[...] = v
    kt_ref[0, 0] = k.T
    vt_ref[0, 0] = v.T

    cbuf[C_PAD:C_PAD + tm, :] = _dot(h, w_ref[:, 7 * gw:8 * gw]) * _dot(h, w_ref[:, 5 * gw:6 * gw])
    yc_ref[...] = (_dot(h, w_ref[:, 6 * gw:7 * gw]) * _causal_conv(cbuf, ccw_ref, CONV_C_WIDTH, C_PAD, tm)).astype(BF16)

    u = _dot(h, w_ref[:, 8 * gw:9 * gw])
    pbuf[P_PAD:P_PAD + tm, :] = u
    s1 = pbuf[...]
    s2 = s1 + pltpu.roll(s1, 1, 0)
    s4 = s2 + pltpu.roll(s2, 2, 0)
    s8 = s4 + pltpu.roll(s4, 4, 0)
    s16 = s8 + pltpu.roll(s8, 8, 0)
    win = _pool_select(s2, s4, s8, s16)[P_PAD:P_PAD + tm]
    pos = j * tm + lax.broadcasted_iota(jnp.int32, (tm, gw), 0)
    cnt = jnp.minimum(_pool_window_lanes((tm, gw)), pos + 1).astype(F32)
    yd_ref[...] = (_dot((win / cnt - u).astype(BF16), pw_ref[...]) * ps_ref[...]).astype(BF16)

    @pl.when(j == pl.num_programs(1) - 1)
    def _():
        sta_ref[0] = abuf[A_PAD + tm - (CONV_A_WIDTH - 1):A_PAD + tm, :]
        stc_ref[0] = cbuf[C_PAD + tm - (CONV_C_WIDTH - 1):C_PAD + tm, :]
        stp_ref[0] = pbuf[P_PAD + tm - POOL_STATE:P_PAD + tm, :]

    abuf[0:A_PAD, :] = abuf[tm:tm + A_PAD, :]
    cbuf[0:C_PAD, :] = cbuf[tm:tm + C_PAD, :]
    pbuf[0:P_PAD, :] = pbuf[tm:tm + P_PAD, :]


def _inmix_prompt(x, g, w_in, caw, cab, lng, lnb, ccw, pool_bd, ps, layer, n, tm, kt_all, vt_all):
    t, d = x.shape
    gw = GROUP_W
    depth = w_in.shape[0]
    spb = t // n // tm
    row = lambda w: pl.BlockSpec((tm, w), lambda i, j: (i * spb + j, 0))
    full = lambda a: pl.BlockSpec(a.shape, lambda i, j: (0,) * a.ndim)
    st = lambda r: pl.BlockSpec((1, r, gw), lambda i, j: (i, 0, 0))
    st_rows = (CONV_A_WIDTH - 1, CONV_C_WIDTH - 1, POOL_STATE)
    cm = pl.BlockSpec((1, 1, gw, tm), lambda i, j: (layer, i, 0, j))
    small = (caw, cab, lng, lnb, ccw, pool_bd, ps)
    in_specs = [row(d), full(g), _layer_spec(w_in, layer)] + [full(a) for a in small]
    args, aliases = [x, g, w_in, *small], {}
    if kt_all is not None:
        in_specs += [pl.BlockSpec(memory_space=pl.ANY)] * 2
        aliases = {len(args): 9, len(args) + 1: 10}
        args += [kt_all, vt_all]
    return pl.pallas_call(
        functools.partial(_inmix_kernel, n_alias=len(aliases), tm=tm),
        grid=(n, spb),
        in_specs=in_specs,
        out_specs=[row(gw)] * 6 + [st(r) for r in st_rows] + [cm, cm],
        out_shape=[jax.ShapeDtypeStruct((t, gw), F32)] * 3 + [jax.ShapeDtypeStruct((t, gw), BF16)] * 3
        + [jax.ShapeDtypeStruct((n, r, gw), F32) for r in st_rows]
        + [jax.ShapeDtypeStruct((depth, n, gw, t // n), F32)] * 2,
        input_output_aliases=aliases,
        scratch_shapes=[pltpu.VMEM((A_PAD + tm, gw), F32), pltpu.VMEM((C_PAD + tm, gw), F32),
                        pltpu.VMEM((P_PAD + tm, gw), F32)],
        compiler_params=_cparams("parallel", "arbitrary"),
        name="inmix_prompt",
    )(*args)


def _outffn_kernel(x_ref, ya_ref, ob_ref, yc_ref, yd_ref, wo_ref, g2_ref, wgu_ref, wd_ref, gf_ref,
                   qs_ref, kn_ref, vn_ref, kb_ref, vb_ref, *refs, n_alias, ff_chunk, final):
    o_ref, obs_ref, ko_ref, vo_ref = refs[n_alias:]
    seq_row = pl.ds(pl.program_id(0), 1)
    obs_ref[seq_row, :] = _cache_job(qs_ref[seq_row, :], kn_ref[seq_row, :], vn_ref[seq_row, :], kb_ref, vb_ref,
                                     ko_ref, vo_ref)
    x1 = x_ref[...] + _out_proj(ya_ref[...], ob_ref[...], yc_ref[...], yd_ref[...], wo_ref.at[0])
    x2 = _ffn(x1, g2_ref[...], wgu_ref.at[0], wd_ref.at[0], ff_chunk)
    o_ref[...] = _rmsnorm(x2, gf_ref[...]) if final else x2


def _outffn_prompt(x, ya, ob, yc, yd, w_out, g2, w_gu, w_down, gf, qs, kn, vn, kbuf, vbuf, k_all, v_all, layer, tm, final):
    t, d = x.shape
    depth, ns, w, buf = kbuf.shape
    assert t // tm == ns
    row = lambda a: pl.BlockSpec((tm, a.shape[1]), lambda i: (i, 0))
    full = lambda a: pl.BlockSpec(a.shape, lambda i: (0,) * a.ndim)
    tok = full(qs)
    blk = pl.BlockSpec((1, 1, w, buf), lambda i: (layer, i, 0, 0))
    in_specs = ([row(a) for a in (x, ya, ob, yc, yd)]
                + [_layer_spec(w_out, layer), full(g2), _layer_spec(w_gu, layer), _layer_spec(w_down, layer), full(gf)]
                + [tok, tok, tok, blk, blk])
    args, aliases = [x, ya, ob, yc, yd, w_out, g2, w_gu, w_down, gf, qs, kn, vn, kbuf, vbuf], {}
    if k_all is not None:
        in_specs += [pl.BlockSpec(memory_space=pl.ANY)] * 2
        aliases = {len(args): 2, len(args) + 1: 3}
        args += [k_all, v_all]
    return pl.pallas_call(
        functools.partial(_outffn_kernel, n_alias=len(aliases), ff_chunk=256, final=final),
        grid=(t // tm,),
        in_specs=in_specs,
        out_specs=[row(x), tok, blk, blk],
        out_shape=[jax.ShapeDtypeStruct((t, d), F32), jax.ShapeDtypeStruct((ns, w), F32)]
        + [jax.ShapeDtypeStruct(kbuf.shape, F32)] * 2,
        input_output_aliases=aliases,
        compiler_params=_cparams("arbitrary"),
        name="outffn_prompt",
    )(*args)


def _cache_job(q_row, kn_row, vn_row, kb_ref, vb_ref, ko_ref, vo_ref):
    gw = GROUP_W
    buf = kb_ref.shape[-1]
    assert max(win for win, _ in DIL_CONFIGS) <= buf
    ident = lax.broadcasted_iota(jnp.int32, (gw, gw), 0) == lax.broadcasted_iota(jnp.int32, (gw, gw), 1)
    to_col = lambda row: jnp.sum(jnp.where(ident, row, 0.0), axis=1, keepdims=True)
    to_row = lambda col: jnp.sum(jnp.where(ident, col, 0.0), axis=0, keepdims=True)
    qc, kc, vc = to_col(q_row), to_col(kn_row), to_col(vn_row)
    kb, vb = kb_ref[0, 0], vb_ref[0, 0]
    last = lax.broadcasted_iota(jnp.int32, (gw, buf), 1) == buf - 1
    ko_ref[0, 0] = jnp.where(last, kc, pltpu.roll(kb, buf - 1, 1))
    vo_ref[0, 0] = jnp.where(last, vc, pltpu.roll(vb, buf - 1, 1))

    def heads(a):
        return jnp.sum(a.reshape(N_HEADS, HEAD_DIM, a.shape[-1]), axis=1)

    def spread(a):
        return jnp.broadcast_to(a[:, None, :], (N_HEADS, HEAD_DIM, a.shape[-1])).reshape(gw, a.shape[-1])

    s_all = heads(kb * qc)
    s_new = heads(kc * qc)
    outs, lses = [], []
    for win, dil in DIL_CONFIGS:
        lo = buf - win
        s = s_all[:, lo:]
        if dil > 1:
            back = win - lax.broadcasted_iota(jnp.int32, s.shape, 1)
            s = jnp.where(back % dil == 0, s, NEG)
        m = jnp.maximum(jnp.max(s, axis=1, keepdims=True), s_new)
        p, p_new = jnp.exp(s - m), jnp.exp(s_new - m)
        den = jnp.sum(p, axis=1, keepdims=True) + p_new
        pv = jnp.sum(vb[:, lo:] * spread(p), axis=1, keepdims=True) + vc * spread(p_new)
        outs.append(pv / spread(den))
        lses.append(spread(m + jnp.log(den)))
    m = jnp.maximum(jnp.maximum(lses[0], lses[1]), lses[2])
    es = [jnp.exp(l - m) for l in lses]
    return to_row((es[0] * outs[0] + es[1] * outs[1] + es[2] * outs[2]) / (es[0] + es[1] + es[2]))


def _sample_rest_kernel(ag_ref, cd_ref, ob_ref, x_ref, sa_ref, sc_ref, sp_ref, caw_ref, cab_ref, lng_ref, lnb_ref,
                        ccw_ref, pw_ref, ps_ref, wo_ref, g2_ref, wgu_ref, wd_ref, gf_ref,
                        y_ref, na_ref, nc_ref, np_ref, *, pos0, final):
    gw = GROUP_W
    ga = ag_ref[:, 0:gw] * jax.nn.sigmoid(ag_ref[:, gw:2 * gw])
    na = CONV_A_WIDTH - 1
    acc = caw_ref[na:na + 1, :] * ga
    for t in range(na):
        acc += caw_ref[t:t + 1, :] * sa_ref[0, t]
    ya = jax.nn.silu(_layernorm(acc + cab_ref[...], lng_ref[...], lnb_ref[...]))
    na_ref[0:na - 1] = sa_ref[0, 1:na]
    na_ref[na - 1] = ga

    cx = cd_ref[:, 2 * gw:3 * gw] * cd_ref[:, 0:gw]
    nc = CONV_C_WIDTH - 1
    acc = ccw_ref[nc:nc + 1, :] * cx
    for t in range(nc):
        acc += ccw_ref[t:t + 1, :] * sc_ref[0, t]
    yc = cd_ref[:, gw:2 * gw] * acc
    nc_ref[0:nc - 1] = sc_ref[0, 1:nc]
    nc_ref[nc - 1] = cx

    u = cd_ref[:, 3 * gw:4 * gw]
    npl = POOL_STATE
    back = lambda i: sp_ref[0, npl - i]
    s2 = u + back(1)
    s4 = s2 + back(2) + back(3)
    s8 = s4 + back(4) + back(5) + back(6) + back(7)
    s16 = s8
    for i in range(8, 16):
        s16 = s16 + back(i)
    cnt = jnp.minimum(_pool_window_lanes(u.shape), pos0 + 1).astype(F32)
    yd = _dot((_pool_select(s2, s4, s8, s16) / cnt - u).astype(BF16), pw_ref[...]) * ps_ref[...]
    np_ref[0:npl - 1] = sp_ref[0, 1:npl]
    np_ref[npl - 1] = u

    x1 = x_ref[...] + _out_proj(ya, ob_ref[...], yc, yd, wo_ref.at[0])
    x2 = _ffn(x1, g2_ref[...], wgu_ref.at[0], wd_ref.at[0], 256)
    y_ref[...] = _rmsnorm(x2, gf_ref[...]) if final else x2


def _sample_rest(ag, cd, ob, x, sa, sc, sp, caw, cab, lng, lnb, ccw, pool_bd, ps, w_out, g2, w_gu, w_down, gf, layer, pos0, final):
    full = lambda a: pl.BlockSpec(a.shape, lambda i: (0,) * a.ndim)
    st_in = lambda a: pl.BlockSpec((1,) + a.shape[1:], lambda i: (layer, 0, 0, 0))
    st_out = lambda a: pl.BlockSpec(a.shape[1:], lambda i: (0, 0, 0))
    dense = (caw, cab, lng, lnb, ccw, pool_bd, ps, w_out, g2, w_gu, w_down, gf)
    stacked = (w_out, w_gu, w_down)
    dense_spec = lambda a: _layer_spec(a, layer) if any(a is w for w in stacked) else full(a)
    return pl.pallas_call(
        functools.partial(_sample_rest_kernel, pos0=pos0, final=final),
        grid=(1,),
        in_specs=[full(a) for a in (ag, cd, ob, x)] + [st_in(a) for a in (sa, sc, sp)] + [dense_spec(a) for a in dense],
        out_specs=[full(x)] + [st_out(a) for a in (sa, sc, sp)],
        out_shape=[jax.ShapeDtypeStruct(x.shape, F32)] + [jax.ShapeDtypeStruct(a.shape[1:], F32) for a in (sa, sc, sp)],
        compiler_params=_cparams("arbitrary"),
        name="sample_rest",
    )(ag, cd, ob, x, sa, sc, sp, *dense)


def _block_diag(pool_w):
    g, c, e = pool_w.shape
    eye = jnp.eye(g, dtype=pool_w.dtype)
    return (pool_w[:, :, None, :] * eye[:, None, :, None]).reshape(g * c, g * e)


def kernel(x_prompt, x_sample, cache_win_k, cache_win_v, state_conv_a, state_conv_c, state_pool, w_in, conv_a_w, conv_a_b, ln_a_g, ln_a_b, conv_c_w, pool_w, pool_scale, w_out, norm1_g, norm2_g, w_gu, w_down, final_g):
    depth = w_in.shape[0]
    n, s, d = x_prompt.shape
    ns, ts, _ = x_sample.shape
    assert ts == 1
    gw = GROUP_W
    tm = 512

    row = lambda a: a.reshape(1, -1)
    to_cm = lambda a: jnp.transpose(a, (0, 1, 3, 4, 2)).reshape(a.shape[0], a.shape[1], gw, a.shape[2])
    from_cm = lambda a: jnp.transpose(a.reshape(a.shape[0], a.shape[1], N_HEADS, HEAD_DIM, a.shape[3]), (0, 1, 4, 2, 3))
    swap = lambda a: jnp.transpose(a, (0, 2, 1, 3))
    kbuf, vbuf = to_cm(cache_win_k), to_cm(cache_win_v)
    sa_all, sc_all, sp_all = swap(state_conv_a), swap(state_conv_c), swap(state_pool)

    xp = x_prompt.reshape(n * s, d)
    xs = x_sample.reshape(ns, d)
    gf = row(final_g)
    kt_p = vt_p = kt_s = vt_s = None
    st_p = [[] for _ in range(3)]
    st_s = [[] for _ in range(3)]
    w_in, w_out, w_gu, w_down = (w.astype(BF16) for w in (w_in, w_out, w_gu, w_down))
    for l in range(depth):
        pool_bd = _block_diag(pool_w[l]).astype(BF16)
        small = (conv_a_w[l], row(conv_a_b[l]), row(ln_a_g[l]), row(ln_a_b[l]), conv_c_w[l], pool_bd, row(pool_scale[l]))
        final = l == depth - 1

        ag, qs, ks, vs, cd = _inproj(xs, row(norm1_g[l]), w_in, l)
        q, k, v, ya, yc, yd, *states, kt_p, vt_p = _inmix_prompt(xp, row(norm1_g[l]), w_in, *small, l, n, tm, kt_p, vt_p)
        ob = _attn_prompt(q.reshape(n, s, gw), k.reshape(n, s, gw), v.reshape(n, s, gw)).reshape(n * s, gw)
        xp, obs, kt_s, vt_s = _outffn_prompt(
            xp, ya, ob, yc, yd, w_out, row(norm2_g[l]), w_gu, w_down, gf,
            qs, ks, vs, kbuf, vbuf, kt_s, vt_s, l, tm, final)
        for lst, a in zip(st_p, states):
            lst.append(a)
        xs, *states = _sample_rest(ag, cd, obs, xs, sa_all, sc_all, sp_all, *small, w_out,
                                   row(norm2_g[l]), w_gu, w_down, gf, l, PAST_LEN, final)
        for lst, a in zip(st_s, states):
            lst.append(a)

    y_prompt = xp.reshape(n, s, d)
    y_sample = xs.reshape(ns, ts, d)
    return (y_prompt, y_sample, from_cm(kt_p), from_cm(vt_p), *[jnp.stack(a, axis=0) for a in st_p],
            from_cm(kt_s), from_cm(vt_s), *[swap(jnp.stack(a, axis=0)) for a in st_s])
```

```python
import functools
import math

import jax
import jax.numpy as jnp
from jax import lax
from jax.experimental import pallas as pl
from jax.experimental.pallas import tpu as pltpu

F32 = jnp.float32
BF16 = jnp.bfloat16

GROUP_W = 256
HEAD_DIM = 64
N_HEADS = GROUP_W // HEAD_DIM
CONV_A_WIDTH = 31
CONV_C_WIDTH = 3
POOL_WINDOWS = (2, 4, 8, 16)
POOL_STATE = max(POOL_WINDOWS) - 1
DIL_CONFIGS = ((128, 1), (512, 4), (2048, 16))
Q_BLOCK = 128
PAST_LEN = 16384
ATTN_SCALE = 1.0 / math.sqrt(HEAD_DIM)
EPS = 1e-6
NEG = -1e30

VMEM_LIMIT_BYTES = 56 * 1024 * 1024


def _cparams(*sem):
    return pltpu.CompilerParams(dimension_semantics=sem, vmem_limit_bytes=VMEM_LIMIT_BYTES)


def _rmsnorm(x, g):
    return x * lax.rsqrt(jnp.mean(x * x, axis=-1, keepdims=True) + EPS) * g


def _layernorm(x, g, b):
    mu = jnp.mean(x, axis=-1, keepdims=True)
    xc = x - mu
    return xc * lax.rsqrt(jnp.mean(xc * xc, axis=-1, keepdims=True) + EPS) * g + b


def _layer_spec(a, layer):
    layer = min(layer, a.shape[0] - 1)
    return pl.BlockSpec((1,) + a.shape[1:], lambda *_: (layer,) + (0,) * (a.ndim - 1))


def _dot(a, b):
    return jnp.dot(a, b, preferred_element_type=F32)


def _head_of_lane(shape, dim):
    return lax.broadcasted_iota(jnp.int32, shape, dim) // HEAD_DIM


def _inproj_kernel(x_ref, g_ref, w_ref, ag_ref, q_ref, k_ref, v_ref, cd_ref):
    w_ref = w_ref.at[0]
    h = _rmsnorm(x_ref[...], g_ref[...]).astype(BF16)
    gw = GROUP_W
    ag_ref[...] = _dot(h, w_ref[:, 0:2 * gw])
    q_ref[...] = _dot(h, w_ref[:, 2 * gw:3 * gw]) * ATTN_SCALE
    k_ref[...] = _dot(h, w_ref[:, 3 * gw:4 * gw])
    v_ref[...] = _dot(h, w_ref[:, 4 * gw:5 * gw])
    cd_ref[...] = _dot(h, w_ref[:, 5 * gw:9 * gw])


def _inproj(x, g, w_in, layer):
    t, d = x.shape
    gw = GROUP_W
    full = lambda a: pl.BlockSpec(a.shape, lambda i: (0,) * a.ndim)
    widths = (2 * gw, gw, gw, gw, 4 * gw)
    return pl.pallas_call(
        _inproj_kernel,
        grid=(1,),
        in_specs=[full(x), full(g), _layer_spec(w_in, layer)],
        out_specs=[pl.BlockSpec((t, w), lambda i: (0, 0)) for w in widths],
        out_shape=[jax.ShapeDtypeStruct((t, w), F32) for w in widths],
        compiler_params=_cparams("arbitrary"),
        name="inproj",
    )(x, g, w_in)


HEAD_GROUPS = ((0, 1), (2, 3))
MID_DIL = 4
UNITS_PER_STEP = 16


def _attn_kernel(q_ref, k_ref, v_ref, ob_ref, q_scr, k_scr, v_scr, q_mid, k_mid, v_mid, qc, kc, vtc, o_scr, l_scr, *, seq):
    qb = Q_BLOCK
    hw = GROUP_W // 2
    for src, dst in ((q_ref, q_scr), (k_ref, k_scr), (v_ref, v_scr)):
        for half in range(2):
            dst[half] = src[0, :, half * hw:(half + 1) * hw]
    kc[0:qb, :] = jnp.zeros((qb, GROUP_W), BF16)
    vtc[:, 0:qb] = jnp.zeros((GROUP_W, qb), BF16)

    gh = len(HEAD_GROUPS[0])
    head_q = _head_of_lane((qb, GROUP_W), 1)
    kk = lax.broadcasted_iota(jnp.int32, (2 * qb, gh * qb), 0)
    qi = lax.broadcasted_iota(jnp.int32, (2 * qb, gh * qb), 1) % qb
    band = jnp.logical_or(jnp.logical_and(kk < qb, kk >= qi), jnp.logical_and(kk >= qb, kk - qb <= qi))

    for c, (_, dil) in enumerate(DIL_CONFIGS):
        n_blocks = seq // (dil * qb)
        n_steps = seq // qb // UNITS_PER_STEP
        rows = lambda s0, dil=dil: pl.ds(s0, qb, stride=dil) if dil > 1 else pl.ds(s0, qb)

        def block_start(u, dil=dil, n_blocks=n_blocks):
            start = u // n_blocks + (u % n_blocks) * (qb * dil)
            return pl.multiple_of(start, qb) if dil == 1 else start

        two_hop = dil > MID_DIL and dil % MID_DIL == 0
        assert not two_hop or MID_DIL in [d for _, d in DIL_CONFIGS[:c]]

        def stage(u, carry, dil=dil, n_blocks=n_blocks, rows=rows, block_start=block_start, two_hop=two_hop):
            if two_hop:
                r, b = u // n_blocks, u % n_blocks
                start = (r % MID_DIL) * (seq // MID_DIL) + r // MID_DIL + b * (qb * dil // MID_DIL)
                src_rows = pl.ds(start, qb, stride=dil // MID_DIL)
                srcs = (q_mid, k_mid, v_mid)
            else:
                src_rows = rows(block_start(u))
                srcs = (q_scr, k_scr, v_scr)
            q, k, v = (jnp.concatenate([scr[0, src_rows, :], scr[1, src_rows, :]], axis=1) for scr in srcs)
            dst = pl.multiple_of(u * qb, qb)
            qc[pl.ds(dst, qb), :] = q
            kc[pl.ds(dst + qb, qb), :] = k.astype(BF16)
            vtc[:, pl.ds(dst + qb, qb)] = v.T.astype(BF16)
            if dil == MID_DIL:
                for half in range(2):
                    for val, mid in ((q, q_mid), (k, k_mid), (v, v_mid)):
                        mid[half, pl.ds(dst, qb), :] = val[:, half * hw:(half + 1) * hw]
            return carry

        lax.fori_loop(0, seq // qb, stage, 0, unroll=4)

        def unit(u, carry, c=c, n_blocks=n_blocks, rows=rows, block_start=block_start):
            us = [u * UNITS_PER_STEP + i for i in range(UNITS_PER_STEP)]
            w0s = [pl.multiple_of(ui * qb, qb) for ui in us]
            chains = [(i, g) for i in range(UNITS_PER_STEP) for g in range(len(HEAD_GROUPS))]
            ch = gh * HEAD_DIM
            scores = []
            for i, g in chains:
                q = qc[pl.ds(w0s[i], qb), :]
                kwin = kc[pl.ds(w0s[i], 2 * qb), :]
                qm = jnp.concatenate([jnp.where(head_q == h, q, 0.0) for h in HEAD_GROUPS[g]], axis=0).astype(BF16)
                scores.append(lax.dot_general(kwin, qm, (((1,), (1,)), ((), ())), preferred_element_type=F32))
            probs, dens, lses = [], [], []
            for (i, g), s in zip(chains, scores):
                valid = jnp.logical_and(band, jnp.logical_or(kk >= qb, us[i] % n_blocks > 0))
                s = jnp.where(valid, s, NEG)
                m = jnp.max(s, axis=0, keepdims=True)
                p = jnp.exp(s - m)
                den = jnp.sum(p, axis=0, keepdims=True)
                probs.append(p.astype(BF16))
                dens.append(den)
                lses.append(m + jnp.log(den))
            nums = [_dot(vtc[g * ch:(g + 1) * ch, pl.ds(w0s[i], 2 * qb)], p) for (i, g), p in zip(chains, probs)]
            for i in range(UNITS_PER_STEP):
                o_t, l_t = [], []
                for g in range(len(HEAD_GROUPS)):
                    n = chains.index((i, g))
                    for j in range(gh):
                        cols = slice(j * qb, (j + 1) * qb)
                        o_t.append(nums[n][j * HEAD_DIM:(j + 1) * HEAD_DIM, cols] / dens[n][:, cols])
                        l_t.append(jnp.broadcast_to(lses[n][:, cols], (HEAD_DIM, qb)))
                o = jnp.concatenate(o_t, axis=0).T
                l = jnp.concatenate(l_t, axis=0).T
                start = block_start(us[i])
                for half in range(2):
                    o_scr[c, half, rows(start), :] = o[:, half * hw:(half + 1) * hw]
                    l_scr[c, half, rows(start), :] = l[:, half * hw:(half + 1) * hw]
            return carry

        lax.fori_loop(0, n_steps, unit, 0)

    for half in range(2):
        l0, l1, l2 = l_scr[0, half], l_scr[1, half], l_scr[2, half]
        m = jnp.maximum(jnp.maximum(l0, l1), l2)
        e0, e1, e2 = jnp.exp(l0 - m), jnp.exp(l1 - m), jnp.exp(l2 - m)
        ob_ref[0, :, half * hw:(half + 1) * hw] = (
            (e0 * o_scr[0, half] + e1 * o_scr[1, half] + e2 * o_scr[2, half]) / (e0 + e1 + e2)).astype(ob_ref.dtype)


def _attn_prompt(q, k, v):
    n, s, w = q.shape
    blk = pl.BlockSpec((1, s, w), lambda i: (i, 0, 0))
    return pl.pallas_call(
        functools.partial(_attn_kernel, seq=s),
        grid=(n,),
        in_specs=[blk, blk, blk],
        out_specs=blk,
        out_shape=jax.ShapeDtypeStruct((n, s, w), BF16),
        scratch_shapes=[pltpu.VMEM((2, s, w // 2), F32)] * 6
        + [pltpu.VMEM((s, w), F32), pltpu.VMEM((Q_BLOCK + s, w), BF16), pltpu.VMEM((w, Q_BLOCK + s), BF16)]
        + [pltpu.VMEM((len(DIL_CONFIGS), 2, s, w // 2), F32)] * 2,
        compiler_params=_cparams("parallel"),
        name="attn_prompt",
    )(q, k, v)


def _pool_select(s2, s4, s8, s16):
    grp = _head_of_lane(s2.shape, s2.ndim - 1)
    return jnp.where(grp == 0, s2, jnp.where(grp == 1, s4, jnp.where(grp == 2, s8, s16)))


def _pool_window_lanes(shape):
    grp = _head_of_lane(shape, len(shape) - 1)
    return jnp.where(grp == 0, 2, jnp.where(grp == 1, 4, jnp.where(grp == 2, 8, 16)))


def _out_proj(ya, ob, yc, yd, w_out_ref):
    gw = GROUP_W
    acc = _dot(ya.astype(BF16), w_out_ref[0:gw, :])
    acc += _dot(ob.astype(BF16), w_out_ref[gw:2 * gw, :])
    acc += _dot(yc.astype(BF16), w_out_ref[2 * gw:3 * gw, :])
    acc += _dot(yd.astype(BF16), w_out_ref[3 * gw:4 * gw, :])
    return acc


def _ffn(x, g2, w_gu_ref, w_down_ref, ff_chunk):
    ff = w_down_ref.shape[0]
    h = _rmsnorm(x, g2).astype(BF16)
    acc = jnp.zeros_like(x)
    for c in range(ff // ff_chunk):
        lo = c * ff_chunk
        g = _dot(h, w_gu_ref[:, lo:lo + ff_chunk])
        u = _dot(h, w_gu_ref[:, ff + lo:ff + lo + ff_chunk])
        acc += _dot((jax.nn.silu(g) * u).astype(BF16), w_down_ref[lo:lo + ff_chunk, :])
    return x + acc


A_PAD = 32
C_PAD = 8
P_PAD = 16
SUBLANES = 8


def _causal_conv(buf, w_ref, width, pad, tm):
    base = pad - SUBLANES
    rows = tm + SUBLANES
    y = None
    for a in range(min(SUBLANES, width)):
        z = None
        for lag in range(a, width, SUBLANES):
            term = w_ref[width - 1 - lag:width - lag, :] * buf[base - (lag - a):base - (lag - a) + rows, :]
            z = term if z is None else z + term
        z = pltpu.roll(z, a, 0) if a else z
        y = z if y is None else y + z
    return y[SUBLANES:SUBLANES + tm]


def _inmix_kernel(x_ref, g_ref, w_ref, caw_ref, cab_ref, lng_ref, lnb_ref, ccw_ref, pw_ref, ps_ref,
                  wo_f32, wgu_f32, wd_f32, *refs, n_alias, tm):
    (q_ref, k_ref, v_ref, ya_ref, yc_ref, yd_ref, sta_ref, stc_ref, stp_ref, kt_ref, vt_ref, wo_b16, wgu_b16, wd_b16,
     abuf, cbuf, pbuf) = refs[n_alias:]
    gw = GROUP_W
    w_ref = w_ref.at[0]
    j = pl.program_id(1)

    @pl.when(j == 0)
    def _():
        abuf[0:A_PAD, :] = jnp.zeros((A_PAD, gw), F32)
        cbuf[0:C_PAD, :] = jnp.zeros((C_PAD, gw), F32)
        pbuf[0:P_PAD, :] = jnp.zeros((P_PAD, gw), F32)
        for src, dst in ((wo_f32, wo_b16), (wgu_f32, wgu_b16), (wd_f32, wd_b16)):
            dst[...] = src[...].astype(BF16)

    h = _rmsnorm(x_ref[...], g_ref[...]).astype(BF16)

    abuf[A_PAD:A_PAD + tm, :] = _dot(h, w_ref[:, 0:gw]) * jax.nn.sigmoid(_dot(h, w_ref[:, gw:2 * gw]))
    acc = _causal_conv(abuf, caw_ref, CONV_A_WIDTH, A_PAD, tm)
    ya_ref[...] = jax.nn.silu(_layernorm(acc + cab_ref[...], lng_ref[...], lnb_ref[...])).astype(BF16)

    q_ref[...] = _dot(h, w_ref[:, 2 * gw:3 * gw]) * ATTN_SCALE
    k = _dot(h, w_ref[:, 3 * gw:4 * gw])
    v = _dot(h, w_ref[:, 4 * gw:5 * gw])
    k_ref[...] = k
    v_ref[...] = v
    kt_ref[0, 0] = k.T
    vt_ref[0, 0] = v.T

    cbuf[C_PAD:C_PAD + tm, :] = _dot(h, w_ref[:, 7 * gw:8 * gw]) * _dot(h, w_ref[:, 5 * gw:6 * gw])
    yc_ref[...] = (_dot(h, w_ref[:, 6 * gw:7 * gw]) * _causal_conv(cbuf, ccw_ref, CONV_C_WIDTH, C_PAD, tm)).astype(BF16)

    u = _dot(h, w_ref[:, 8 * gw:9 * gw])
    pbuf[P_PAD:P_PAD + tm, :] = u
    s1 = pbuf[...]
    s2 = s1 + pltpu.roll(s1, 1, 0)
    s4 = s2 + pltpu.roll(s2, 2, 0)
    s8 = s4 + pltpu.roll(s4, 4, 0)
    s16 = s8 + pltpu.roll(s8, 8, 0)
    win = _pool_select(s2, s4, s8, s16)[P_PAD:P_PAD + tm]
    pos = j * tm + lax.broadcasted_iota(jnp.int32, (tm, gw), 0)
    cnt = jnp.minimum(_pool_window_lanes((tm, gw)), pos + 1).astype(F32)
    yd_ref[...] = (_dot((win / cnt - u).astype(BF16), pw_ref[...]) * ps_ref[...]).astype(BF16)

    @pl.when(j == pl.num_programs(1) - 1)
    def _():
        sta_ref[0] = abuf[A_PAD + tm - (CONV_A_WIDTH - 1):A_PAD + tm, :]
        stc_ref[0] = cbuf[C_PAD + tm - (CONV_C_WIDTH - 1):C_PAD + tm, :]
        stp_ref[0] = pbuf[P_PAD + tm - POOL_STATE:P_PAD + tm, :]

    abuf[0:A_PAD, :] = abuf[tm:tm + A_PAD, :]
    cbuf[0:C_PAD, :] = cbuf[tm:tm + C_PAD, :]
    pbuf[0:P_PAD, :] = pbuf[tm:tm + P_PAD, :]


def _inmix_prompt(x, g, w_in, caw, cab, lng, lnb, ccw, pool_bd, ps, later_weights, layer, n, tm, kt_all, vt_all):
    t, d = x.shape
    gw = GROUP_W
    depth = w_in.shape[0]
    spb = t // n // tm
    row = lambda w: pl.BlockSpec((tm, w), lambda i, j: (i * spb + j, 0))
    full = lambda a: pl.BlockSpec(a.shape, lambda i, j: (0,) * a.ndim)
    st = lambda r: pl.BlockSpec((1, r, gw), lambda i, j: (i, 0, 0))
    st_rows = (CONV_A_WIDTH - 1, CONV_C_WIDTH - 1, POOL_STATE)
    cm = pl.BlockSpec((1, 1, gw, tm), lambda i, j: (layer, i, 0, j))
    small = (caw, cab, lng, lnb, ccw, pool_bd, ps)
    slab = lambda a, l: pl.BlockSpec((1, a.shape[1] // n, a.shape[2]), lambda i, j: (l, i, 0))
    in_specs = ([row(d), full(g), _layer_spec(w_in, layer)] + [full(a) for a in small]
                + [slab(a, layer) for a in later_weights])
    args, aliases = [x, g, w_in, *small, *later_weights], {}
    if kt_all is not None:
        in_specs += [pl.BlockSpec(memory_space=pl.ANY)] * 2
        aliases = {len(args): 9, len(args) + 1: 10}
        args += [kt_all, vt_all]
    return pl.pallas_call(
        functools.partial(_inmix_kernel, n_alias=len(aliases), tm=tm),
        grid=(n, spb),
        in_specs=in_specs,
        out_specs=[row(gw)] * 6 + [st(r) for r in st_rows] + [cm, cm] + [slab(a, 0) for a in later_weights],
        out_shape=[jax.ShapeDtypeStruct((t, gw), F32)] * 3 + [jax.ShapeDtypeStruct((t, gw), BF16)] * 3
        + [jax.ShapeDtypeStruct((n, r, gw), F32) for r in st_rows]
        + [jax.ShapeDtypeStruct((depth, n, gw, t // n), F32)] * 2
        + [jax.ShapeDtypeStruct((1,) + a.shape[1:], BF16) for a in later_weights],
        input_output_aliases=aliases,
        scratch_shapes=[pltpu.VMEM((A_PAD + tm, gw), F32), pltpu.VMEM((C_PAD + tm, gw), F32),
                        pltpu.VMEM((P_PAD + tm, gw), F32)],
        compiler_params=_cparams("parallel", "arbitrary"),
        name="inmix_prompt",
    )(*args)


def _outffn_kernel(x_ref, ya_ref, ob_ref, yc_ref, yd_ref, wo_ref, g2_ref, wgu_ref, wd_ref, gf_ref,
                   qs_ref, kn_ref, vn_ref, kb_ref, vb_ref, *refs, n_alias, ff_chunk, final):
    o_ref, obs_ref, ko_ref, vo_ref = refs[n_alias:]
    seq_row = pl.ds(pl.program_id(0), 1)
    obs_ref[seq_row, :] = _cache_job(qs_ref[seq_row, :], kn_ref[seq_row, :], vn_ref[seq_row, :], kb_ref, vb_ref,
                                     ko_ref, vo_ref)
    x1 = x_ref[...] + _out_proj(ya_ref[...], ob_ref[...], yc_ref[...], yd_ref[...], wo_ref.at[0])
    x2 = _ffn(x1, g2_ref[...], wgu_ref.at[0], wd_ref.at[0], ff_chunk)
    o_ref[...] = _rmsnorm(x2, gf_ref[...]) if final else x2


def _outffn_prompt(x, ya, ob, yc, yd, w_out, g2, w_gu, w_down, gf, qs, kn, vn, kbuf, vbuf, k_all, v_all, layer, tm, final):
    t, d = x.shape
    depth, ns, w, buf = kbuf.shape
    assert t // tm == ns
    row = lambda a: pl.BlockSpec((tm, a.shape[1]), lambda i: (i, 0))
    full = lambda a: pl.BlockSpec(a.shape, lambda i: (0,) * a.ndim)
    tok = full(qs)
    blk = pl.BlockSpec((1, 1, w, buf), lambda i: (layer, i, 0, 0))
    in_specs = ([row(a) for a in (x, ya, ob, yc, yd)]
                + [_layer_spec(w_out, layer), full(g2), _layer_spec(w_gu, layer), _layer_spec(w_down, layer), full(gf)]
                + [tok, tok, tok, blk, blk])
    args, aliases = [x, ya, ob, yc, yd, w_out, g2, w_gu, w_down, gf, qs, kn, vn, kbuf, vbuf], {}
    if k_all is not None:
        in_specs += [pl.BlockSpec(memory_space=pl.ANY)] * 2
        aliases = {len(args): 2, len(args) + 1: 3}
        args += [k_all, v_all]
    return pl.pallas_call(
        functools.partial(_outffn_kernel, n_alias=len(aliases), ff_chunk=256, final=final),
        grid=(t // tm,),
        in_specs=in_specs,
        out_specs=[row(x), tok, blk, blk],
        out_shape=[jax.ShapeDtypeStruct((t, d), F32), jax.ShapeDtypeStruct((ns, w), F32)]
        + [jax.ShapeDtypeStruct(kbuf.shape, F32)] * 2,
        input_output_aliases=aliases,
        compiler_params=_cparams("arbitrary"),
        name="outffn_prompt",
    )(*args)


def _cache_job(q_row, kn_row, vn_row, kb_ref, vb_ref, ko_ref, vo_ref):
    gw = GROUP_W
    buf = kb_ref.shape[-1]
    assert max(win for win, _ in DIL_CONFIGS) <= buf
    ident = lax.broadcasted_iota(jnp.int32, (gw, gw), 0) == lax.broadcasted_iota(jnp.int32, (gw, gw), 1)
    to_col = lambda row: jnp.sum(jnp.where(ident, row, 0.0), axis=1, keepdims=True)
    to_row = lambda col: jnp.sum(jnp.where(ident, col, 0.0), axis=0, keepdims=True)
    qc, kc, vc = to_col(q_row), to_col(kn_row), to_col(vn_row)
    kb, vb = kb_ref[0, 0], vb_ref[0, 0]
    last = lax.broadcasted_iota(jnp.int32, (gw, buf), 1) == buf - 1
    ko_ref[0, 0] = jnp.where(last, kc, pltpu.roll(kb, buf - 1, 1))
    vo_ref[0, 0] = jnp.where(last, vc, pltpu.roll(vb, buf - 1, 1))

    def heads(a):
        return jnp.sum(a.reshape(N_HEADS, HEAD_DIM, a.shape[-1]), axis=1)

    def spread(a):
        return jnp.broadcast_to(a[:, None, :], (N_HEADS, HEAD_DIM, a.shape[-1])).reshape(gw, a.shape[-1])

    s_all = heads(kb * qc)
    s_new = heads(kc * qc)
    outs, lses = [], []
    for win, dil in DIL_CONFIGS:
        lo = buf - win
        s = s_all[:, lo:]
        if dil > 1:
            back = win - lax.broadcasted_iota(jnp.int32, s.shape, 1)
            s = jnp.where(back % dil == 0, s, NEG)
        m = jnp.maximum(jnp.max(s, axis=1, keepdims=True), s_new)
        p, p_new = jnp.exp(s - m), jnp.exp(s_new - m)
        den = jnp.sum(p, axis=1, keepdims=True) + p_new
        pv = jnp.sum(vb[:, lo:] * spread(p), axis=1, keepdims=True) + vc * spread(p_new)
        outs.append(pv / spread(den))
        lses.append(spread(m + jnp.log(den)))
    m = jnp.maximum(jnp.maximum(lses[0], lses[1]), lses[2])
    es = [jnp.exp(l - m) for l in lses]
    return to_row((es[0] * outs[0] + es[1] * outs[1] + es[2] * outs[2]) / (es[0] + es[1] + es[2]))


def _sample_rest_kernel(ag_ref, cd_ref, ob_ref, x_ref, sa_ref, sc_ref, sp_ref, caw_ref, cab_ref, lng_ref, lnb_ref,
                        ccw_ref, pw_ref, ps_ref, wo_ref, g2_ref, wgu_ref, wd_ref, gf_ref,
                        y_ref, na_ref, nc_ref, np_ref, *, pos0, final):
    gw = GROUP_W
    ga = ag_ref[:, 0:gw] * jax.nn.sigmoid(ag_ref[:, gw:2 * gw])
    na = CONV_A_WIDTH - 1
    acc = caw_ref[na:na + 1, :] * ga
    for t in range(na):
        acc += caw_ref[t:t + 1, :] * sa_ref[0, t]
    ya = jax.nn.silu(_layernorm(acc + cab_ref[...], lng_ref[...], lnb_ref[...]))
    na_ref[0:na - 1] = sa_ref[0, 1:na]
    na_ref[na - 1] = ga

    cx = cd_ref[:, 2 * gw:3 * gw] * cd_ref[:, 0:gw]
    nc = CONV_C_WIDTH - 1
    acc = ccw_ref[nc:nc + 1, :] * cx
    for t in range(nc):
        acc += ccw_ref[t:t + 1, :] * sc_ref[0, t]
    yc = cd_ref[:, gw:2 * gw] * acc
    nc_ref[0:nc - 1] = sc_ref[0, 1:nc]
    nc_ref[nc - 1] = cx

    u = cd_ref[:, 3 * gw:4 * gw]
    npl = POOL_STATE
    back = lambda i: sp_ref[0, npl - i]
    s2 = u + back(1)
    s4 = s2 + back(2) + back(3)
    s8 = s4 + back(4) + back(5) + back(6) + back(7)
    s16 = s8
    for i in range(8, 16):
        s16 = s16 + back(i)
    cnt = jnp.minimum(_pool_window_lanes(u.shape), pos0 + 1).astype(F32)
    yd = _dot((_pool_select(s2, s4, s8, s16) / cnt - u).astype(BF16), pw_ref[...]) * ps_ref[...]
    np_ref[0:npl - 1] = sp_ref[0, 1:npl]
    np_ref[npl - 1] = u

    x1 = x_ref[...] + _out_proj(ya, ob_ref[...], yc, yd, wo_ref.at[0])
    x2 = _ffn(x1, g2_ref[...], wgu_ref.at[0], wd_ref.at[0], 256)
    y_ref[...] = _rmsnorm(x2, gf_ref[...]) if final else x2


def _sample_rest(ag, cd, ob, x, sa, sc, sp, caw, cab, lng, lnb, ccw, pool_bd, ps, w_out, g2, w_gu, w_down, gf, layer, pos0, final):
    full = lambda a: pl.BlockSpec(a.shape, lambda i: (0,) * a.ndim)
    st_in = lambda a: pl.BlockSpec((1,) + a.shape[1:], lambda i: (layer, 0, 0, 0))
    st_out = lambda a: pl.BlockSpec(a.shape[1:], lambda i: (0, 0, 0))
    dense = (caw, cab, lng, lnb, ccw, pool_bd, ps, w_out, g2, w_gu, w_down, gf)
    stacked = (w_out, w_gu, w_down)
    dense_spec = lambda a: _layer_spec(a, layer) if any(a is w for w in stacked) else full(a)
    return pl.pallas_call(
        functools.partial(_sample_rest_kernel, pos0=pos0, final=final),
        grid=(1,),
        in_specs=[full(a) for a in (ag, cd, ob, x)] + [st_in(a) for a in (sa, sc, sp)] + [dense_spec(a) for a in dense],
        out_specs=[full(x)] + [st_out(a) for a in (sa, sc, sp)],
        out_shape=[jax.ShapeDtypeStruct(x.shape, F32)] + [jax.ShapeDtypeStruct(a.shape[1:], F32) for a in (sa, sc, sp)],
        compiler_params=_cparams("arbitrary"),
        name="sample_rest",
    )(ag, cd, ob, x, sa, sc, sp, *dense)


def _block_diag(pool_w):
    g, c, e = pool_w.shape
    eye = jnp.eye(g, dtype=pool_w.dtype)
    return (pool_w[:, :, None, :] * eye[:, None, :, None]).reshape(g * c, g * e)


def kernel(x_prompt, x_sample, cache_win_k, cache_win_v, state_conv_a, state_conv_c, state_pool, w_in, conv_a_w, conv_a_b, ln_a_g, ln_a_b, conv_c_w, pool_w, pool_scale, w_out, norm1_g, norm2_g, w_gu, w_down, final_g):
    depth = w_in.shape[0]
    n, s, d = x_prompt.shape
    ns, ts, _ = x_sample.shape
    assert ts == 1
    gw = GROUP_W
    tm = 512

    row = lambda a: a.reshape(1, -1)
    to_cm = lambda a: jnp.transpose(a, (0, 1, 3, 4, 2)).reshape(a.shape[0], a.shape[1], gw, a.shape[2])
    from_cm = lambda a: jnp.transpose(a.reshape(a.shape[0], a.shape[1], N_HEADS, HEAD_DIM, a.shape[3]), (0, 1, 4, 2, 3))
    swap = lambda a: jnp.transpose(a, (0, 2, 1, 3))
    kbuf, vbuf = to_cm(cache_win_k), to_cm(cache_win_v)
    sa_all, sc_all, sp_all = swap(state_conv_a), swap(state_conv_c), swap(state_pool)

    xp = x_prompt.reshape(n * s, d)
    xs = x_sample.reshape(ns, d)
    gf = row(final_g)
    kt_p = vt_p = kt_s = vt_s = None
    st_p = [[] for _ in range(3)]
    st_s = [[] for _ in range(3)]
    w_in = w_in.astype(BF16)
    for l in range(depth):
        pool_bd = _block_diag(pool_w[l]).astype(BF16)
        small = (conv_a_w[l], row(conv_a_b[l]), row(ln_a_g[l]), row(ln_a_b[l]), conv_c_w[l], pool_bd, row(pool_scale[l]))
        final = l == depth - 1

        ag, qs, ks, vs, cd = _inproj(xs, row(norm1_g[l]), w_in, l)
        q, k, v, ya, yc, yd, *states, kt_p, vt_p, w_out_l, w_gu_l, w_down_l = _inmix_prompt(
            xp, row(norm1_g[l]), w_in, *small, (w_out, w_gu, w_down), l, n, tm, kt_p, vt_p)
        ob = _attn_prompt(q.reshape(n, s, gw), k.reshape(n, s, gw), v.reshape(n, s, gw)).reshape(n * s, gw)
        xp, obs, kt_s, vt_s = _outffn_prompt(
            xp, ya, ob, yc, yd, w_out_l, row(norm2_g[l]), w_gu_l, w_down_l, gf,
            qs, ks, vs, kbuf, vbuf, kt_s, vt_s, l, tm, final)
        for lst, a in zip(st_p, states):
            lst.append(a)
        xs, *states = _sample_rest(ag, cd, obs, xs, sa_all, sc_all, sp_all, *small, w_out_l,
                                   row(norm2_g[l]), w_gu_l, w_down_l, gf, l, PAST_LEN, final)
        for lst, a in zip(st_s, states):
            lst.append(a)

    y_prompt = xp.reshape(n, s, d)
    y_sample = xs.reshape(ns, ts, d)
    return (y_prompt, y_sample, from_cm(kt_p), from_cm(vt_p), *[jnp.stack(a, axis=0) for a in st_p],
            from_cm(kt_s), from_cm(vt_s), *[swap(jnp.stack(a, axis=0)) for a in st_s])
```

```python
import functools
import math

import jax
import jax.numpy as jnp
from jax import lax
from jax.experimental import pallas as pl
from jax.experimental.pallas import tpu as pltpu

F32 = jnp.float32
BF16 = jnp.bfloat16

GROUP_W = 256
HEAD_DIM = 64
N_HEADS = GROUP_W // HEAD_DIM
CONV_A_WIDTH = 31
CONV_C_WIDTH = 3
POOL_WINDOWS = (2, 4, 8, 16)
POOL_STATE = max(POOL_WINDOWS) - 1
DIL_CONFIGS = ((128, 1), (512, 4), (2048, 16))
Q_BLOCK = 128
PAST_LEN = 16384
ATTN_SCALE = 1.0 / math.sqrt(HEAD_DIM)
EPS = 1e-6
NEG = -1e30

VMEM_LIMIT_BYTES = 56 * 1024 * 1024


def _cparams(*sem):
    return pltpu.CompilerParams(dimension_semantics=sem, vmem_limit_bytes=VMEM_LIMIT_BYTES)


def _rmsnorm(x, g):
    return x * lax.rsqrt(jnp.mean(x * x, axis=-1, keepdims=True) + EPS) * g


def _layernorm(x, g, b):
    mu = jnp.mean(x, axis=-1, keepdims=True)
    xc = x - mu
    return xc * lax.rsqrt(jnp.mean(xc * xc, axis=-1, keepdims=True) + EPS) * g + b


def _layer_spec(a, layer):
    layer = min(layer, a.shape[0] - 1)
    return pl.BlockSpec((1,) + a.shape[1:], lambda *_: (layer,) + (0,) * (a.ndim - 1))


def _dot(a, b):
    return jnp.dot(a, b, preferred_element_type=F32)


def _head_of_lane(shape, dim):
    return lax.broadcasted_iota(jnp.int32, shape, dim) // HEAD_DIM


def _inproj_kernel(x_ref, g_ref, w_ref, ag_ref, q_ref, k_ref, v_ref, cd_ref):
    w_ref = w_ref.at[0]
    h = _rmsnorm(x_ref[...], g_ref[...]).astype(BF16)
    gw = GROUP_W
    ag_ref[...] = _dot(h, w_ref[:, 0:2 * gw])
    q_ref[...] = _dot(h, w_ref[:, 2 * gw:3 * gw]) * ATTN_SCALE
    k_ref[...] = _dot(h, w_ref[:, 3 * gw:4 * gw])
    v_ref[...] = _dot(h, w_ref[:, 4 * gw:5 * gw])
    cd_ref[...] = _dot(h, w_ref[:, 5 * gw:9 * gw])


def _inproj(x, g, w_in, layer):
    t, d = x.shape
    gw = GROUP_W
    full = lambda a: pl.BlockSpec(a.shape, lambda i: (0,) * a.ndim)
    widths = (2 * gw, gw, gw, gw, 4 * gw)
    return pl.pallas_call(
        _inproj_kernel,
        grid=(1,),
        in_specs=[full(x), full(g), _layer_spec(w_in, layer)],
        out_specs=[pl.BlockSpec((t, w), lambda i: (0, 0)) for w in widths],
        out_shape=[jax.ShapeDtypeStruct((t, w), F32) for w in widths],
        compiler_params=_cparams("arbitrary"),
        name="inproj",
    )(x, g, w_in)


HEAD_GROUPS = ((0, 1), (2, 3))
MID_DIL = 4
UNITS_PER_STEP = 16


def _attn_kernel(q_scr, k_scr, v_scr, ob_ref, q_mid, k_mid, v_mid, qc, kc, vtc, o_scr, l_scr, *, seq):
    qb = Q_BLOCK
    hw = GROUP_W // 2
    kc[0:qb, :] = jnp.zeros((qb, GROUP_W), BF16)
    vtc[:, 0:qb] = jnp.zeros((GROUP_W, qb), BF16)

    gh = len(HEAD_GROUPS[0])
    head_q = _head_of_lane((qb, GROUP_W), 1)
    kk = lax.broadcasted_iota(jnp.int32, (2 * qb, gh * qb), 0)
    qi = lax.broadcasted_iota(jnp.int32, (2 * qb, gh * qb), 1) % qb
    band = jnp.logical_or(jnp.logical_and(kk < qb, kk >= qi), jnp.logical_and(kk >= qb, kk - qb <= qi))

    for c, (_, dil) in enumerate(DIL_CONFIGS):
        n_blocks = seq // (dil * qb)
        n_steps = seq // qb // UNITS_PER_STEP
        rows = lambda s0, dil=dil: pl.ds(s0, qb, stride=dil) if dil > 1 else pl.ds(s0, qb)

        def block_start(u, dil=dil, n_blocks=n_blocks):
            start = u // n_blocks + (u % n_blocks) * (qb * dil)
            return pl.multiple_of(start, qb) if dil == 1 else start

        two_hop = dil > MID_DIL and dil % MID_DIL == 0
        assert not two_hop or MID_DIL in [d for _, d in DIL_CONFIGS[:c]]

        def stage(u, carry, dil=dil, n_blocks=n_blocks, rows=rows, block_start=block_start, two_hop=two_hop):
            if two_hop:
                r, b = u // n_blocks, u % n_blocks
                start = (r % MID_DIL) * (seq // MID_DIL) + r // MID_DIL + b * (qb * dil // MID_DIL)
                src_rows = pl.ds(start, qb, stride=dil // MID_DIL)
                srcs = (q_mid, k_mid, v_mid)
            else:
                src_rows = rows(block_start(u))
                srcs = (q_scr, k_scr, v_scr)
            q, k, v = (jnp.concatenate([scr[0, src_rows, :], scr[1, src_rows, :]], axis=1) for scr in srcs)
            dst = pl.multiple_of(u * qb, qb)
            qc[pl.ds(dst, qb), :] = q
            kc[pl.ds(dst + qb, qb), :] = k.astype(BF16)
            vtc[:, pl.ds(dst + qb, qb)] = v.T.astype(BF16)
            if dil == MID_DIL:
                for half in range(2):
                    for val, mid in ((q, q_mid), (k, k_mid), (v, v_mid)):
                        mid[half, pl.ds(dst, qb), :] = val[:, half * hw:(half + 1) * hw]
            return carry

        lax.fori_loop(0, seq // qb, stage, 0, unroll=4)

        def unit(u, carry, c=c, n_blocks=n_blocks, rows=rows, block_start=block_start):
            us = [u * UNITS_PER_STEP + i for i in range(UNITS_PER_STEP)]
            w0s = [pl.multiple_of(ui * qb, qb) for ui in us]
            chains = [(i, g) for i in range(UNITS_PER_STEP) for g in range(len(HEAD_GROUPS))]
            ch = gh * HEAD_DIM
            scores = []
            for i, g in chains:
                q = qc[pl.ds(w0s[i], qb), :]
                kwin = kc[pl.ds(w0s[i], 2 * qb), :]
                qm = jnp.concatenate([jnp.where(head_q == h, q, 0.0) for h in HEAD_GROUPS[g]], axis=0).astype(BF16)
                scores.append(lax.dot_general(kwin, qm, (((1,), (1,)), ((), ())), preferred_element_type=F32))
            probs, dens, lses = [], [], []
            for (i, g), s in zip(chains, scores):
                valid = jnp.logical_and(band, jnp.logical_or(kk >= qb, us[i] % n_blocks > 0))
                s = jnp.where(valid, s, NEG)
                m = jnp.max(s, axis=0, keepdims=True)
                p = jnp.exp(s - m)
                den = jnp.sum(p, axis=0, keepdims=True)
                probs.append(p.astype(BF16))
                dens.append(den)
                lses.append(m + jnp.log(den))
            nums = [_dot(vtc[g * ch:(g + 1) * ch, pl.ds(w0s[i], 2 * qb)], p) for (i, g), p in zip(chains, probs)]
            for i in range(UNITS_PER_STEP):
                o_t, l_t = [], []
                for g in range(len(HEAD_GROUPS)):
                    n = chains.index((i, g))
                    for j in range(gh):
                        cols = slice(j * qb, (j + 1) * qb)
                        o_t.append(nums[n][j * HEAD_DIM:(j + 1) * HEAD_DIM, cols] / dens[n][:, cols])
                        l_t.append(jnp.broadcast_to(lses[n][:, cols], (HEAD_DIM, qb)))
                o = jnp.concatenate(o_t, axis=0).T
                l = jnp.concatenate(l_t, axis=0).T
                start = block_start(us[i])
                for half in range(2):
                    o_scr[c, half, rows(start), :] = o[:, half * hw:(half + 1) * hw]
                    l_scr[c, half, rows(start), :] = l[:, half * hw:(half + 1) * hw]
            return carry

        lax.fori_loop(0, n_steps, unit, 0)

    for half in range(2):
        l0, l1, l2 = l_scr[0, half], l_scr[1, half], l_scr[2, half]
        m = jnp.maximum(jnp.maximum(l0, l1), l2)
        e0, e1, e2 = jnp.exp(l0 - m), jnp.exp(l1 - m), jnp.exp(l2 - m)
        ob_ref[0, :, half * hw:(half + 1) * hw] = (
            (e0 * o_scr[0, half] + e1 * o_scr[1, half] + e2 * o_scr[2, half]) / (e0 + e1 + e2)).astype(ob_ref.dtype)


def _attn_prompt(q, k, v, n):
    _, t, hw = q.shape
    s, w = t // n, 2 * hw
    halves = pl.BlockSpec((2, s, hw), lambda i: (0, i, 0))
    return pl.pallas_call(
        functools.partial(_attn_kernel, seq=s),
        grid=(n,),
        in_specs=[halves] * 3,
        out_specs=pl.BlockSpec((1, s, w), lambda i: (i, 0, 0)),
        out_shape=jax.ShapeDtypeStruct((n, s, w), BF16),
        scratch_shapes=[pltpu.VMEM((2, s, w // 2), F32)] * 3
        + [pltpu.VMEM((s, w), F32), pltpu.VMEM((Q_BLOCK + s, w), BF16), pltpu.VMEM((w, Q_BLOCK + s), BF16)]
        + [pltpu.VMEM((len(DIL_CONFIGS), 2, s, w // 2), F32)] * 2,
        compiler_params=_cparams("parallel"),
        name="attn_prompt",
    )(q, k, v)


def _pool_select(s2, s4, s8, s16):
    grp = _head_of_lane(s2.shape, s2.ndim - 1)
    return jnp.where(grp == 0, s2, jnp.where(grp == 1, s4, jnp.where(grp == 2, s8, s16)))


def _pool_window_lanes(shape):
    grp = _head_of_lane(shape, len(shape) - 1)
    return jnp.where(grp == 0, 2, jnp.where(grp == 1, 4, jnp.where(grp == 2, 8, 16)))


def _out_proj(ya, ob, yc, yd, w_out_ref):
    gw = GROUP_W
    acc = _dot(ya.astype(BF16), w_out_ref[0:gw, :])
    acc += _dot(ob.astype(BF16), w_out_ref[gw:2 * gw, :])
    acc += _dot(yc.astype(BF16), w_out_ref[2 * gw:3 * gw, :])
    acc += _dot(yd.astype(BF16), w_out_ref[3 * gw:4 * gw, :])
    return acc


def _ffn(x, g2, w_gu_ref, w_down_ref, ff_chunk):
    ff = w_down_ref.shape[0]
    h = _rmsnorm(x, g2).astype(BF16)
    acc = jnp.zeros_like(x)
    for c in range(ff // ff_chunk):
        lo = c * ff_chunk
        g = _dot(h, w_gu_ref[:, lo:lo + ff_chunk])
        u = _dot(h, w_gu_ref[:, ff + lo:ff + lo + ff_chunk])
        acc += _dot((jax.nn.silu(g) * u).astype(BF16), w_down_ref[lo:lo + ff_chunk, :])
    return x + acc


A_PAD = 32
C_PAD = 8
P_PAD = 16
SUBLANES = 8


def _causal_conv(buf, w_ref, width, pad, tm):
    base = pad - SUBLANES
    rows = tm + SUBLANES
    y = None
    for a in range(min(SUBLANES, width)):
        z = None
        for lag in range(a, width, SUBLANES):
            term = w_ref[width - 1 - lag:width - lag, :] * buf[base - (lag - a):base - (lag - a) + rows, :]
            z = term if z is None else z + term
        z = pltpu.roll(z, a, 0) if a else z
        y = z if y is None else y + z
    return y[SUBLANES:SUBLANES + tm]


def _inmix_kernel(x_ref, g_ref, w_ref, caw_ref, cab_ref, lng_ref, lnb_ref, ccw_ref, pw_ref, ps_ref,
                  wo_f32, wgu_f32, wd_f32, *refs, n_alias, tm):
    (q_ref, k_ref, v_ref, ya_ref, yc_ref, yd_ref, sta_ref, stc_ref, stp_ref, kt_ref, vt_ref, wo_b16, wgu_b16, wd_b16,
     abuf, cbuf, pbuf) = refs[n_alias:]
    gw = GROUP_W
    w_ref = w_ref.at[0]
    j = pl.program_id(1)

    @pl.when(j == 0)
    def _():
        abuf[0:A_PAD, :] = jnp.zeros((A_PAD, gw), F32)
        cbuf[0:C_PAD, :] = jnp.zeros((C_PAD, gw), F32)
        pbuf[0:P_PAD, :] = jnp.zeros((P_PAD, gw), F32)
        for src, dst in ((wo_f32, wo_b16), (wgu_f32, wgu_b16), (wd_f32, wd_b16)):
            dst[...] = src[...].astype(BF16)

    h = _rmsnorm(x_ref[...], g_ref[...]).astype(BF16)

    abuf[A_PAD:A_PAD + tm, :] = _dot(h, w_ref[:, 0:gw]) * jax.nn.sigmoid(_dot(h, w_ref[:, gw:2 * gw]))
    acc = _causal_conv(abuf, caw_ref, CONV_A_WIDTH, A_PAD, tm)
    ya_ref[...] = jax.nn.silu(_layernorm(acc + cab_ref[...], lng_ref[...], lnb_ref[...])).astype(BF16)

    q = _dot(h, w_ref[:, 2 * gw:3 * gw]) * ATTN_SCALE
    k = _dot(h, w_ref[:, 3 * gw:4 * gw])
    v = _dot(h, w_ref[:, 4 * gw:5 * gw])
    for half in range(2):
        lanes = slice(half * gw // 2, (half + 1) * gw // 2)
        q_ref[half], k_ref[half], v_ref[half] = q[:, lanes], k[:, lanes], v[:, lanes]
    kt_ref[0, 0] = k.T
    vt_ref[0, 0] = v.T

    cbuf[C_PAD:C_PAD + tm, :] = _dot(h, w_ref[:, 7 * gw:8 * gw]) * _dot(h, w_ref[:, 5 * gw:6 * gw])
    yc_ref[...] = (_dot(h, w_ref[:, 6 * gw:7 * gw]) * _causal_conv(cbuf, ccw_ref, CONV_C_WIDTH, C_PAD, tm)).astype(BF16)

    u = _dot(h, w_ref[:, 8 * gw:9 * gw])
    pbuf[P_PAD:P_PAD + tm, :] = u
    s1 = pbuf[...]
    s2 = s1 + pltpu.roll(s1, 1, 0)
    s4 = s2 + pltpu.roll(s2, 2, 0)
    s8 = s4 + pltpu.roll(s4, 4, 0)
    s16 = s8 + pltpu.roll(s8, 8, 0)
    win = _pool_select(s2, s4, s8, s16)[P_PAD:P_PAD + tm]
    pos = j * tm + lax.broadcasted_iota(jnp.int32, (tm, gw), 0)
    cnt = jnp.minimum(_pool_window_lanes((tm, gw)), pos + 1).astype(F32)
    yd_ref[...] = (_dot((win / cnt - u).astype(BF16), pw_ref[...]) * ps_ref[...]).astype(BF16)

    @pl.when(j == pl.num_programs(1) - 1)
    def _():
        sta_ref[0] = abuf[A_PAD + tm - (CONV_A_WIDTH - 1):A_PAD + tm, :]
        stc_ref[0] = cbuf[C_PAD + tm - (CONV_C_WIDTH - 1):C_PAD + tm, :]
        stp_ref[0] = pbuf[P_PAD + tm - POOL_STATE:P_PAD + tm, :]

    abuf[0:A_PAD, :] = abuf[tm:tm + A_PAD, :]
    cbuf[0:C_PAD, :] = cbuf[tm:tm + C_PAD, :]
    pbuf[0:P_PAD, :] = pbuf[tm:tm + P_PAD, :]


def _inmix_prompt(x, g, w_in, caw, cab, lng, lnb, ccw, pool_bd, ps, later_weights, layer, n, tm, kt_all, vt_all):
    t, d = x.shape
    gw = GROUP_W
    depth = w_in.shape[0]
    spb = t // n // tm
    row = lambda w: pl.BlockSpec((tm, w), lambda i, j: (i * spb + j, 0))
    full = lambda a: pl.BlockSpec(a.shape, lambda i, j: (0,) * a.ndim)
    st = lambda r: pl.BlockSpec((1, r, gw), lambda i, j: (i, 0, 0))
    st_rows = (CONV_A_WIDTH - 1, CONV_C_WIDTH - 1, POOL_STATE)
    cm = pl.BlockSpec((1, 1, gw, tm), lambda i, j: (layer, i, 0, j))
    small = (caw, cab, lng, lnb, ccw, pool_bd, ps)
    slab = lambda a, l: pl.BlockSpec((1, a.shape[1] // n, a.shape[2]), lambda i, j: (l, i, 0))
    in_specs = ([row(d), full(g), _layer_spec(w_in, layer)] + [full(a) for a in small]
                + [slab(a, layer) for a in later_weights])
    args, aliases = [x, g, w_in, *small, *later_weights], {}
    if kt_all is not None:
        in_specs += [pl.BlockSpec(memory_space=pl.ANY)] * 2
        aliases = {len(args): 9, len(args) + 1: 10}
        args += [kt_all, vt_all]
    return pl.pallas_call(
        functools.partial(_inmix_kernel, n_alias=len(aliases), tm=tm),
        grid=(n, spb),
        in_specs=in_specs,
        out_specs=[pl.BlockSpec((2, tm, gw // 2), lambda i, j: (0, i * spb + j, 0))] * 3 + [row(gw)] * 3
        + [st(r) for r in st_rows] + [cm, cm] + [slab(a, 0) for a in later_weights],
        out_shape=[jax.ShapeDtypeStruct((2, t, gw // 2), F32)] * 3 + [jax.ShapeDtypeStruct((t, gw), BF16)] * 3
        + [jax.ShapeDtypeStruct((n, r, gw), F32) for r in st_rows]
        + [jax.ShapeDtypeStruct((depth, n, gw, t // n), F32)] * 2
        + [jax.ShapeDtypeStruct((1,) + a.shape[1:], BF16) for a in later_weights],
        input_output_aliases=aliases,
        scratch_shapes=[pltpu.VMEM((A_PAD + tm, gw), F32), pltpu.VMEM((C_PAD + tm, gw), F32),
                        pltpu.VMEM((P_PAD + tm, gw), F32)],
        compiler_params=_cparams("parallel", "arbitrary"),
        name="inmix_prompt",
    )(*args)


def _outffn_kernel(x_ref, ya_ref, ob_ref, yc_ref, yd_ref, wo_ref, g2_ref, wgu_ref, wd_ref, gf_ref,
                   qs_ref, kn_ref, vn_ref, kb_ref, vb_ref, *refs, n_alias, ff_chunk, final):
    o_ref, obs_ref, ko_ref, vo_ref = refs[n_alias:]
    seq_row = pl.ds(pl.program_id(0), 1)
    obs_ref[seq_row, :] = _cache_job(qs_ref[seq_row, :], kn_ref[seq_row, :], vn_ref[seq_row, :], kb_ref, vb_ref,
                                     ko_ref, vo_ref)
    x1 = x_ref[...] + _out_proj(ya_ref[...], ob_ref[...], yc_ref[...], yd_ref[...], wo_ref.at[0])
    x2 = _ffn(x1, g2_ref[...], wgu_ref.at[0], wd_ref.at[0], ff_chunk)
    o_ref[...] = _rmsnorm(x2, gf_ref[...]) if final else x2


def _outffn_prompt(x, ya, ob, yc, yd, w_out, g2, w_gu, w_down, gf, qs, kn, vn, kbuf, vbuf, k_all, v_all, layer, tm, final):
    t, d = x.shape
    depth, ns, w, buf = kbuf.shape
    assert t // tm == ns
    row = lambda a: pl.BlockSpec((tm, a.shape[1]), lambda i: (i, 0))
    full = lambda a: pl.BlockSpec(a.shape, lambda i: (0,) * a.ndim)
    tok = full(qs)
    blk = pl.BlockSpec((1, 1, w, buf), lambda i: (layer, i, 0, 0))
    in_specs = ([row(a) for a in (x, ya, ob, yc, yd)]
                + [_layer_spec(w_out, layer), full(g2), _layer_spec(w_gu, layer), _layer_spec(w_down, layer), full(gf)]
                + [tok, tok, tok, blk, blk])
    args, aliases = [x, ya, ob, yc, yd, w_out, g2, w_gu, w_down, gf, qs, kn, vn, kbuf, vbuf], {}
    if k_all is not None:
        in_specs += [pl.BlockSpec(memory_space=pl.ANY)] * 2
        aliases = {len(args): 2, len(args) + 1: 3}
        args += [k_all, v_all]
    return pl.pallas_call(
        functools.partial(_outffn_kernel, n_alias=len(aliases), ff_chunk=256, final=final),
        grid=(t // tm,),
        in_specs=in_specs,
        out_specs=[row(x), tok, blk, blk],
        out_shape=[jax.ShapeDtypeStruct((t, d), F32), jax.ShapeDtypeStruct((ns, w), F32)]
        + [jax.ShapeDtypeStruct(kbuf.shape, F32)] * 2,
        input_output_aliases=aliases,
        compiler_params=_cparams("arbitrary"),
        name="outffn_prompt",
    )(*args)


def _cache_job(q_row, kn_row, vn_row, kb_ref, vb_ref, ko_ref, vo_ref):
    gw = GROUP_W
    buf = kb_ref.shape[-1]
    assert max(win for win, _ in DIL_CONFIGS) <= buf
    ident = lax.broadcasted_iota(jnp.int32, (gw, gw), 0) == lax.broadcasted_iota(jnp.int32, (gw, gw), 1)
    to_col = lambda row: jnp.sum(jnp.where(ident, row, 0.0), axis=1, keepdims=True)
    to_row = lambda col: jnp.sum(jnp.where(ident, col, 0.0), axis=0, keepdims=True)
    qc, kc, vc = to_col(q_row), to_col(kn_row), to_col(vn_row)
    kb, vb = kb_ref[0, 0], vb_ref[0, 0]
    last = lax.broadcasted_iota(jnp.int32, (gw, buf), 1) == buf - 1
    ko_ref[0, 0] = jnp.where(last, kc, pltpu.roll(kb, buf - 1, 1))
    vo_ref[0, 0] = jnp.where(last, vc, pltpu.roll(vb, buf - 1, 1))

    def heads(a):
        return jnp.sum(a.reshape(N_HEADS, HEAD_DIM, a.shape[-1]), axis=1)

    def spread(a):
        return jnp.broadcast_to(a[:, None, :], (N_HEADS, HEAD_DIM, a.shape[-1])).reshape(gw, a.shape[-1])

    s_all = heads(kb * qc)
    s_new = heads(kc * qc)
    outs, lses = [], []
    for win, dil in DIL_CONFIGS:
        lo = buf - win
        s = s_all[:, lo:]
        if dil > 1:
            back = win - lax.broadcasted_iota(jnp.int32, s.shape, 1)
            s = jnp.where(back % dil == 0, s, NEG)
        m = jnp.maximum(jnp.max(s, axis=1, keepdims=True), s_new)
        p, p_new = jnp.exp(s - m), jnp.exp(s_new - m)
        den = jnp.sum(p, axis=1, keepdims=True) + p_new
        pv = jnp.sum(vb[:, lo:] * spread(p), axis=1, keepdims=True) + vc * spread(p_new)
        outs.append(pv / spread(den))
        lses.append(spread(m + jnp.log(den)))
    m = jnp.maximum(jnp.maximum(lses[0], lses[1]), lses[2])
    es = [jnp.exp(l - m) for l in lses]
    return to_row((es[0] * outs[0] + es[1] * outs[1] + es[2] * outs[2]) / (es[0] + es[1] + es[2]))


def _sample_rest_kernel(ag_ref, cd_ref, ob_ref, x_ref, sa_ref, sc_ref, sp_ref, caw_ref, cab_ref, lng_ref, lnb_ref,
                        ccw_ref, pw_ref, ps_ref, wo_ref, g2_ref, wgu_ref, wd_ref, gf_ref,
                        y_ref, na_ref, nc_ref, np_ref, *, pos0, final):
    gw = GROUP_W
    ga = ag_ref[:, 0:gw] * jax.nn.sigmoid(ag_ref[:, gw:2 * gw])
    na = CONV_A_WIDTH - 1
    acc = caw_ref[na:na + 1, :] * ga
    for t in range(na):
        acc += caw_ref[t:t + 1, :] * sa_ref[0, t]
    ya = jax.nn.silu(_layernorm(acc + cab_ref[...], lng_ref[...], lnb_ref[...]))
    na_ref[0:na - 1] = sa_ref[0, 1:na]
    na_ref[na - 1] = ga

    cx = cd_ref[:, 2 * gw:3 * gw] * cd_ref[:, 0:gw]
    nc = CONV_C_WIDTH - 1
    acc = ccw_ref[nc:nc + 1, :] * cx
    for t in range(nc):
        acc += ccw_ref[t:t + 1, :] * sc_ref[0, t]
    yc = cd_ref[:, gw:2 * gw] * acc
    nc_ref[0:nc - 1] = sc_ref[0, 1:nc]
    nc_ref[nc - 1] = cx

    u = cd_ref[:, 3 * gw:4 * gw]
    npl = POOL_STATE
    back = lambda i: sp_ref[0, npl - i]
    s2 = u + back(1)
    s4 = s2 + back(2) + back(3)
    s8 = s4 + back(4) + back(5) + back(6) + back(7)
    s16 = s8
    for i in range(8, 16):
        s16 = s16 + back(i)
    cnt = jnp.minimum(_pool_window_lanes(u.shape), pos0 + 1).astype(F32)
    yd = _dot((_pool_select(s2, s4, s8, s16) / cnt - u).astype(BF16), pw_ref[...]) * ps_ref[...]
    np_ref[0:npl - 1] = sp_ref[0, 1:npl]
    np_ref[npl - 1] = u

    x1 = x_ref[...] + _out_proj(ya, ob_ref[...], yc, yd, wo_ref.at[0])
    x2 = _ffn(x1, g2_ref[...], wgu_ref.at[0], wd_ref.at[0], 256)
    y_ref[...] = _rmsnorm(x2, gf_ref[...]) if final else x2


def _sample_rest(ag, cd, ob, x, sa, sc, sp, caw, cab, lng, lnb, ccw, pool_bd, ps, w_out, g2, w_gu, w_down, gf, layer, pos0, final):
    full = lambda a: pl.BlockSpec(a.shape, lambda i: (0,) * a.ndim)
    st_in = lambda a: pl.BlockSpec((1,) + a.shape[1:], lambda i: (layer, 0, 0, 0))
    st_out = lambda a: pl.BlockSpec(a.shape[1:], lambda i: (0, 0, 0))
    dense = (caw, cab, lng, lnb, ccw, pool_bd, ps, w_out, g2, w_gu, w_down, gf)
    stacked = (w_out, w_gu, w_down)
    dense_spec = lambda a: _layer_spec(a, layer) if any(a is w for w in stacked) else full(a)
    return pl.pallas_call(
        functools.partial(_sample_rest_kernel, pos0=pos0, final=final),
        grid=(1,),
        in_specs=[full(a) for a in (ag, cd, ob, x)] + [st_in(a) for a in (sa, sc, sp)] + [dense_spec(a) for a in dense],
        out_specs=[full(x)] + [st_out(a) for a in (sa, sc, sp)],
        out_shape=[jax.ShapeDtypeStruct(x.shape, F32)] + [jax.ShapeDtypeStruct(a.shape[1:], F32) for a in (sa, sc, sp)],
        compiler_params=_cparams("arbitrary"),
        name="sample_rest",
    )(ag, cd, ob, x, sa, sc, sp, *dense)


def _block_diag(pool_w):
    g, c, e = pool_w.shape
    eye = jnp.eye(g, dtype=pool_w.dtype)
    return (pool_w[:, :, None, :] * eye[:, None, :, None]).reshape(g * c, g * e)


def kernel(x_prompt, x_sample, cache_win_k, cache_win_v, state_conv_a, state_conv_c, state_pool, w_in, conv_a_w, conv_a_b, ln_a_g, ln_a_b, conv_c_w, pool_w, pool_scale, w_out, norm1_g, norm2_g, w_gu, w_down, final_g):
    depth = w_in.shape[0]
    n, s, d = x_prompt.shape
    ns, ts, _ = x_sample.shape
    assert ts == 1
    gw = GROUP_W
    tm = 512

    row = lambda a: a.reshape(1, -1)
    to_cm = lambda a: jnp.transpose(a, (0, 1, 3, 4, 2)).reshape(a.shape[0], a.shape[1], gw, a.shape[2])
    from_cm = lambda a: jnp.transpose(a.reshape(a.shape[0], a.shape[1], N_HEADS, HEAD_DIM, a.shape[3]), (0, 1, 4, 2, 3))
    swap = lambda a: jnp.transpose(a, (0, 2, 1, 3))
    kbuf, vbuf = to_cm(cache_win_k), to_cm(cache_win_v)
    sa_all, sc_all, sp_all = swap(state_conv_a), swap(state_conv_c), swap(state_pool)

    xp = x_prompt.reshape(n * s, d)
    xs = x_sample.reshape(ns, d)
    gf = row(final_g)
    kt_p = vt_p = kt_s = vt_s = None
    st_p = [[] for _ in range(3)]
    st_s = [[] for _ in range(3)]
    w_in = w_in.astype(BF16)
    for l in range(depth):
        pool_bd = _block_diag(pool_w[l]).astype(BF16)
        small = (conv_a_w[l], row(conv_a_b[l]), row(ln_a_g[l]), row(ln_a_b[l]), conv_c_w[l], pool_bd, row(pool_scale[l]))
        final = l == depth - 1

        ag, qs, ks, vs, cd = _inproj(xs, row(norm1_g[l]), w_in, l)
        q, k, v, ya, yc, yd, *states, kt_p, vt_p, w_out_l, w_gu_l, w_down_l = _inmix_prompt(
            xp, row(norm1_g[l]), w_in, *small, (w_out, w_gu, w_down), l, n, tm, kt_p, vt_p)
        ob = _attn_prompt(q, k, v, n).reshape(n * s, gw)
        xp, obs, kt_s, vt_s = _outffn_prompt(
            xp, ya, ob, yc, yd, w_out_l, row(norm2_g[l]), w_gu_l, w_down_l, gf,
            qs, ks, vs, kbuf, vbuf, kt_s, vt_s, l, tm, final)
        for lst, a in zip(st_p, states):
            lst.append(a)
        xs, *states = _sample_rest(ag, cd, obs, xs, sa_all, sc_all, sp_all, *small, w_out_l,
                                   row(norm2_g[l]), w_gu_l, w_down_l, gf, l, PAST_LEN, final)
        for lst, a in zip(st_s, states):
            lst.append(a)

    y_prompt = xp.reshape(n, s, d)
    y_sample = xs.reshape(ns, ts, d)
    return (y_prompt, y_sample, from_cm(kt_p), from_cm(vt_p), *[jnp.stack(a, axis=0) for a in st_p],
            from_cm(kt_s), from_cm(vt_s), *[swap(jnp.stack(a, axis=0)) for a in st_s])
```

```python
import functools
import math

import jax
import jax.numpy as jnp
from jax import lax
from jax.experimental import pallas as pl
from jax.experimental.pallas import tpu as pltpu

F32 = jnp.float32
BF16 = jnp.bfloat16

GROUP_W = 256
HEAD_DIM = 64
N_HEADS = GROUP_W // HEAD_DIM
CONV_A_WIDTH = 31
CONV_C_WIDTH = 3
POOL_WINDOWS = (2, 4, 8, 16)
POOL_STATE = max(POOL_WINDOWS) - 1
DIL_CONFIGS = ((128, 1), (512, 4), (2048, 16))
Q_BLOCK = 128
PAST_LEN = 16384
ATTN_SCALE = 1.0 / math.sqrt(HEAD_DIM)
EPS = 1e-6
NEG = -1e30

VMEM_LIMIT_BYTES = 56 * 1024 * 1024


def _cparams(*sem):
    return pltpu.CompilerParams(dimension_semantics=sem, vmem_limit_bytes=VMEM_LIMIT_BYTES)


def _rmsnorm(x, g):
    return x * lax.rsqrt(jnp.mean(x * x, axis=-1, keepdims=True) + EPS) * g


def _layernorm(x, g, b):
    mu = jnp.mean(x, axis=-1, keepdims=True)
    xc = x - mu
    return xc * lax.rsqrt(jnp.mean(xc * xc, axis=-1, keepdims=True) + EPS) * g + b


def _layer_spec(a, layer):
    layer = min(layer, a.shape[0] - 1)
    return pl.BlockSpec((1,) + a.shape[1:], lambda *_: (layer,) + (0,) * (a.ndim - 1))


def _dot(a, b):
    return jnp.dot(a, b, preferred_element_type=F32)


def _head_of_lane(shape, dim):
    return lax.broadcasted_iota(jnp.int32, shape, dim) // HEAD_DIM


def _inproj_kernel(x_ref, g_ref, w_ref, ag_ref, q_ref, k_ref, v_ref, cd_ref):
    w_ref = w_ref.at[0]
    h = _rmsnorm(x_ref[...], g_ref[...]).astype(BF16)
    gw = GROUP_W
    ag_ref[...] = _dot(h, w_ref[:, 0:2 * gw])
    q_ref[...] = _dot(h, w_ref[:, 2 * gw:3 * gw]) * ATTN_SCALE
    k_ref[...] = _dot(h, w_ref[:, 3 * gw:4 * gw])
    v_ref[...] = _dot(h, w_ref[:, 4 * gw:5 * gw])
    cd_ref[...] = _dot(h, w_ref[:, 5 * gw:9 * gw])


def _inproj(x, g, w_in, layer):
    t, d = x.shape
    gw = GROUP_W
    full = lambda a: pl.BlockSpec(a.shape, lambda i: (0,) * a.ndim)
    widths = (2 * gw, gw, gw, gw, 4 * gw)
    return pl.pallas_call(
        _inproj_kernel,
        grid=(1,),
        in_specs=[full(x), full(g), _layer_spec(w_in, layer)],
        out_specs=[pl.BlockSpec((t, w), lambda i: (0, 0)) for w in widths],
        out_shape=[jax.ShapeDtypeStruct((t, w), F32) for w in widths],
        compiler_params=_cparams("arbitrary"),
        name="inproj",
    )(x, g, w_in)


HEAD_GROUPS = ((0, 1), (2, 3))
MID_DIL = 4
UNITS_PER_STEP = 16


def _attn_kernel(q_scr, k_scr, v_scr, ob_ref, q_mid, k_mid, v_mid, qc, kc, vtc, o_scr, l_scr, *, seq):
    qb = Q_BLOCK
    hw = GROUP_W // 2
    kc[0:qb, :] = jnp.zeros((qb, GROUP_W), BF16)
    vtc[:, 0:qb] = jnp.zeros((GROUP_W, qb), BF16)

    gh = len(HEAD_GROUPS[0])
    head_q = _head_of_lane((qb, GROUP_W), 1)
    kk = lax.broadcasted_iota(jnp.int32, (2 * qb, gh * qb), 0)
    qi = lax.broadcasted_iota(jnp.int32, (2 * qb, gh * qb), 1) % qb
    band = jnp.logical_or(jnp.logical_and(kk < qb, kk >= qi), jnp.logical_and(kk >= qb, kk - qb <= qi))

    for c, (_, dil) in enumerate(DIL_CONFIGS):
        n_blocks = seq // (dil * qb)
        n_steps = seq // qb // UNITS_PER_STEP
        rows = lambda s0, dil=dil: pl.ds(s0, qb, stride=dil) if dil > 1 else pl.ds(s0, qb)

        def block_start(u, dil=dil, n_blocks=n_blocks):
            start = u // n_blocks + (u % n_blocks) * (qb * dil)
            return pl.multiple_of(start, qb) if dil == 1 else start

        two_hop = dil > MID_DIL and dil % MID_DIL == 0
        assert not two_hop or MID_DIL in [d for _, d in DIL_CONFIGS[:c]]

        def stage(u, carry, dil=dil, n_blocks=n_blocks, rows=rows, block_start=block_start, two_hop=two_hop):
            if two_hop:
                r, b = u // n_blocks, u % n_blocks
                start = (r % MID_DIL) * (seq // MID_DIL) + r // MID_DIL + b * (qb * dil // MID_DIL)
                src_rows = pl.ds(start, qb, stride=dil // MID_DIL)
                srcs = (q_mid, k_mid, v_mid)
            else:
                src_rows = rows(block_start(u))
                srcs = (q_scr, k_scr, v_scr)
            q, k, v = (jnp.concatenate([scr[0, src_rows, :], scr[1, src_rows, :]], axis=1) for scr in srcs)
            dst = pl.multiple_of(u * qb, qb)
            qc[pl.ds(dst, qb), :] = q
            kc[pl.ds(dst + qb, qb), :] = k.astype(BF16)
            vtc[:, pl.ds(dst + qb, qb)] = v.T.astype(BF16)
            if dil == MID_DIL:
                for half in range(2):
                    for val, mid in ((q, q_mid), (k, k_mid), (v, v_mid)):
                        mid[half, pl.ds(dst, qb), :] = val[:, half * hw:(half + 1) * hw]
            return carry

        lax.fori_loop(0, seq // qb, stage, 0, unroll=4)

        def unit(u, carry, c=c, n_blocks=n_blocks, rows=rows, block_start=block_start):
            us = [u * UNITS_PER_STEP + i for i in range(UNITS_PER_STEP)]
            w0s = [pl.multiple_of(ui * qb, qb) for ui in us]
            chains = [(i, g) for i in range(UNITS_PER_STEP) for g in range(len(HEAD_GROUPS))]
            ch = gh * HEAD_DIM
            scores = []
            for i, g in chains:
                q = qc[pl.ds(w0s[i], qb), :]
                kwin = kc[pl.ds(w0s[i], 2 * qb), :]
                qm = jnp.concatenate([jnp.where(head_q == h, q, 0.0) for h in HEAD_GROUPS[g]], axis=0).astype(BF16)
                scores.append(lax.dot_general(kwin, qm, (((1,), (1,)), ((), ())), preferred_element_type=F32))
            probs, dens, lses = [], [], []
            for (i, g), s in zip(chains, scores):
                valid = jnp.logical_and(band, jnp.logical_or(kk >= qb, us[i] % n_blocks > 0))
                s = jnp.where(valid, s, NEG)
                m = jnp.max(s, axis=0, keepdims=True)
                p = jnp.exp(s - m)
                den = jnp.sum(p, axis=0, keepdims=True)
                probs.append(p.astype(BF16))
                dens.append(den)
                lses.append(m + jnp.log(den))
            nums = [_dot(vtc[g * ch:(g + 1) * ch, pl.ds(w0s[i], 2 * qb)], p) for (i, g), p in zip(chains, probs)]
            for i in range(UNITS_PER_STEP):
                o_t, l_t = [], []
                for g in range(len(HEAD_GROUPS)):
                    n = chains.index((i, g))
                    for j in range(gh):
                        cols = slice(j * qb, (j + 1) * qb)
                        o_t.append(nums[n][j * HEAD_DIM:(j + 1) * HEAD_DIM, cols] / dens[n][:, cols])
                        l_t.append(jnp.broadcast_to(lses[n][:, cols], (HEAD_DIM, qb)))
                o = jnp.concatenate(o_t, axis=0).T
                l = jnp.concatenate(l_t, axis=0).T
                start = block_start(us[i])
                for half in range(2):
                    o_scr[c, half, rows(start), :] = o[:, half * hw:(half + 1) * hw]
                    l_scr[c, half, rows(start), :] = l[:, half * hw:(half + 1) * hw]
            return carry

        lax.fori_loop(0, n_steps, unit, 0)

    for half in range(2):
        l0, l1, l2 = l_scr[0, half], l_scr[1, half], l_scr[2, half]
        m = jnp.maximum(jnp.maximum(l0, l1), l2)
        e0, e1, e2 = jnp.exp(l0 - m), jnp.exp(l1 - m), jnp.exp(l2 - m)
        ob_ref[0, :, half * hw:(half + 1) * hw] = (
            (e0 * o_scr[0, half] + e1 * o_scr[1, half] + e2 * o_scr[2, half]) / (e0 + e1 + e2)).astype(ob_ref.dtype)


def _attn_prompt(q, k, v, n):
    _, t, hw = q.shape
    s, w = t // n, 2 * hw
    halves = pl.BlockSpec((2, s, hw), lambda i: (0, i, 0))
    return pl.pallas_call(
        functools.partial(_attn_kernel, seq=s),
        grid=(n,),
        in_specs=[halves] * 3,
        out_specs=pl.BlockSpec((1, s, w), lambda i: (i, 0, 0)),
        out_shape=jax.ShapeDtypeStruct((n, s, w), BF16),
        scratch_shapes=[pltpu.VMEM((2, s, w // 2), F32)] * 3
        + [pltpu.VMEM((s, w), F32), pltpu.VMEM((Q_BLOCK + s, w), BF16), pltpu.VMEM((w, Q_BLOCK + s), BF16)]
        + [pltpu.VMEM((len(DIL_CONFIGS), 2, s, w // 2), F32)] * 2,
        compiler_params=_cparams("parallel"),
        name="attn_prompt",
    )(q, k, v)


def _pool_select(s2, s4, s8, s16):
    grp = _head_of_lane(s2.shape, s2.ndim - 1)
    return jnp.where(grp == 0, s2, jnp.where(grp == 1, s4, jnp.where(grp == 2, s8, s16)))


def _pool_window_lanes(shape):
    grp = _head_of_lane(shape, len(shape) - 1)
    return jnp.where(grp == 0, 2, jnp.where(grp == 1, 4, jnp.where(grp == 2, 8, 16)))


def _out_proj(ya, ob, yc, yd, w_out_ref):
    gw = GROUP_W
    acc = _dot(ya.astype(BF16), w_out_ref[0:gw, :])
    acc += _dot(ob.astype(BF16), w_out_ref[gw:2 * gw, :])
    acc += _dot(yc.astype(BF16), w_out_ref[2 * gw:3 * gw, :])
    acc += _dot(yd.astype(BF16), w_out_ref[3 * gw:4 * gw, :])
    return acc


def _ffn(x, g2, w_gu_ref, w_down_ref, ff_chunk):
    ff = w_down_ref.shape[0]
    h = _rmsnorm(x, g2).astype(BF16)
    acc = jnp.zeros_like(x)
    for c in range(ff // ff_chunk):
        lo = c * ff_chunk
        g = _dot(h, w_gu_ref[:, lo:lo + ff_chunk])
        u = _dot(h, w_gu_ref[:, ff + lo:ff + lo + ff_chunk])
        acc += _dot((jax.nn.silu(g) * u).astype(BF16), w_down_ref[lo:lo + ff_chunk, :])
    return x + acc


A_PAD = 32
C_PAD = 8
P_PAD = 16
SUBLANES = 8


def _causal_conv(buf, w_ref, width, pad, tm):
    base = pad - SUBLANES
    rows = tm + SUBLANES
    y = None
    for a in range(min(SUBLANES, width)):
        z = None
        for lag in range(a, width, SUBLANES):
            term = w_ref[width - 1 - lag:width - lag, :] * buf[base - (lag - a):base - (lag - a) + rows, :]
            z = term if z is None else z + term
        z = pltpu.roll(z, a, 0) if a else z
        y = z if y is None else y + z
    return y[SUBLANES:SUBLANES + tm]


def _inmix_kernel(x_ref, g_ref, w_ref, caw_ref, cab_ref, lng_ref, lnb_ref, ccw_ref, pw_ref, ps_ref,
                  wo_f32, wgu_f32, wd_f32, *refs, n_alias, tm):
    (q_ref, k_ref, v_ref, ya_ref, yc_ref, yd_ref, sta_ref, stc_ref, stp_ref, kt_ref, vt_ref, wo_b16, wgu_b16, wd_b16,
     abuf, cbuf, pbuf) = refs[n_alias:]
    gw = GROUP_W
    w_ref = w_ref.at[0]
    j = pl.program_id(1)

    @pl.when(j == 0)
    def _():
        abuf[0:A_PAD, :] = jnp.zeros((A_PAD, gw), F32)
        cbuf[0:C_PAD, :] = jnp.zeros((C_PAD, gw), F32)
        pbuf[0:P_PAD, :] = jnp.zeros((P_PAD, gw), F32)
        for src, dst in ((wo_f32, wo_b16), (wgu_f32, wgu_b16), (wd_f32, wd_b16)):
            dst[...] = src[...].astype(BF16)

    h = _rmsnorm(x_ref[...], g_ref[...]).astype(BF16)

    abuf[A_PAD:A_PAD + tm, :] = _dot(h, w_ref[:, 0:gw]) * jax.nn.sigmoid(_dot(h, w_ref[:, gw:2 * gw]))
    acc = _causal_conv(abuf, caw_ref, CONV_A_WIDTH, A_PAD, tm)
    ya_ref[...] = jax.nn.silu(_layernorm(acc + cab_ref[...], lng_ref[...], lnb_ref[...])).astype(BF16)

    q = _dot(h, w_ref[:, 2 * gw:3 * gw]) * ATTN_SCALE
    k = _dot(h, w_ref[:, 3 * gw:4 * gw])
    v = _dot(h, w_ref[:, 4 * gw:5 * gw])
    for half in range(2):
        lanes = slice(half * gw // 2, (half + 1) * gw // 2)
        q_ref[half], k_ref[half], v_ref[half] = q[:, lanes], k[:, lanes], v[:, lanes]
    kt_ref[0, 0] = k.T
    vt_ref[0, 0] = v.T

    cbuf[C_PAD:C_PAD + tm, :] = _dot(h, w_ref[:, 7 * gw:8 * gw]) * _dot(h, w_ref[:, 5 * gw:6 * gw])
    yc_ref[...] = (_dot(h, w_ref[:, 6 * gw:7 * gw]) * _causal_conv(cbuf, ccw_ref, CONV_C_WIDTH, C_PAD, tm)).astype(BF16)

    u = _dot(h, w_ref[:, 8 * gw:9 * gw])
    pbuf[P_PAD:P_PAD + tm, :] = u
    s1 = pbuf[...]
    s2 = s1 + pltpu.roll(s1, 1, 0)
    s4 = s2 + pltpu.roll(s2, 2, 0)
    s8 = s4 + pltpu.roll(s4, 4, 0)
    s16 = s8 + pltpu.roll(s8, 8, 0)
    win = _pool_select(s2, s4, s8, s16)[P_PAD:P_PAD + tm]
    pos = j * tm + lax.broadcasted_iota(jnp.int32, (tm, gw), 0)
    cnt = jnp.minimum(_pool_window_lanes((tm, gw)), pos + 1).astype(F32)
    yd_ref[...] = (_dot((win / cnt - u).astype(BF16), pw_ref[...]) * ps_ref[...]).astype(BF16)

    @pl.when(j == pl.num_programs(1) - 1)
    def _():
        sta_ref[0] = abuf[A_PAD + tm - (CONV_A_WIDTH - 1):A_PAD + tm, :]
        stc_ref[0] = cbuf[C_PAD + tm - (CONV_C_WIDTH - 1):C_PAD + tm, :]
        stp_ref[0] = pbuf[P_PAD + tm - POOL_STATE:P_PAD + tm, :]

    abuf[0:A_PAD, :] = abuf[tm:tm + A_PAD, :]
    cbuf[0:C_PAD, :] = cbuf[tm:tm + C_PAD, :]
    pbuf[0:P_PAD, :] = pbuf[tm:tm + P_PAD, :]


def _inmix_prompt(x, g, w_in, caw, cab, lng, lnb, ccw, pool_bd, ps, later_weights, layer, n, tm, kt_all, vt_all):
    t, d = x.shape
    gw = GROUP_W
    depth = w_in.shape[0]
    spb = t // n // tm
    row = lambda w: pl.BlockSpec((tm, w), lambda i, j: (i * spb + j, 0))
    full = lambda a: pl.BlockSpec(a.shape, lambda i, j: (0,) * a.ndim)
    st = lambda r: pl.BlockSpec((1, r, gw), lambda i, j: (i, 0, 0))
    st_rows = (CONV_A_WIDTH - 1, CONV_C_WIDTH - 1, POOL_STATE)
    cm = pl.BlockSpec((1, 1, gw, tm), lambda i, j: (layer, i, 0, j))
    small = (caw, cab, lng, lnb, ccw, pool_bd, ps)
    slab = lambda a, l: pl.BlockSpec((1, a.shape[1] // n, a.shape[2]), lambda i, j: (l, i, 0))
    in_specs = ([row(d), full(g), _layer_spec(w_in, layer)] + [full(a) for a in small]
                + [slab(a, layer) for a in later_weights])
    args, aliases = [x, g, w_in, *small, *later_weights], {}
    if kt_all is not None:
        in_specs += [pl.BlockSpec(memory_space=pl.ANY)] * 2
        aliases = {len(args): 9, len(args) + 1: 10}
        args += [kt_all, vt_all]
    return pl.pallas_call(
        functools.partial(_inmix_kernel, n_alias=len(aliases), tm=tm),
        grid=(n, spb),
        in_specs=in_specs,
        out_specs=[pl.BlockSpec((2, tm, gw // 2), lambda i, j: (0, i * spb + j, 0))] * 3 + [row(gw)] * 3
        + [st(r) for r in st_rows] + [cm, cm] + [slab(a, 0) for a in later_weights],
        out_shape=[jax.ShapeDtypeStruct((2, t, gw // 2), F32)] * 3 + [jax.ShapeDtypeStruct((t, gw), BF16)] * 3
        + [jax.ShapeDtypeStruct((n, r, gw), F32) for r in st_rows]
        + [jax.ShapeDtypeStruct((depth, n, gw, t // n), F32)] * 2
        + [jax.ShapeDtypeStruct((1,) + a.shape[1:], BF16) for a in later_weights],
        input_output_aliases=aliases,
        scratch_shapes=[pltpu.VMEM((A_PAD + tm, gw), F32), pltpu.VMEM((C_PAD + tm, gw), F32),
                        pltpu.VMEM((P_PAD + tm, gw), F32)],
        compiler_params=_cparams("parallel", "arbitrary"),
        name="inmix_prompt",
    )(*args)


def _outffn_kernel(x_ref, ya_ref, ob_ref, yc_ref, yd_ref, wo_ref, g2_ref, wgu_ref, wd_ref, gf_ref,
                   qs_ref, kn_ref, vn_ref, kb_ref, vb_ref, *refs, n_alias, ff_chunk, final):
    o_ref, obs_ref, ko_ref, vo_ref = refs[n_alias:]
    seq_row = pl.ds(pl.program_id(0), 1)
    obs_ref[seq_row, :] = _cache_job(qs_ref[seq_row, :], kn_ref[seq_row, :], vn_ref[seq_row, :], kb_ref, vb_ref,
                                     ko_ref, vo_ref)
    x1 = x_ref[...] + _out_proj(ya_ref[...], ob_ref[...], yc_ref[...], yd_ref[...], wo_ref.at[0])
    x2 = _ffn(x1, g2_ref[...], wgu_ref.at[0], wd_ref.at[0], ff_chunk)
    o_ref[...] = _rmsnorm(x2, gf_ref[...]) if final else x2


def _outffn_prompt(x, ya, ob, yc, yd, w_out, g2, w_gu, w_down, gf, qs, kn, vn, kbuf, vbuf, k_all, v_all, layer, tm, final):
    t, d = x.shape
    depth, ns, w, buf = kbuf.shape
    assert t // tm == ns
    row = lambda a: pl.BlockSpec((tm, a.shape[1]), lambda i: (i, 0))
    full = lambda a: pl.BlockSpec(a.shape, lambda i: (0,) * a.ndim)
    tok = full(qs)
    blk = pl.BlockSpec((1, 1, w, buf), lambda i: (layer, i, 0, 0))
    in_specs = ([row(a) for a in (x, ya, ob, yc, yd)]
                + [_layer_spec(w_out, layer), full(g2), _layer_spec(w_gu, layer), _layer_spec(w_down, layer), full(gf)]
                + [tok, tok, tok, blk, blk])
    args, aliases = [x, ya, ob, yc, yd, w_out, g2, w_gu, w_down, gf, qs, kn, vn, kbuf, vbuf], {}
    if k_all is not None:
        in_specs += [pl.BlockSpec(memory_space=pl.ANY)] * 2
        aliases = {len(args): 2, len(args) + 1: 3}
        args += [k_all, v_all]
    return pl.pallas_call(
        functools.partial(_outffn_kernel, n_alias=len(aliases), ff_chunk=256, final=final),
        grid=(t // tm,),
        in_specs=in_specs,
        out_specs=[row(x), tok, blk, blk],
        out_shape=[jax.ShapeDtypeStruct((t, d), F32), jax.ShapeDtypeStruct((ns, w), F32)]
        + [jax.ShapeDtypeStruct(kbuf.shape, F32)] * 2,
        input_output_aliases=aliases,
        compiler_params=_cparams("arbitrary"),
        name="outffn_prompt",
    )(*args)


def _cache_job(q_row, kn_row, vn_row, kb_ref, vb_ref, ko_ref, vo_ref):
    gw = GROUP_W
    buf = kb_ref.shape[-1]
    assert max(win for win, _ in DIL_CONFIGS) <= buf
    ident = lax.broadcasted_iota(jnp.int32, (gw, gw), 0) == lax.broadcasted_iota(jnp.int32, (gw, gw), 1)
    to_col = lambda row: jnp.sum(jnp.where(ident, row, 0.0), axis=1, keepdims=True)
    to_row = lambda col: jnp.sum(jnp.where(ident, col, 0.0), axis=0, keepdims=True)
    qc, kc, vc = to_col(q_row), to_col(kn_row), to_col(vn_row)
    kb, vb = kb_ref[0, 0], vb_ref[0, 0]
    last = lax.broadcasted_iota(jnp.int32, (gw, buf), 1) == buf - 1
    ko_ref[0, 0] = jnp.where(last, kc, pltpu.roll(kb, buf - 1, 1))
    vo_ref[0, 0] = jnp.where(last, vc, pltpu.roll(vb, buf - 1, 1))

    def heads(a):
        return jnp.sum(a.reshape(N_HEADS, HEAD_DIM, a.shape[-1]), axis=1)

    def spread(a):
        return jnp.broadcast_to(a[:, None, :], (N_HEADS, HEAD_DIM, a.shape[-1])).reshape(gw, a.shape[-1])

    s_all = heads(kb * qc)
    s_new = heads(kc * qc)
    outs, lses = [], []
    for win, dil in DIL_CONFIGS:
        lo = buf - win
        s = s_all[:, lo:]
        if dil > 1:
            back = win - lax.broadcasted_iota(jnp.int32, s.shape, 1)
            s = jnp.where(back % dil == 0, s, NEG)
        m = jnp.maximum(jnp.max(s, axis=1, keepdims=True), s_new)
        p, p_new = jnp.exp(s - m), jnp.exp(s_new - m)
        den = jnp.sum(p, axis=1, keepdims=True) + p_new
        pv = jnp.sum(vb[:, lo:] * spread(p), axis=1, keepdims=True) + vc * spread(p_new)
        outs.append(pv / spread(den))
        lses.append(spread(m + jnp.log(den)))
    m = jnp.maximum(jnp.maximum(lses[0], lses[1]), lses[2])
    es = [jnp.exp(l - m) for l in lses]
    return to_row((es[0] * outs[0] + es[1] * outs[1] + es[2] * outs[2]) / (es[0] + es[1] + es[2]))


def _sample_rest_kernel(ag_ref, cd_ref, ob_ref, x_ref, sa_ref, sc_ref, sp_ref, caw_ref, cab_ref, lng_ref, lnb_ref,
                        ccw_ref, pw_ref, ps_ref, wo_ref, g2_ref, wgu_ref, wd_ref, gf_ref,
                        y_ref, na_ref, nc_ref, np_ref, *, pos0, final):
    gw = GROUP_W
    ga = ag_ref[:, 0:gw] * jax.nn.sigmoid(ag_ref[:, gw:2 * gw])
    na = CONV_A_WIDTH - 1
    acc = caw_ref[na:na + 1, :] * ga
    for t in range(na):
        acc += caw_ref[t:t + 1, :] * sa_ref[0, t]
    ya = jax.nn.silu(_layernorm(acc + cab_ref[...], lng_ref[...], lnb_ref[...]))
    na_ref[0:na - 1] = sa_ref[0, 1:na]
    na_ref[na - 1] = ga

    cx = cd_ref[:, 2 * gw:3 * gw] * cd_ref[:, 0:gw]
    nc = CONV_C_WIDTH - 1
    acc = ccw_ref[nc:nc + 1, :] * cx
    for t in range(nc):
        acc += ccw_ref[t:t + 1, :] * sc_ref[0, t]
    yc = cd_ref[:, gw:2 * gw] * acc
    nc_ref[0:nc - 1] = sc_ref[0, 1:nc]
    nc_ref[nc - 1] = cx

    u = cd_ref[:, 3 * gw:4 * gw]
    npl = POOL_STATE
    back = lambda i: sp_ref[0, npl - i]
    s2 = u + back(1)
    s4 = s2 + back(2) + back(3)
    s8 = s4 + back(4) + back(5) + back(6) + back(7)
    s16 = s8
    for i in range(8, 16):
        s16 = s16 + back(i)
    cnt = jnp.minimum(_pool_window_lanes(u.shape), pos0 + 1).astype(F32)
    yd = _dot((_pool_select(s2, s4, s8, s16) / cnt - u).astype(BF16), pw_ref[...]) * ps_ref[...]
    np_ref[0:npl - 1] = sp_ref[0, 1:npl]
    np_ref[npl - 1] = u

    x1 = x_ref[...] + _out_proj(ya, ob_ref[...], yc, yd, wo_ref.at[0])
    x2 = _ffn(x1, g2_ref[...], wgu_ref.at[0], wd_ref.at[0], 256)
    y_ref[...] = _rmsnorm(x2, gf_ref[...]) if final else x2


def _sample_rest(ag, cd, ob, x, sa, sc, sp, caw, cab, lng, lnb, ccw, pool_bd, ps, w_out, g2, w_gu, w_down, gf, layer, pos0, final):
    full = lambda a: pl.BlockSpec(a.shape, lambda i: (0,) * a.ndim)
    st_in = lambda a: pl.BlockSpec((1,) + a.shape[1:], lambda i: (layer, 0, 0, 0))
    st_out = lambda a: pl.BlockSpec(a.shape[1:], lambda i: (0, 0, 0))
    dense = (caw, cab, lng, lnb, ccw, pool_bd, ps, w_out, g2, w_gu, w_down, gf)
    stacked = (w_out, w_gu, w_down)
    dense_spec = lambda a: _layer_spec(a, layer) if any(a is w for w in stacked) else full(a)
    return pl.pallas_call(
        functools.partial(_sample_rest_kernel, pos0=pos0, final=final),
        grid=(1,),
        in_specs=[full(a) for a in (ag, cd, ob, x)] + [st_in(a) for a in (sa, sc, sp)] + [dense_spec(a) for a in dense],
        out_specs=[full(x)] + [st_out(a) for a in (sa, sc, sp)],
        out_shape=[jax.ShapeDtypeStruct(x.shape, F32)] + [jax.ShapeDtypeStruct(a.shape[1:], F32) for a in (sa, sc, sp)],
        compiler_params=_cparams("arbitrary"),
        name="sample_rest",
    )(ag, cd, ob, x, sa, sc, sp, *dense)


def _block_diag(pool_w):
    g, c, e = pool_w.shape
    eye = jnp.eye(g, dtype=pool_w.dtype)
    return (pool_w[:, :, None, :] * eye[:, None, :, None]).reshape(g * c, g * e)


def kernel(x_prompt, x_sample, cache_win_k, cache_win_v, state_conv_a, state_conv_c, state_pool, w_in, conv_a_w, conv_a_b, ln_a_g, ln_a_b, conv_c_w, pool_w, pool_scale, w_out, norm1_g, norm2_g, w_gu, w_down, final_g):
    depth = w_in.shape[0]
    n, s, d = x_prompt.shape
    ns, ts, _ = x_sample.shape
    assert ts == 1
    gw = GROUP_W
    tm = 512
    tm_mix = 1024

    row = lambda a: a.reshape(1, -1)
    to_cm = lambda a: jnp.transpose(a, (0, 1, 3, 4, 2)).reshape(a.shape[0], a.shape[1], gw, a.shape[2])
    from_cm = lambda a: jnp.transpose(a.reshape(a.shape[0], a.shape[1], N_HEADS, HEAD_DIM, a.shape[3]), (0, 1, 4, 2, 3))
    swap = lambda a: jnp.transpose(a, (0, 2, 1, 3))
    kbuf, vbuf = to_cm(cache_win_k), to_cm(cache_win_v)
    sa_all, sc_all, sp_all = swap(state_conv_a), swap(state_conv_c), swap(state_pool)

    xp = x_prompt.reshape(n * s, d)
    xs = x_sample.reshape(ns, d)
    gf = row(final_g)
    kt_p = vt_p = kt_s = vt_s = None
    st_p = [[] for _ in range(3)]
    st_s = [[] for _ in range(3)]
    w_in = w_in.astype(BF16)
    for l in range(depth):
        pool_bd = _block_diag(pool_w[l]).astype(BF16)
        small = (conv_a_w[l], row(conv_a_b[l]), row(ln_a_g[l]), row(ln_a_b[l]), conv_c_w[l], pool_bd, row(pool_scale[l]))
        final = l == depth - 1

        ag, qs, ks, vs, cd = _inproj(xs, row(norm1_g[l]), w_in, l)
        q, k, v, ya, yc, yd, *states, kt_p, vt_p, w_out_l, w_gu_l, w_down_l = _inmix_prompt(
            xp, row(norm1_g[l]), w_in, *small, (w_out, w_gu, w_down), l, n, tm_mix, kt_p, vt_p)
        ob = _attn_prompt(q, k, v, n).reshape(n * s, gw)
        xp, obs, kt_s, vt_s = _outffn_prompt(
            xp, ya, ob, yc, yd, w_out_l, row(norm2_g[l]), w_gu_l, w_down_l, gf,
            qs, ks, vs, kbuf, vbuf, kt_s, vt_s, l, tm, final)
        for lst, a in zip(st_p, states):
            lst.append(a)
        xs, *states = _sample_rest(ag, cd, obs, xs, sa_all, sc_all, sp_all, *small, w_out_l,
                                   row(norm2_g[l]), w_gu_l, w_down_l, gf, l, PAST_LEN, final)
        for lst, a in zip(st_s, states):
            lst.append(a)

    y_prompt = xp.reshape(n, s, d)
    y_sample = xs.reshape(ns, ts, d)
    return (y_prompt, y_sample, from_cm(kt_p), from_cm(vt_p), *[jnp.stack(a, axis=0) for a in st_p],
            from_cm(kt_s), from_cm(vt_s), *[swap(jnp.stack(a, axis=0)) for a in st_s])
```

```python
import functools
import math

import jax
import jax.numpy as jnp
from jax import lax
from jax.experimental import pallas as pl
from jax.experimental.pallas import tpu as pltpu

F32 = jnp.float32
BF16 = jnp.bfloat16

GROUP_W = 256
HEAD_DIM = 64
N_HEADS = GROUP_W // HEAD_DIM
CONV_A_WIDTH = 31
CONV_C_WIDTH = 3
POOL_WINDOWS = (2, 4, 8, 16)
POOL_STATE = max(POOL_WINDOWS) - 1
DIL_CONFIGS = ((128, 1), (512, 4), (2048, 16))
Q_BLOCK = 128
PAST_LEN = 16384
ATTN_SCALE = 1.0 / math.sqrt(HEAD_DIM)
LOG2E = math.log2(math.e)
EPS = 1e-6
NEG = -1e30

VMEM_LIMIT_BYTES = 56 * 1024 * 1024


def _cparams(*sem):
    return pltpu.CompilerParams(dimension_semantics=sem, vmem_limit_bytes=VMEM_LIMIT_BYTES)


def _rmsnorm(x, g):
    return x * lax.rsqrt(jnp.mean(x * x, axis=-1, keepdims=True) + EPS) * g


def _layernorm(x, g, b):
    mu = jnp.mean(x, axis=-1, keepdims=True)
    xc = x - mu
    return xc * lax.rsqrt(jnp.mean(xc * xc, axis=-1, keepdims=True) + EPS) * g + b


def _layer_spec(a, layer):
    layer = min(layer, a.shape[0] - 1)
    return pl.BlockSpec((1,) + a.shape[1:], lambda *_: (layer,) + (0,) * (a.ndim - 1))


def _dot(a, b):
    return jnp.dot(a, b, preferred_element_type=F32)


def _head_of_lane(shape, dim):
    return lax.broadcasted_iota(jnp.int32, shape, dim) // HEAD_DIM


def _inproj_kernel(x_ref, g_ref, w_ref, ag_ref, q_ref, k_ref, v_ref, cd_ref):
    w_ref = w_ref.at[0]
    h = _rmsnorm(x_ref[...], g_ref[...]).astype(BF16)
    gw = GROUP_W
    ag_ref[...] = _dot(h, w_ref[:, 0:2 * gw])
    q_ref[...] = _dot(h, w_ref[:, 2 * gw:3 * gw]) * ATTN_SCALE
    k_ref[...] = _dot(h, w_ref[:, 3 * gw:4 * gw])
    v_ref[...] = _dot(h, w_ref[:, 4 * gw:5 * gw])
    cd_ref[...] = _dot(h, w_ref[:, 5 * gw:9 * gw])


def _inproj(x, g, w_in, layer):
    t, d = x.shape
    gw = GROUP_W
    full = lambda a: pl.BlockSpec(a.shape, lambda i: (0,) * a.ndim)
    widths = (2 * gw, gw, gw, gw, 4 * gw)
    return pl.pallas_call(
        _inproj_kernel,
        grid=(1,),
        in_specs=[full(x), full(g), _layer_spec(w_in, layer)],
        out_specs=[pl.BlockSpec((t, w), lambda i: (0, 0)) for w in widths],
        out_shape=[jax.ShapeDtypeStruct((t, w), F32) for w in widths],
        compiler_params=_cparams("arbitrary"),
        name="inproj",
    )(x, g, w_in)


HEAD_GROUPS = ((0, 1), (2, 3))
MID_DIL = 4
UNITS_PER_STEP = 16


def _attn_kernel(q_scr, k_scr, v_scr, ob_ref, q_mid, k_mid, v_mid, qc, kc, vtc, o_scr, l_scr, *, seq):
    qb = Q_BLOCK
    hw = GROUP_W // 2
    kc[0:qb, :] = jnp.zeros((qb, GROUP_W), BF16)
    vtc[:, 0:qb] = jnp.zeros((GROUP_W, qb), BF16)

    gh = len(HEAD_GROUPS[0])
    head_q = _head_of_lane((qb, GROUP_W), 1)
    kk = lax.broadcasted_iota(jnp.int32, (2 * qb, gh * qb), 0)
    qi = lax.broadcasted_iota(jnp.int32, (2 * qb, gh * qb), 1) % qb
    band = jnp.logical_or(jnp.logical_and(kk < qb, kk >= qi), jnp.logical_and(kk >= qb, kk - qb <= qi))

    for c, (_, dil) in enumerate(DIL_CONFIGS):
        n_blocks = seq // (dil * qb)
        n_steps = seq // qb // UNITS_PER_STEP
        rows = lambda s0, dil=dil: pl.ds(s0, qb, stride=dil) if dil > 1 else pl.ds(s0, qb)

        def block_start(u, dil=dil, n_blocks=n_blocks):
            start = u // n_blocks + (u % n_blocks) * (qb * dil)
            return pl.multiple_of(start, qb) if dil == 1 else start

        two_hop = dil > MID_DIL and dil % MID_DIL == 0
        assert not two_hop or MID_DIL in [d for _, d in DIL_CONFIGS[:c]]

        def stage(u, carry, dil=dil, n_blocks=n_blocks, rows=rows, block_start=block_start, two_hop=two_hop):
            if two_hop:
                r, b = u // n_blocks, u % n_blocks
                start = (r % MID_DIL) * (seq // MID_DIL) + r // MID_DIL + b * (qb * dil // MID_DIL)
                src_rows = pl.ds(start, qb, stride=dil // MID_DIL)
                srcs = (q_mid, k_mid, v_mid)
            else:
                src_rows = rows(block_start(u))
                srcs = (q_scr, k_scr, v_scr)
            q, k, v = (jnp.concatenate([scr[0, src_rows, :], scr[1, src_rows, :]], axis=1) for scr in srcs)
            dst = pl.multiple_of(u * qb, qb)
            qc[pl.ds(dst, qb), :] = q
            kc[pl.ds(dst + qb, qb), :] = k.astype(BF16)
            vtc[:, pl.ds(dst + qb, qb)] = v.T.astype(BF16)
            if dil == MID_DIL:
                for half in range(2):
                    for val, mid in ((q, q_mid), (k, k_mid), (v, v_mid)):
                        mid[half, pl.ds(dst, qb), :] = val[:, half * hw:(half + 1) * hw]
            return carry

        lax.fori_loop(0, seq // qb, stage, 0, unroll=4)

        def unit(u, carry, c=c, n_blocks=n_blocks, rows=rows, block_start=block_start):
            us = [u * UNITS_PER_STEP + i for i in range(UNITS_PER_STEP)]
            w0s = [pl.multiple_of(ui * qb, qb) for ui in us]
            chains = [(i, g) for i in range(UNITS_PER_STEP) for g in range(len(HEAD_GROUPS))]
            ch = gh * HEAD_DIM
            scores = []
            for i, g in chains:
                q = qc[pl.ds(w0s[i], qb), :]
                kwin = kc[pl.ds(w0s[i], 2 * qb), :]
                qm = jnp.concatenate([jnp.where(head_q == h, q, 0.0) for h in HEAD_GROUPS[g]], axis=0).astype(BF16)
                scores.append(lax.dot_general(kwin, qm, (((1,), (1,)), ((), ())), preferred_element_type=F32))
            probs, dens, lses = [], [], []
            for (i, g), s in zip(chains, scores):
                valid = jnp.logical_and(band, jnp.logical_or(kk >= qb, us[i] % n_blocks > 0))
                s = jnp.where(valid, s, NEG)
                m = jnp.max(s, axis=0, keepdims=True)
                p = jnp.exp2(s - m)
                den = jnp.sum(p, axis=0, keepdims=True)
                probs.append(p.astype(BF16))
                dens.append(den)
                lses.append((m + jnp.log2(den)) * (1.0 / LOG2E))
            nums = [_dot(vtc[g * ch:(g + 1) * ch, pl.ds(w0s[i], 2 * qb)], p) for (i, g), p in zip(chains, probs)]
            for i in range(UNITS_PER_STEP):
                o_t, l_t = [], []
                for g in range(len(HEAD_GROUPS)):
                    n = chains.index((i, g))
                    for j in range(gh):
                        cols = slice(j * qb, (j + 1) * qb)
                        o_t.append(nums[n][j * HEAD_DIM:(j + 1) * HEAD_DIM, cols] / dens[n][:, cols])
                        l_t.append(jnp.broadcast_to(lses[n][:, cols], (HEAD_DIM, qb)))
                o = jnp.concatenate(o_t, axis=0).T
                l = jnp.concatenate(l_t, axis=0).T
                start = block_start(us[i])
                for half in range(2):
                    o_scr[c, half, rows(start), :] = o[:, half * hw:(half + 1) * hw]
                    l_scr[c, half, rows(start), :] = l[:, half * hw:(half + 1) * hw]
            return carry

        lax.fori_loop(0, n_steps, unit, 0)

    for half in range(2):
        l0, l1, l2 = l_scr[0, half], l_scr[1, half], l_scr[2, half]
        m = jnp.maximum(jnp.maximum(l0, l1), l2)
        e0, e1, e2 = jnp.exp(l0 - m), jnp.exp(l1 - m), jnp.exp(l2 - m)
        ob_ref[0, :, half * hw:(half + 1) * hw] = (
            (e0 * o_scr[0, half] + e1 * o_scr[1, half] + e2 * o_scr[2, half]) / (e0 + e1 + e2)).astype(ob_ref.dtype)


def _attn_prompt(q, k, v, n):
    _, t, hw = q.shape
    s, w = t // n, 2 * hw
    halves = pl.BlockSpec((2, s, hw), lambda i: (0, i, 0))
    return pl.pallas_call(
        functools.partial(_attn_kernel, seq=s),
        grid=(n,),
        in_specs=[halves] * 3,
        out_specs=pl.BlockSpec((1, s, w), lambda i: (i, 0, 0)),
        out_shape=jax.ShapeDtypeStruct((n, s, w), BF16),
        scratch_shapes=[pltpu.VMEM((2, s, w // 2), F32)] * 3
        + [pltpu.VMEM((s, w), F32), pltpu.VMEM((Q_BLOCK + s, w), BF16), pltpu.VMEM((w, Q_BLOCK + s), BF16)]
        + [pltpu.VMEM((len(DIL_CONFIGS), 2, s, w // 2), F32)] * 2,
        compiler_params=_cparams("parallel"),
        name="attn_prompt",
    )(q, k, v)


def _pool_select(s2, s4, s8, s16):
    grp = _head_of_lane(s2.shape, s2.ndim - 1)
    return jnp.where(grp == 0, s2, jnp.where(grp == 1, s4, jnp.where(grp == 2, s8, s16)))


def _pool_window_lanes(shape):
    grp = _head_of_lane(shape, len(shape) - 1)
    return jnp.where(grp == 0, 2, jnp.where(grp == 1, 4, jnp.where(grp == 2, 8, 16)))


def _out_proj(ya, ob, yc, yd, w_out_ref):
    gw = GROUP_W
    acc = _dot(ya.astype(BF16), w_out_ref[0:gw, :])
    acc += _dot(ob.astype(BF16), w_out_ref[gw:2 * gw, :])
    acc += _dot(yc.astype(BF16), w_out_ref[2 * gw:3 * gw, :])
    acc += _dot(yd.astype(BF16), w_out_ref[3 * gw:4 * gw, :])
    return acc


def _ffn(x, g2, w_gu_ref, w_down_ref, ff_chunk):
    ff = w_down_ref.shape[0]
    h = _rmsnorm(x, g2).astype(BF16)
    acc = jnp.zeros_like(x)
    for c in range(ff // ff_chunk):
        lo = c * ff_chunk
        g = _dot(h, w_gu_ref[:, lo:lo + ff_chunk])
        u = _dot(h, w_gu_ref[:, ff + lo:ff + lo + ff_chunk])
        acc += _dot((jax.nn.silu(g) * u).astype(BF16), w_down_ref[lo:lo + ff_chunk, :])
    return x + acc


A_PAD = 32
C_PAD = 8
P_PAD = 16
SUBLANES = 8


def _causal_conv(buf, w_ref, width, pad, tm):
    base = pad - SUBLANES
    rows = tm + SUBLANES
    y = None
    for a in range(min(SUBLANES, width)):
        z = None
        for lag in range(a, width, SUBLANES):
            term = w_ref[width - 1 - lag:width - lag, :] * buf[base - (lag - a):base - (lag - a) + rows, :]
            z = term if z is None else z + term
        z = pltpu.roll(z, a, 0) if a else z
        y = z if y is None else y + z
    return y[SUBLANES:SUBLANES + tm]


def _inmix_kernel(x_ref, g_ref, w_ref, caw_ref, cab_ref, lng_ref, lnb_ref, ccw_ref, pw_ref, ps_ref,
                  wo_f32, wgu_f32, wd_f32, *refs, n_alias, tm):
    (q_ref, k_ref, v_ref, ya_ref, yc_ref, yd_ref, sta_ref, stc_ref, stp_ref, kt_ref, vt_ref, wo_b16, wgu_b16, wd_b16,
     abuf, cbuf, pbuf) = refs[n_alias:]
    gw = GROUP_W
    w_ref = w_ref.at[0]
    j = pl.program_id(1)

    @pl.when(j == 0)
    def _():
        abuf[0:A_PAD, :] = jnp.zeros((A_PAD, gw), F32)
        cbuf[0:C_PAD, :] = jnp.zeros((C_PAD, gw), F32)
        pbuf[0:P_PAD, :] = jnp.zeros((P_PAD, gw), F32)
        for src, dst in ((wo_f32, wo_b16), (wgu_f32, wgu_b16), (wd_f32, wd_b16)):
            dst[...] = src[...].astype(BF16)

    h = _rmsnorm(x_ref[...], g_ref[...]).astype(BF16)

    abuf[A_PAD:A_PAD + tm, :] = _dot(h, w_ref[:, 0:gw]) * jax.nn.sigmoid(_dot(h, w_ref[:, gw:2 * gw]))
    acc = _causal_conv(abuf, caw_ref, CONV_A_WIDTH, A_PAD, tm)
    ya_ref[...] = jax.nn.silu(_layernorm(acc + cab_ref[...], lng_ref[...], lnb_ref[...])).astype(BF16)

    q = _dot(h, w_ref[:, 2 * gw:3 * gw]) * (ATTN_SCALE * LOG2E)
    k = _dot(h, w_ref[:, 3 * gw:4 * gw])
    v = _dot(h, w_ref[:, 4 * gw:5 * gw])
    for half in range(2):
        lanes = slice(half * gw // 2, (half + 1) * gw // 2)
        q_ref[half], k_ref[half], v_ref[half] = q[:, lanes], k[:, lanes], v[:, lanes]
    kt_ref[0, 0] = k.T
    vt_ref[0, 0] = v.T

    cbuf[C_PAD:C_PAD + tm, :] = _dot(h, w_ref[:, 7 * gw:8 * gw]) * _dot(h, w_ref[:, 5 * gw:6 * gw])
    yc_ref[...] = (_dot(h, w_ref[:, 6 * gw:7 * gw]) * _causal_conv(cbuf, ccw_ref, CONV_C_WIDTH, C_PAD, tm)).astype(BF16)

    u = _dot(h, w_ref[:, 8 * gw:9 * gw])
    pbuf[P_PAD:P_PAD + tm, :] = u
    s1 = pbuf[...]
    s2 = s1 + pltpu.roll(s1, 1, 0)
    s4 = s2 + pltpu.roll(s2, 2, 0)
    s8 = s4 + pltpu.roll(s4, 4, 0)
    s16 = s8 + pltpu.roll(s8, 8, 0)
    win = _pool_select(s2, s4, s8, s16)[P_PAD:P_PAD + tm]
    pos = j * tm + lax.broadcasted_iota(jnp.int32, (tm, gw), 0)
    cnt = jnp.minimum(_pool_window_lanes((tm, gw)), pos + 1).astype(F32)
    yd_ref[...] = (_dot((win / cnt - u).astype(BF16), pw_ref[...]) * ps_ref[...]).astype(BF16)

    @pl.when(j == pl.num_programs(1) - 1)
    def _():
        sta_ref[0] = abuf[A_PAD + tm - (CONV_A_WIDTH - 1):A_PAD + tm, :]
        stc_ref[0] = cbuf[C_PAD + tm - (CONV_C_WIDTH - 1):C_PAD + tm, :]
        stp_ref[0] = pbuf[P_PAD + tm - POOL_STATE:P_PAD + tm, :]

    abuf[0:A_PAD, :] = abuf[tm:tm + A_PAD, :]
    cbuf[0:C_PAD, :] = cbuf[tm:tm + C_PAD, :]
    pbuf[0:P_PAD, :] = pbuf[tm:tm + P_PAD, :]


def _inmix_prompt(x, g, w_in, caw, cab, lng, lnb, ccw, pool_bd, ps, later_weights, layer, n, tm, kt_all, vt_all):
    t, d = x.shape
    gw = GROUP_W
    depth = w_in.shape[0]
    spb = t // n // tm
    row = lambda w: pl.BlockSpec((tm, w), lambda i, j: (i * spb + j, 0))
    full = lambda a: pl.BlockSpec(a.shape, lambda i, j: (0,) * a.ndim)
    st = lambda r: pl.BlockSpec((1, r, gw), lambda i, j: (i, 0, 0))
    st_rows = (CONV_A_WIDTH - 1, CONV_C_WIDTH - 1, POOL_STATE)
    cm = pl.BlockSpec((1, 1, gw, tm), lambda i, j: (layer, i, 0, j))
    small = (caw, cab, lng, lnb, ccw, pool_bd, ps)
    slab = lambda a, l: pl.BlockSpec((1, a.shape[1] // n, a.shape[2]), lambda i, j: (l, i, 0))
    in_specs = ([row(d), full(g), _layer_spec(w_in, layer)] + [full(a) for a in small]
                + [slab(a, layer) for a in later_weights])
    args, aliases = [x, g, w_in, *small, *later_weights], {}
    if kt_all is not None:
        in_specs += [pl.BlockSpec(memory_space=pl.ANY)] * 2
        aliases = {len(args): 9, len(args) + 1: 10}
        args += [kt_all, vt_all]
    return pl.pallas_call(
        functools.partial(_inmix_kernel, n_alias=len(aliases), tm=tm),
        grid=(n, spb),
        in_specs=in_specs,
        out_specs=[pl.BlockSpec((2, tm, gw // 2), lambda i, j: (0, i * spb + j, 0))] * 3 + [row(gw)] * 3
        + [st(r) for r in st_rows] + [cm, cm] + [slab(a, 0) for a in later_weights],
        out_shape=[jax.ShapeDtypeStruct((2, t, gw // 2), F32)] * 3 + [jax.ShapeDtypeStruct((t, gw), BF16)] * 3
        + [jax.ShapeDtypeStruct((n, r, gw), F32) for r in st_rows]
        + [jax.ShapeDtypeStruct((depth, n, gw, t // n), F32)] * 2
        + [jax.ShapeDtypeStruct((1,) + a.shape[1:], BF16) for a in later_weights],
        input_output_aliases=aliases,
        scratch_shapes=[pltpu.VMEM((A_PAD + tm, gw), F32), pltpu.VMEM((C_PAD + tm, gw), F32),
                        pltpu.VMEM((P_PAD + tm, gw), F32)],
        compiler_params=_cparams("parallel", "arbitrary"),
        name="inmix_prompt",
    )(*args)


def _outffn_kernel(x_ref, ya_ref, ob_ref, yc_ref, yd_ref, wo_ref, g2_ref, wgu_ref, wd_ref, gf_ref,
                   qs_ref, kn_ref, vn_ref, kb_ref, vb_ref, *refs, n_alias, ff_chunk, final):
    o_ref, obs_ref, ko_ref, vo_ref = refs[n_alias:]
    seq_row = pl.ds(pl.program_id(0), 1)
    obs_ref[seq_row, :] = _cache_job(qs_ref[seq_row, :], kn_ref[seq_row, :], vn_ref[seq_row, :], kb_ref, vb_ref,
                                     ko_ref, vo_ref)
    x1 = x_ref[...] + _out_proj(ya_ref[...], ob_ref[...], yc_ref[...], yd_ref[...], wo_ref.at[0])
    x2 = _ffn(x1, g2_ref[...], wgu_ref.at[0], wd_ref.at[0], ff_chunk)
    o_ref[...] = _rmsnorm(x2, gf_ref[...]) if final else x2


def _outffn_prompt(x, ya, ob, yc, yd, w_out, g2, w_gu, w_down, gf, qs, kn, vn, kbuf, vbuf, k_all, v_all, layer, tm, final):
    t, d = x.shape
    depth, ns, w, buf = kbuf.shape
    assert t // tm == ns
    row = lambda a: pl.BlockSpec((tm, a.shape[1]), lambda i: (i, 0))
    full = lambda a: pl.BlockSpec(a.shape, lambda i: (0,) * a.ndim)
    tok = full(qs)
    blk = pl.BlockSpec((1, 1, w, buf), lambda i: (layer, i, 0, 0))
    in_specs = ([row(a) for a in (x, ya, ob, yc, yd)]
                + [_layer_spec(w_out, layer), full(g2), _layer_spec(w_gu, layer), _layer_spec(w_down, layer), full(gf)]
                + [tok, tok, tok, blk, blk])
    args, aliases = [x, ya, ob, yc, yd, w_out, g2, w_gu, w_down, gf, qs, kn, vn, kbuf, vbuf], {}
    if k_all is not None:
        in_specs += [pl.BlockSpec(memory_space=pl.ANY)] * 2
        aliases = {len(args): 2, len(args) + 1: 3}
        args += [k_all, v_all]
    return pl.pallas_call(
        functools.partial(_outffn_kernel, n_alias=len(aliases), ff_chunk=256, final=final),
        grid=(t // tm,),
        in_specs=in_specs,
        out_specs=[row(x), tok, blk, blk],
        out_shape=[jax.ShapeDtypeStruct((t, d), F32), jax.ShapeDtypeStruct((ns, w), F32)]
        + [jax.ShapeDtypeStruct(kbuf.shape, F32)] * 2,
        input_output_aliases=aliases,
        compiler_params=_cparams("arbitrary"),
        name="outffn_prompt",
    )(*args)


def _cache_job(q_row, kn_row, vn_row, kb_ref, vb_ref, ko_ref, vo_ref):
    gw = GROUP_W
    buf = kb_ref.shape[-1]
    assert max(win for win, _ in DIL_CONFIGS) <= buf
    ident = lax.broadcasted_iota(jnp.int32, (gw, gw), 0) == lax.broadcasted_iota(jnp.int32, (gw, gw), 1)
    to_col = lambda row: jnp.sum(jnp.where(ident, row, 0.0), axis=1, keepdims=True)
    to_row = lambda col: jnp.sum(jnp.where(ident, col, 0.0), axis=0, keepdims=True)
    qc, kc, vc = to_col(q_row), to_col(kn_row), to_col(vn_row)
    kb, vb = kb_ref[0, 0], vb_ref[0, 0]
    last = lax.broadcasted_iota(jnp.int32, (gw, buf), 1) == buf - 1
    ko_ref[0, 0] = jnp.where(last, kc, pltpu.roll(kb, buf - 1, 1))
    vo_ref[0, 0] = jnp.where(last, vc, pltpu.roll(vb, buf - 1, 1))

    def heads(a):
        return jnp.sum(a.reshape(N_HEADS, HEAD_DIM, a.shape[-1]), axis=1)

    def spread(a):
        return jnp.broadcast_to(a[:, None, :], (N_HEADS, HEAD_DIM, a.shape[-1])).reshape(gw, a.shape[-1])

    s_all = heads(kb * qc)
    s_new = heads(kc * qc)
    outs, lses = [], []
    for win, dil in DIL_CONFIGS:
        lo = buf - win
        s = s_all[:, lo:]
        if dil > 1:
            back = win - lax.broadcasted_iota(jnp.int32, s.shape, 1)
            s = jnp.where(back % dil == 0, s, NEG)
        m = jnp.maximum(jnp.max(s, axis=1, keepdims=True), s_new)
        p, p_new = jnp.exp(s - m), jnp.exp(s_new - m)
        den = jnp.sum(p, axis=1, keepdims=True) + p_new
        pv = jnp.sum(vb[:, lo:] * spread(p), axis=1, keepdims=True) + vc * spread(p_new)
        outs.append(pv / spread(den))
        lses.append(spread(m + jnp.log(den)))
    m = jnp.maximum(jnp.maximum(lses[0], lses[1]), lses[2])
    es = [jnp.exp(l - m) for l in lses]
    return to_row((es[0] * outs[0] + es[1] * outs[1] + es[2] * outs[2]) / (es[0] + es[1] + es[2]))


def _sample_rest_kernel(ag_ref, cd_ref, ob_ref, x_ref, sa_ref, sc_ref, sp_ref, caw_ref, cab_ref, lng_ref, lnb_ref,
                        ccw_ref, pw_ref, ps_ref, wo_ref, g2_ref, wgu_ref, wd_ref, gf_ref,
                        y_ref, na_ref, nc_ref, np_ref, *, pos0, final):
    gw = GROUP_W
    ga = ag_ref[:, 0:gw] * jax.nn.sigmoid(ag_ref[:, gw:2 * gw])
    na = CONV_A_WIDTH - 1
    acc = caw_ref[na:na + 1, :] * ga
    for t in range(na):
        acc += caw_ref[t:t + 1, :] * sa_ref[0, t]
    ya = jax.nn.silu(_layernorm(acc + cab_ref[...], lng_ref[...], lnb_ref[...]))
    na_ref[0:na - 1] = sa_ref[0, 1:na]
    na_ref[na - 1] = ga

    cx = cd_ref[:, 2 * gw:3 * gw] * cd_ref[:, 0:gw]
    nc = CONV_C_WIDTH - 1
    acc = ccw_ref[nc:nc + 1, :] * cx
    for t in range(nc):
        acc += ccw_ref[t:t + 1, :] * sc_ref[0, t]
    yc = cd_ref[:, gw:2 * gw] * acc
    nc_ref[0:nc - 1] = sc_ref[0, 1:nc]
    nc_ref[nc - 1] = cx

    u = cd_ref[:, 3 * gw:4 * gw]
    npl = POOL_STATE
    back = lambda i: sp_ref[0, npl - i]
    s2 = u + back(1)
    s4 = s2 + back(2) + back(3)
    s8 = s4 + back(4) + back(5) + back(6) + back(7)
    s16 = s8
    for i in range(8, 16):
        s16 = s16 + back(i)
    cnt = jnp.minimum(_pool_window_lanes(u.shape), pos0 + 1).astype(F32)
    yd = _dot((_pool_select(s2, s4, s8, s16) / cnt - u).astype(BF16), pw_ref[...]) * ps_ref[...]
    np_ref[0:npl - 1] = sp_ref[0, 1:npl]
    np_ref[npl - 1] = u

    x1 = x_ref[...] + _out_proj(ya, ob_ref[...], yc, yd, wo_ref.at[0])
    x2 = _ffn(x1, g2_ref[...], wgu_ref.at[0], wd_ref.at[0], 256)
    y_ref[...] = _rmsnorm(x2, gf_ref[...]) if final else x2


def _sample_rest(ag, cd, ob, x, sa, sc, sp, caw, cab, lng, lnb, ccw, pool_bd, ps, w_out, g2, w_gu, w_down, gf, layer, pos0, final):
    full = lambda a: pl.BlockSpec(a.shape, lambda i: (0,) * a.ndim)
    st_in = lambda a: pl.BlockSpec((1,) + a.shape[1:], lambda i: (layer, 0, 0, 0))
    st_out = lambda a: pl.BlockSpec(a.shape[1:], lambda i: (0, 0, 0))
    dense = (caw, cab, lng, lnb, ccw, pool_bd, ps, w_out, g2, w_gu, w_down, gf)
    stacked = (w_out, w_gu, w_down)
    dense_spec = lambda a: _layer_spec(a, layer) if any(a is w for w in stacked) else full(a)
    return pl.pallas_call(
        functools.partial(_sample_rest_kernel, pos0=pos0, final=final),
        grid=(1,),
        in_specs=[full(a) for a in (ag, cd, ob, x)] + [st_in(a) for a in (sa, sc, sp)] + [dense_spec(a) for a in dense],
        out_specs=[full(x)] + [st_out(a) for a in (sa, sc, sp)],
        out_shape=[jax.ShapeDtypeStruct(x.shape, F32)] + [jax.ShapeDtypeStruct(a.shape[1:], F32) for a in (sa, sc, sp)],
        compiler_params=_cparams("arbitrary"),
        name="sample_rest",
    )(ag, cd, ob, x, sa, sc, sp, *dense)


def _block_diag(pool_w):
    g, c, e = pool_w.shape
    eye = jnp.eye(g, dtype=pool_w.dtype)
    return (pool_w[:, :, None, :] * eye[:, None, :, None]).reshape(g * c, g * e)


def kernel(x_prompt, x_sample, cache_win_k, cache_win_v, state_conv_a, state_conv_c, state_pool, w_in, conv_a_w, conv_a_b, ln_a_g, ln_a_b, conv_c_w, pool_w, pool_scale, w_out, norm1_g, norm2_g, w_gu, w_down, final_g):
    depth = w_in.shape[0]
    n, s, d = x_prompt.shape
    ns, ts, _ = x_sample.shape
    assert ts == 1
    gw = GROUP_W
    tm = 512
    tm_mix = 1024

    row = lambda a: a.reshape(1, -1)
    to_cm = lambda a: jnp.transpose(a, (0, 1, 3, 4, 2)).reshape(a.shape[0], a.shape[1], gw, a.shape[2])
    from_cm = lambda a: jnp.transpose(a.reshape(a.shape[0], a.shape[1], N_HEADS, HEAD_DIM, a.shape[3]), (0, 1, 4, 2, 3))
    swap = lambda a: jnp.transpose(a, (0, 2, 1, 3))
    kbuf, vbuf = to_cm(cache_win_k), to_cm(cache_win_v)
    sa_all, sc_all, sp_all = swap(state_conv_a), swap(state_conv_c), swap(state_pool)

    xp = x_prompt.reshape(n * s, d)
    xs = x_sample.reshape(ns, d)
    gf = row(final_g)
    kt_p = vt_p = kt_s = vt_s = None
    st_p = [[] for _ in range(3)]
    st_s = [[] for _ in range(3)]
    w_in = w_in.astype(BF16)
    for l in range(depth):
        pool_bd = _block_diag(pool_w[l]).astype(BF16)
        small = (conv_a_w[l], row(conv_a_b[l]), row(ln_a_g[l]), row(ln_a_b[l]), conv_c_w[l], pool_bd, row(pool_scale[l]))
        final = l == depth - 1

        ag, qs, ks, vs, cd = _inproj(xs, row(norm1_g[l]), w_in, l)
        q, k, v, ya, yc, yd, *states, kt_p, vt_p, w_out_l, w_gu_l, w_down_l = _inmix_prompt(
            xp, row(norm1_g[l]), w_in, *small, (w_out, w_gu, w_down), l, n, tm_mix, kt_p, vt_p)
        ob = _attn_prompt(q, k, v, n).reshape(n * s, gw)
        xp, obs, kt_s, vt_s = _outffn_prompt(
            xp, ya, ob, yc, yd, w_out_l, row(norm2_g[l]), w_gu_l, w_down_l, gf,
            qs, ks, vs, kbuf, vbuf, kt_s, vt_s, l, tm, final)
        for lst, a in zip(st_p, states):
            lst.append(a)
        xs, *states = _sample_rest(ag, cd, obs, xs, sa_all, sc_all, sp_all, *small, w_out_l,
                                   row(norm2_g[l]), w_gu_l, w_down_l, gf, l, PAST_LEN, final)
        for lst, a in zip(st_s, states):
            lst.append(a)

    y_prompt = xp.reshape(n, s, d)
    y_sample = xs.reshape(ns, ts, d)
    return (y_prompt, y_sample, from_cm(kt_p), from_cm(vt_p), *[jnp.stack(a, axis=0) for a in st_p],
            from_cm(kt_s), from_cm(vt_s), *[swap(jnp.stack(a, axis=0)) for a in st_s])
```

```python
import functools
import math

import jax
import jax.numpy as jnp
from jax import lax
from jax.experimental import pallas as pl
from jax.experimental.pallas import tpu as pltpu

F32 = jnp.float32
BF16 = jnp.bfloat16

GROUP_W = 256
HEAD_DIM = 64
N_HEADS = GROUP_W // HEAD_DIM
CONV_A_WIDTH = 31
CONV_C_WIDTH = 3
POOL_WINDOWS = (2, 4, 8, 16)
POOL_STATE = max(POOL_WINDOWS) - 1
DIL_CONFIGS = ((128, 1), (512, 4), (2048, 16))
Q_BLOCK = 128
PAST_LEN = 16384
ATTN_SCALE = 1.0 / math.sqrt(HEAD_DIM)
LOG2E = math.log2(math.e)
EPS = 1e-6
NEG = -1e30

VMEM_LIMIT_BYTES = 56 * 1024 * 1024


def _cparams(*sem):
    return pltpu.CompilerParams(dimension_semantics=sem, vmem_limit_bytes=VMEM_LIMIT_BYTES)


def _rmsnorm(x, g):
    return x * lax.rsqrt(jnp.mean(x * x, axis=-1, keepdims=True) + EPS) * g


def _layernorm(x, g, b):
    mu = jnp.mean(x, axis=-1, keepdims=True)
    xc = x - mu
    return xc * lax.rsqrt(jnp.mean(xc * xc, axis=-1, keepdims=True) + EPS) * g + b


def _layer_spec(a, layer):
    layer = min(layer, a.shape[0] - 1)
    return pl.BlockSpec((1,) + a.shape[1:], lambda *_: (layer,) + (0,) * (a.ndim - 1))


def _dot(a, b):
    return jnp.dot(a, b, preferred_element_type=F32)


def _head_of_lane(shape, dim):
    return lax.broadcasted_iota(jnp.int32, shape, dim) // HEAD_DIM


def _inproj_kernel(x_ref, g_ref, w_ref, ag_ref, q_ref, k_ref, v_ref, cd_ref):
    w_ref = w_ref.at[0]
    h = _rmsnorm(x_ref[...], g_ref[...]).astype(BF16)
    gw = GROUP_W
    ag_ref[...] = _dot(h, w_ref[:, 0:2 * gw])
    q_ref[...] = _dot(h, w_ref[:, 2 * gw:3 * gw]) * ATTN_SCALE
    k_ref[...] = _dot(h, w_ref[:, 3 * gw:4 * gw])
    v_ref[...] = _dot(h, w_ref[:, 4 * gw:5 * gw])
    cd_ref[...] = _dot(h, w_ref[:, 5 * gw:9 * gw])


def _inproj(x, g, w_in, layer):
    t, d = x.shape
    gw = GROUP_W
    full = lambda a: pl.BlockSpec(a.shape, lambda i: (0,) * a.ndim)
    widths = (2 * gw, gw, gw, gw, 4 * gw)
    return pl.pallas_call(
        _inproj_kernel,
        grid=(1,),
        in_specs=[full(x), full(g), _layer_spec(w_in, layer)],
        out_specs=[pl.BlockSpec((t, w), lambda i: (0, 0)) for w in widths],
        out_shape=[jax.ShapeDtypeStruct((t, w), F32) for w in widths],
        compiler_params=_cparams("arbitrary"),
        name="inproj",
    )(x, g, w_in)


HEAD_GROUPS = ((0, 1), (2, 3))
MID_DIL = 4
UNITS_PER_STEP = 16


def _attn_kernel(q_scr, k_scr, v_scr, ob_ref, q_mid, k_mid, v_mid, qc, kc, vtc, o_scr, l_scr, *, seq):
    qb = Q_BLOCK
    hw = GROUP_W // 2
    kc[0:qb, :] = jnp.zeros((qb, GROUP_W), BF16)
    vtc[:, 0:qb] = jnp.zeros((GROUP_W, qb), BF16)

    gh = len(HEAD_GROUPS[0])
    head_q = _head_of_lane((qb, GROUP_W), 1)
    kk = lax.broadcasted_iota(jnp.int32, (2 * qb, gh * qb), 0)
    qi = lax.broadcasted_iota(jnp.int32, (2 * qb, gh * qb), 1) % qb
    band = jnp.logical_or(jnp.logical_and(kk < qb, kk >= qi), jnp.logical_and(kk >= qb, kk - qb <= qi))
    causal = (lax.broadcasted_iota(jnp.int32, (qb, gh * qb), 0)
              <= lax.broadcasted_iota(jnp.int32, (qb, gh * qb), 1) % qb)

    for c, (_, dil) in enumerate(DIL_CONFIGS):
        n_blocks = seq // (dil * qb)
        n_steps = seq // qb // UNITS_PER_STEP
        rows = lambda s0, dil=dil: pl.ds(s0, qb, stride=dil) if dil > 1 else pl.ds(s0, qb)

        def block_start(u, dil=dil, n_blocks=n_blocks):
            start = u // n_blocks + (u % n_blocks) * (qb * dil)
            return pl.multiple_of(start, qb) if dil == 1 else start

        two_hop = dil > MID_DIL and dil % MID_DIL == 0
        assert not two_hop or MID_DIL in [d for _, d in DIL_CONFIGS[:c]]

        def stage(u, carry, dil=dil, n_blocks=n_blocks, rows=rows, block_start=block_start, two_hop=two_hop):
            if two_hop:
                r, b = u // n_blocks, u % n_blocks
                start = (r % MID_DIL) * (seq // MID_DIL) + r // MID_DIL + b * (qb * dil // MID_DIL)
                src_rows = pl.ds(start, qb, stride=dil // MID_DIL)
                srcs = (q_mid, k_mid, v_mid)
            else:
                src_rows = rows(block_start(u))
                srcs = (q_scr, k_scr, v_scr)
            q, k, v = (jnp.concatenate([scr[0, src_rows, :], scr[1, src_rows, :]], axis=1) for scr in srcs)
            dst = pl.multiple_of(u * qb, qb)
            qc[pl.ds(dst, qb), :] = q
            kc[pl.ds(dst + qb, qb), :] = k.astype(BF16)
            vtc[:, pl.ds(dst + qb, qb)] = v.T.astype(BF16)
            if dil == MID_DIL:
                for half in range(2):
                    for val, mid in ((q, q_mid), (k, k_mid), (v, v_mid)):
                        mid[half, pl.ds(dst, qb), :] = val[:, half * hw:(half + 1) * hw]
            return carry

        lax.fori_loop(0, seq // qb, stage, 0, unroll=4)

        def unit(u, carry, c=c, n_blocks=n_blocks, rows=rows, block_start=block_start):
            us = [u * UNITS_PER_STEP + i for i in range(UNITS_PER_STEP)]
            w0s = [pl.multiple_of(ui * qb, qb) for ui in us]
            chains = [(i, g) for i in range(UNITS_PER_STEP) for g in range(len(HEAD_GROUPS))]
            ch = gh * HEAD_DIM
            k0, nk = (qb, qb) if n_blocks == 1 else (0, 2 * qb)
            k0s = [pl.multiple_of(w0 + k0, qb) for w0 in w0s]
            scores = []
            for i, g in chains:
                q = qc[pl.ds(w0s[i], qb), :]
                kwin = kc[pl.ds(k0s[i], nk), :]
                qm = jnp.concatenate([jnp.where(head_q == h, q, 0.0) for h in HEAD_GROUPS[g]], axis=0).astype(BF16)
                scores.append(lax.dot_general(kwin, qm, (((1,), (1,)), ((), ())), preferred_element_type=F32))
            probs, dens, lses = [], [], []
            for (i, g), s in zip(chains, scores):
                valid = causal if n_blocks == 1 else jnp.logical_and(band, jnp.logical_or(kk >= qb, us[i] % n_blocks > 0))
                s = jnp.where(valid, s, NEG)
                m = jnp.max(s, axis=0, keepdims=True)
                p = jnp.exp2(s - m)
                den = jnp.sum(p, axis=0, keepdims=True)
                probs.append(p.astype(BF16))
                dens.append(den)
                lses.append((m + jnp.log2(den)) * (1.0 / LOG2E))
            nums = [_dot(vtc[g * ch:(g + 1) * ch, pl.ds(k0s[i], nk)], p) for (i, g), p in zip(chains, probs)]
            for i in range(UNITS_PER_STEP):
                o_t, l_t = [], []
                for g in range(len(HEAD_GROUPS)):
                    n = chains.index((i, g))
                    for j in range(gh):
                        cols = slice(j * qb, (j + 1) * qb)
                        o_t.append(nums[n][j * HEAD_DIM:(j + 1) * HEAD_DIM, cols] / dens[n][:, cols])
                        l_t.append(jnp.broadcast_to(lses[n][:, cols], (HEAD_DIM, qb)))
                o = jnp.concatenate(o_t, axis=0).T
                l = jnp.concatenate(l_t, axis=0).T
                start = block_start(us[i])
                for half in range(2):
                    o_scr[c, half, rows(start), :] = o[:, half * hw:(half + 1) * hw]
                    l_scr[c, half, rows(start), :] = l[:, half * hw:(half + 1) * hw]
            return carry

        lax.fori_loop(0, n_steps, unit, 0)

    for half in range(2):
        l0, l1, l2 = l_scr[0, half], l_scr[1, half], l_scr[2, half]
        m = jnp.maximum(jnp.maximum(l0, l1), l2)
        e0, e1, e2 = jnp.exp(l0 - m), jnp.exp(l1 - m), jnp.exp(l2 - m)
        ob_ref[0, :, half * hw:(half + 1) * hw] = (
            (e0 * o_scr[0, half] + e1 * o_scr[1, half] + e2 * o_scr[2, half]) / (e0 + e1 + e2)).astype(ob_ref.dtype)


def _attn_prompt(q, k, v, n):
    _, t, hw = q.shape
    s, w = t // n, 2 * hw
    halves = pl.BlockSpec((2, s, hw), lambda i: (0, i, 0))
    return pl.pallas_call(
        functools.partial(_attn_kernel, seq=s),
        grid=(n,),
        in_specs=[halves] * 3,
        out_specs=pl.BlockSpec((1, s, w), lambda i: (i, 0, 0)),
        out_shape=jax.ShapeDtypeStruct((n, s, w), BF16),
        scratch_shapes=[pltpu.VMEM((2, s, w // 2), F32)] * 3
        + [pltpu.VMEM((s, w), F32), pltpu.VMEM((Q_BLOCK + s, w), BF16), pltpu.VMEM((w, Q_BLOCK + s), BF16)]
        + [pltpu.VMEM((len(DIL_CONFIGS), 2, s, w // 2), F32)] * 2,
        compiler_params=_cparams("parallel"),
        name="attn_prompt",
    )(q, k, v)


def _pool_select(s2, s4, s8, s16):
    grp = _head_of_lane(s2.shape, s2.ndim - 1)
    return jnp.where(grp == 0, s2, jnp.where(grp == 1, s4, jnp.where(grp == 2, s8, s16)))


def _pool_window_lanes(shape):
    grp = _head_of_lane(shape, len(shape) - 1)
    return jnp.where(grp == 0, 2, jnp.where(grp == 1, 4, jnp.where(grp == 2, 8, 16)))


def _out_proj(ya, ob, yc, yd, w_out_ref):
    gw = GROUP_W
    acc = _dot(ya.astype(BF16), w_out_ref[0:gw, :])
    acc += _dot(ob.astype(BF16), w_out_ref[gw:2 * gw, :])
    acc += _dot(yc.astype(BF16), w_out_ref[2 * gw:3 * gw, :])
    acc += _dot(yd.astype(BF16), w_out_ref[3 * gw:4 * gw, :])
    return acc


def _ffn(x, g2, w_gu_ref, w_down_ref, ff_chunk):
    ff = w_down_ref.shape[0]
    h = _rmsnorm(x, g2).astype(BF16)
    acc = jnp.zeros_like(x)
    for c in range(ff // ff_chunk):
        lo = c * ff_chunk
        g = _dot(h, w_gu_ref[:, lo:lo + ff_chunk])
        u = _dot(h, w_gu_ref[:, ff + lo:ff + lo + ff_chunk])
        acc += _dot((jax.nn.silu(g) * u).astype(BF16), w_down_ref[lo:lo + ff_chunk, :])
    return x + acc


A_PAD = 32
C_PAD = 8
P_PAD = 16
SUBLANES = 8


def _causal_conv(buf, w_ref, width, pad, tm):
    base = pad - SUBLANES
    rows = tm + SUBLANES
    y = None
    for a in range(min(SUBLANES, width)):
        z = None
        for lag in range(a, width, SUBLANES):
            term = w_ref[width - 1 - lag:width - lag, :] * buf[base - (lag - a):base - (lag - a) + rows, :]
            z = term if z is None else z + term
        z = pltpu.roll(z, a, 0) if a else z
        y = z if y is None else y + z
    return y[SUBLANES:SUBLANES + tm]


def _inmix_kernel(x_ref, g_ref, w_ref, caw_ref, cab_ref, lng_ref, lnb_ref, ccw_ref, pw_ref, ps_ref,
                  wo_f32, wgu_f32, wd_f32, *refs, n_alias, tm):
    (q_ref, k_ref, v_ref, ya_ref, yc_ref, yd_ref, sta_ref, stc_ref, stp_ref, kt_ref, vt_ref, wo_b16, wgu_b16, wd_b16,
     abuf, cbuf, pbuf) = refs[n_alias:]
    gw = GROUP_W
    w_ref = w_ref.at[0]
    j = pl.program_id(1)

    @pl.when(j == 0)
    def _():
        abuf[0:A_PAD, :] = jnp.zeros((A_PAD, gw), F32)
        cbuf[0:C_PAD, :] = jnp.zeros((C_PAD, gw), F32)
        pbuf[0:P_PAD, :] = jnp.zeros((P_PAD, gw), F32)
        for src, dst in ((wo_f32, wo_b16), (wgu_f32, wgu_b16), (wd_f32, wd_b16)):
            dst[...] = src[...].astype(BF16)

    h = _rmsnorm(x_ref[...], g_ref[...]).astype(BF16)

    abuf[A_PAD:A_PAD + tm, :] = _dot(h, w_ref[:, 0:gw]) * jax.nn.sigmoid(_dot(h, w_ref[:, gw:2 * gw]))
    acc = _causal_conv(abuf, caw_ref, CONV_A_WIDTH, A_PAD, tm)
    ya_ref[...] = jax.nn.silu(_layernorm(acc + cab_ref[...], lng_ref[...], lnb_ref[...])).astype(BF16)

    q = _dot(h, w_ref[:, 2 * gw:3 * gw]) * (ATTN_SCALE * LOG2E)
    k = _dot(h, w_ref[:, 3 * gw:4 * gw])
    v = _dot(h, w_ref[:, 4 * gw:5 * gw])
    for half in range(2):
        lanes = slice(half * gw // 2, (half + 1) * gw // 2)
        q_ref[half], k_ref[half], v_ref[half] = q[:, lanes], k[:, lanes], v[:, lanes]
    kt_ref[0, 0] = k.T
    vt_ref[0, 0] = v.T

    cbuf[C_PAD:C_PAD + tm, :] = _dot(h, w_ref[:, 7 * gw:8 * gw]) * _dot(h, w_ref[:, 5 * gw:6 * gw])
    yc_ref[...] = (_dot(h, w_ref[:, 6 * gw:7 * gw]) * _causal_conv(cbuf, ccw_ref, CONV_C_WIDTH, C_PAD, tm)).astype(BF16)

    u = _dot(h, w_ref[:, 8 * gw:9 * gw])
    pbuf[P_PAD:P_PAD + tm, :] = u
    s1 = pbuf[...]
    s2 = s1 + pltpu.roll(s1, 1, 0)
    s4 = s2 + pltpu.roll(s2, 2, 0)
    s8 = s4 + pltpu.roll(s4, 4, 0)
    s16 = s8 + pltpu.roll(s8, 8, 0)
    win = _pool_select(s2, s4, s8, s16)[P_PAD:P_PAD + tm]
    pos = j * tm + lax.broadcasted_iota(jnp.int32, (tm, gw), 0)
    cnt = jnp.minimum(_pool_window_lanes((tm, gw)), pos + 1).astype(F32)
    yd_ref[...] = (_dot((win / cnt - u).astype(BF16), pw_ref[...]) * ps_ref[...]).astype(BF16)

    @pl.when(j == pl.num_programs(1) - 1)
    def _():
        sta_ref[0] = abuf[A_PAD + tm - (CONV_A_WIDTH - 1):A_PAD + tm, :]
        stc_ref[0] = cbuf[C_PAD + tm - (CONV_C_WIDTH - 1):C_PAD + tm, :]
        stp_ref[0] = pbuf[P_PAD + tm - POOL_STATE:P_PAD + tm, :]

    abuf[0:A_PAD, :] = abuf[tm:tm + A_PAD, :]
    cbuf[0:C_PAD, :] = cbuf[tm:tm + C_PAD, :]
    pbuf[0:P_PAD, :] = pbuf[tm:tm + P_PAD, :]


def _inmix_prompt(x, g, w_in, caw, cab, lng, lnb, ccw, pool_bd, ps, later_weights, layer, n, tm, kt_all, vt_all):
    t, d = x.shape
    gw = GROUP_W
    depth = w_in.shape[0]
    spb = t // n // tm
    row = lambda w: pl.BlockSpec((tm, w), lambda i, j: (i * spb + j, 0))
    full = lambda a: pl.BlockSpec(a.shape, lambda i, j: (0,) * a.ndim)
    st = lambda r: pl.BlockSpec((1, r, gw), lambda i, j: (i, 0, 0))
    st_rows = (CONV_A_WIDTH - 1, CONV_C_WIDTH - 1, POOL_STATE)
    cm = pl.BlockSpec((1, 1, gw, tm), lambda i, j: (layer, i, 0, j))
    small = (caw, cab, lng, lnb, ccw, pool_bd, ps)
    slab = lambda a, l: pl.BlockSpec((1, a.shape[1] // n, a.shape[2]), lambda i, j: (l, i, 0))
    in_specs = ([row(d), full(g), _layer_spec(w_in, layer)] + [full(a) for a in small]
                + [slab(a, layer) for a in later_weights])
    args, aliases = [x, g, w_in, *small, *later_weights], {}
    if kt_all is not None:
        in_specs += [pl.BlockSpec(memory_space=pl.ANY)] * 2
        aliases = {len(args): 9, len(args) + 1: 10}
        args += [kt_all, vt_all]
    return pl.pallas_call(
        functools.partial(_inmix_kernel, n_alias=len(aliases), tm=tm),
        grid=(n, spb),
        in_specs=in_specs,
        out_specs=[pl.BlockSpec((2, tm, gw // 2), lambda i, j: (0, i * spb + j, 0))] * 3 + [row(gw)] * 3
        + [st(r) for r in st_rows] + [cm, cm] + [slab(a, 0) for a in later_weights],
        out_shape=[jax.ShapeDtypeStruct((2, t, gw // 2), F32)] * 3 + [jax.ShapeDtypeStruct((t, gw), BF16)] * 3
        + [jax.ShapeDtypeStruct((n, r, gw), F32) for r in st_rows]
        + [jax.ShapeDtypeStruct((depth, n, gw, t // n), F32)] * 2
        + [jax.ShapeDtypeStruct((1,) + a.shape[1:], BF16) for a in later_weights],
        input_output_aliases=aliases,
        scratch_shapes=[pltpu.VMEM((A_PAD + tm, gw), F32), pltpu.VMEM((C_PAD + tm, gw), F32),
                        pltpu.VMEM((P_PAD + tm, gw), F32)],
        compiler_params=_cparams("parallel", "arbitrary"),
        name="inmix_prompt",
    )(*args)


def _outffn_kernel(x_ref, ya_ref, ob_ref, yc_ref, yd_ref, wo_ref, g2_ref, wgu_ref, wd_ref, gf_ref,
                   qs_ref, kn_ref, vn_ref, kb_ref, vb_ref, *refs, n_alias, ff_chunk, final):
    o_ref, obs_ref, ko_ref, vo_ref = refs[n_alias:]
    seq_row = pl.ds(pl.program_id(0), 1)
    obs_ref[seq_row, :] = _cache_job(qs_ref[seq_row, :], kn_ref[seq_row, :], vn_ref[seq_row, :], kb_ref, vb_ref,
                                     ko_ref, vo_ref)
    x1 = x_ref[...] + _out_proj(ya_ref[...], ob_ref[...], yc_ref[...], yd_ref[...], wo_ref.at[0])
    x2 = _ffn(x1, g2_ref[...], wgu_ref.at[0], wd_ref.at[0], ff_chunk)
    o_ref[...] = _rmsnorm(x2, gf_ref[...]) if final else x2


def _outffn_prompt(x, ya, ob, yc, yd, w_out, g2, w_gu, w_down, gf, qs, kn, vn, kbuf, vbuf, k_all, v_all, layer, tm, final):
    t, d = x.shape
    depth, ns, w, buf = kbuf.shape
    assert t // tm == ns
    row = lambda a: pl.BlockSpec((tm, a.shape[1]), lambda i: (i, 0))
    full = lambda a: pl.BlockSpec(a.shape, lambda i: (0,) * a.ndim)
    tok = full(qs)
    blk = pl.BlockSpec((1, 1, w, buf), lambda i: (layer, i, 0, 0))
    in_specs = ([row(a) for a in (x, ya, ob, yc, yd)]
                + [_layer_spec(w_out, layer), full(g2), _layer_spec(w_gu, layer), _layer_spec(w_down, layer), full(gf)]
                + [tok, tok, tok, blk, blk])
    args, aliases = [x, ya, ob, yc, yd, w_out, g2, w_gu, w_down, gf, qs, kn, vn, kbuf, vbuf], {}
    if k_all is not None:
        in_specs += [pl.BlockSpec(memory_space=pl.ANY)] * 2
        aliases = {len(args): 2, len(args) + 1: 3}
        args += [k_all, v_all]
    return pl.pallas_call(
        functools.partial(_outffn_kernel, n_alias=len(aliases), ff_chunk=256, final=final),
        grid=(t // tm,),
        in_specs=in_specs,
        out_specs=[row(x), tok, blk, blk],
        out_shape=[jax.ShapeDtypeStruct((t, d), F32), jax.ShapeDtypeStruct((ns, w), F32)]
        + [jax.ShapeDtypeStruct(kbuf.shape, F32)] * 2,
        input_output_aliases=aliases,
        compiler_params=_cparams("arbitrary"),
        name="outffn_prompt",
    )(*args)


def _cache_job(q_row, kn_row, vn_row, kb_ref, vb_ref, ko_ref, vo_ref):
    gw = GROUP_W
    buf = kb_ref.shape[-1]
    assert max(win for win, _ in DIL_CONFIGS) <= buf
    ident = lax.broadcasted_iota(jnp.int32, (gw, gw), 0) == lax.broadcasted_iota(jnp.int32, (gw, gw), 1)
    to_col = lambda row: jnp.sum(jnp.where(ident, row, 0.0), axis=1, keepdims=True)
    to_row = lambda col: jnp.sum(jnp.where(ident, col, 0.0), axis=0, keepdims=True)
    qc, kc, vc = to_col(q_row), to_col(kn_row), to_col(vn_row)
    kb, vb = kb_ref[0, 0], vb_ref[0, 0]
    last = lax.broadcasted_iota(jnp.int32, (gw, buf), 1) == buf - 1
    ko_ref[0, 0] = jnp.where(last, kc, pltpu.roll(kb, buf - 1, 1))
    vo_ref[0, 0] = jnp.where(last, vc, pltpu.roll(vb, buf - 1, 1))

    def heads(a):
        return jnp.sum(a.reshape(N_HEADS, HEAD_DIM, a.shape[-1]), axis=1)

    def spread(a):
        return jnp.broadcast_to(a[:, None, :], (N_HEADS, HEAD_DIM, a.shape[-1])).reshape(gw, a.shape[-1])

    s_all = heads(kb * qc)
    s_new = heads(kc * qc)
    outs, lses = [], []
    for win, dil in DIL_CONFIGS:
        lo = buf - win
        s = s_all[:, lo:]
        if dil > 1:
            back = win - lax.broadcasted_iota(jnp.int32, s.shape, 1)
            s = jnp.where(back % dil == 0, s, NEG)
        m = jnp.maximum(jnp.max(s, axis=1, keepdims=True), s_new)
        p, p_new = jnp.exp(s - m), jnp.exp(s_new - m)
        den = jnp.sum(p, axis=1, keepdims=True) + p_new
        pv = jnp.sum(vb[:, lo:] * spread(p), axis=1, keepdims=True) + vc * spread(p_new)
        outs.append(pv / spread(den))
        lses.append(spread(m + jnp.log(den)))
    m = jnp.maximum(jnp.maximum(lses[0], lses[1]), lses[2])
    es = [jnp.exp(l - m) for l in lses]
    return to_row((es[0] * outs[0] + es[1] * outs[1] + es[2] * outs[2]) / (es[0] + es[1] + es[2]))


def _sample_rest_kernel(ag_ref, cd_ref, ob_ref, x_ref, sa_ref, sc_ref, sp_ref, caw_ref, cab_ref, lng_ref, lnb_ref,
                        ccw_ref, pw_ref, ps_ref, wo_ref, g2_ref, wgu_ref, wd_ref, gf_ref,
                        y_ref, na_ref, nc_ref, np_ref, *, pos0, final):
    gw = GROUP_W
    ga = ag_ref[:, 0:gw] * jax.nn.sigmoid(ag_ref[:, gw:2 * gw])
    na = CONV_A_WIDTH - 1
    acc = caw_ref[na:na + 1, :] * ga
    for t in range(na):
        acc += caw_ref[t:t + 1, :] * sa_ref[0, t]
    ya = jax.nn.silu(_layernorm(acc + cab_ref[...], lng_ref[...], lnb_ref[...]))
    na_ref[0:na - 1] = sa_ref[0, 1:na]
    na_ref[na - 1] = ga

    cx = cd_ref[:, 2 * gw:3 * gw] * cd_ref[:, 0:gw]
    nc = CONV_C_WIDTH - 1
    acc = ccw_ref[nc:nc + 1, :] * cx
    for t in range(nc):
        acc += ccw_ref[t:t + 1, :] * sc_ref[0, t]
    yc = cd_ref[:, gw:2 * gw] * acc
    nc_ref[0:nc - 1] = sc_ref[0, 1:nc]
    nc_ref[nc - 1] = cx

    u = cd_ref[:, 3 * gw:4 * gw]
    npl = POOL_STATE
    back = lambda i: sp_ref[0, npl - i]
    s2 = u + back(1)
    s4 = s2 + back(2) + back(3)
    s8 = s4 + back(4) + back(5) + back(6) + back(7)
    s16 = s8
    for i in range(8, 16):
        s16 = s16 + back(i)
    cnt = jnp.minimum(_pool_window_lanes(u.shape), pos0 + 1).astype(F32)
    yd = _dot((_pool_select(s2, s4, s8, s16) / cnt - u).astype(BF16), pw_ref[...]) * ps_ref[...]
    np_ref[0:npl - 1] = sp_ref[0, 1:npl]
    np_ref[npl - 1] = u

    x1 = x_ref[...] + _out_proj(ya, ob_ref[...], yc, yd, wo_ref.at[0])
    x2 = _ffn(x1, g2_ref[...], wgu_ref.at[0], wd_ref.at[0], 256)
    y_ref[...] = _rmsnorm(x2, gf_ref[...]) if final else x2


def _sample_rest(ag, cd, ob, x, sa, sc, sp, caw, cab, lng, lnb, ccw, pool_bd, ps, w_out, g2, w_gu, w_down, gf, layer, pos0, final):
    full = lambda a: pl.BlockSpec(a.shape, lambda i: (0,) * a.ndim)
    st_in = lambda a: pl.BlockSpec((1,) + a.shape[1:], lambda i: (layer, 0, 0, 0))
    st_out = lambda a: pl.BlockSpec(a.shape[1:], lambda i: (0, 0, 0))
    dense = (caw, cab, lng, lnb, ccw, pool_bd, ps, w_out, g2, w_gu, w_down, gf)
    stacked = (w_out, w_gu, w_down)
    dense_spec = lambda a: _layer_spec(a, layer) if any(a is w for w in stacked) else full(a)
    return pl.pallas_call(
        functools.partial(_sample_rest_kernel, pos0=pos0, final=final),
        grid=(1,),
        in_specs=[full(a) for a in (ag, cd, ob, x)] + [st_in(a) for a in (sa, sc, sp)] + [dense_spec(a) for a in dense],
        out_specs=[full(x)] + [st_out(a) for a in (sa, sc, sp)],
        out_shape=[jax.ShapeDtypeStruct(x.shape, F32)] + [jax.ShapeDtypeStruct(a.shape[1:], F32) for a in (sa, sc, sp)],
        compiler_params=_cparams("arbitrary"),
        name="sample_rest",
    )(ag, cd, ob, x, sa, sc, sp, *dense)


def _block_diag(pool_w):
    g, c, e = pool_w.shape
    eye = jnp.eye(g, dtype=pool_w.dtype)
    return (pool_w[:, :, None, :] * eye[:, None, :, None]).reshape(g * c, g * e)


def kernel(x_prompt, x_sample, cache_win_k, cache_win_v, state_conv_a, state_conv_c, state_pool, w_in, conv_a_w, conv_a_b, ln_a_g, ln_a_b, conv_c_w, pool_w, pool_scale, w_out, norm1_g, norm2_g, w_gu, w_down, final_g):
    depth = w_in.shape[0]
    n, s, d = x_prompt.shape
    ns, ts, _ = x_sample.shape
    assert ts == 1
    gw = GROUP_W
    tm = 512
    tm_mix = 1024

    row = lambda a: a.reshape(1, -1)
    to_cm = lambda a: jnp.transpose(a, (0, 1, 3, 4, 2)).reshape(a.shape[0], a.shape[1], gw, a.shape[2])
    from_cm = lambda a: jnp.transpose(a.reshape(a.shape[0], a.shape[1], N_HEADS, HEAD_DIM, a.shape[3]), (0, 1, 4, 2, 3))
    swap = lambda a: jnp.transpose(a, (0, 2, 1, 3))
    kbuf, vbuf = to_cm(cache_win_k), to_cm(cache_win_v)
    sa_all, sc_all, sp_all = swap(state_conv_a), swap(state_conv_c), swap(state_pool)

    xp = x_prompt.reshape(n * s, d)
    xs = x_sample.reshape(ns, d)
    gf = row(final_g)
    kt_p = vt_p = kt_s = vt_s = None
    st_p = [[] for _ in range(3)]
    st_s = [[] for _ in range(3)]
    w_in = w_in.astype(BF16)
    for l in range(depth):
        pool_bd = _block_diag(pool_w[l]).astype(BF16)
        small = (conv_a_w[l], row(conv_a_b[l]), row(ln_a_g[l]), row(ln_a_b[l]), conv_c_w[l], pool_bd, row(pool_scale[l]))
        final = l == depth - 1

        ag, qs, ks, vs, cd = _inproj(xs, row(norm1_g[l]), w_in, l)
        q, k, v, ya, yc, yd, *states, kt_p, vt_p, w_out_l, w_gu_l, w_down_l = _inmix_prompt(
            xp, row(norm1_g[l]), w_in, *small, (w_out, w_gu, w_down), l, n, tm_mix, kt_p, vt_p)
        ob = _attn_prompt(q, k, v, n).reshape(n * s, gw)
        xp, obs, kt_s, vt_s = _outffn_prompt(
            xp, ya, ob, yc, yd, w_out_l, row(norm2_g[l]), w_gu_l, w_down_l, gf,
            qs, ks, vs, kbuf, vbuf, kt_s, vt_s, l, tm, final)
        for lst, a in zip(st_p, states):
            lst.append(a)
        xs, *states = _sample_rest(ag, cd, obs, xs, sa_all, sc_all, sp_all, *small, w_out_l,
                                   row(norm2_g[l]), w_gu_l, w_down_l, gf, l, PAST_LEN, final)
        for lst, a in zip(st_s, states):
            lst.append(a)

    y_prompt = xp.reshape(n, s, d)
    y_sample = xs.reshape(ns, ts, d)
    return (y_prompt, y_sample, from_cm(kt_p), from_cm(vt_p), *[jnp.stack(a, axis=0) for a in st_p],
            from_cm(kt_s), from_cm(vt_s), *[swap(jnp.stack(a, axis=0)) for a in st_s])
```

```python
import functools
import math

import jax
import jax.numpy as jnp
from jax import lax
from jax.experimental import pallas as pl
from jax.experimental.pallas import tpu as pltpu

F32 = jnp.float32
BF16 = jnp.bfloat16

GROUP_W = 256
HEAD_DIM = 64
N_HEADS = GROUP_W // HEAD_DIM
CONV_A_WIDTH = 31
CONV_C_WIDTH = 3
POOL_WINDOWS = (2, 4, 8, 16)
POOL_STATE = max(POOL_WINDOWS) - 1
DIL_CONFIGS = ((128, 1), (512, 4), (2048, 16))
Q_BLOCK = 128
PAST_LEN = 16384
ATTN_SCALE = 1.0 / math.sqrt(HEAD_DIM)
LOG2E = math.log2(math.e)
EPS = 1e-6
NEG = -1e30

VMEM_LIMIT_BYTES = 56 * 1024 * 1024


def _cparams(*sem):
    return pltpu.CompilerParams(dimension_semantics=sem, vmem_limit_bytes=VMEM_LIMIT_BYTES)


def _rmsnorm(x, g):
    return x * lax.rsqrt(jnp.mean(x * x, axis=-1, keepdims=True) + EPS) * g


def _layernorm(x, g, b):
    mu = jnp.mean(x, axis=-1, keepdims=True)
    xc = x - mu
    return xc * lax.rsqrt(jnp.mean(xc * xc, axis=-1, keepdims=True) + EPS) * g + b


def _layer_spec(a, layer):
    layer = min(layer, a.shape[0] - 1)
    return pl.BlockSpec((1,) + a.shape[1:], lambda *_: (layer,) + (0,) * (a.ndim - 1))


def _dot(a, b):
    return jnp.dot(a, b, preferred_element_type=F32)


def _head_of_lane(shape, dim):
    return lax.broadcasted_iota(jnp.int32, shape, dim) // HEAD_DIM


def _inproj_rows(x, g_ref, w_ref, ag_ref, q_ref, k_ref, v_ref, cd_ref):
    w_ref = w_ref.at[0]
    h = _rmsnorm(x, g_ref[...]).astype(BF16)
    gw = GROUP_W
    ag_ref[...] = _dot(h, w_ref[:, 0:2 * gw])
    q_ref[...] = _dot(h, w_ref[:, 2 * gw:3 * gw]) * ATTN_SCALE
    k_ref[...] = _dot(h, w_ref[:, 3 * gw:4 * gw])
    v_ref[...] = _dot(h, w_ref[:, 4 * gw:5 * gw])
    cd_ref[...] = _dot(h, w_ref[:, 5 * gw:9 * gw])


def _inproj_kernel(x_ref, g_ref, w_ref, *out_refs):
    _inproj_rows(x_ref[...], g_ref, w_ref, *out_refs)


def _inproj(x, g, w_in, layer):
    t, d = x.shape
    gw = GROUP_W
    full = lambda a: pl.BlockSpec(a.shape, lambda i: (0,) * a.ndim)
    widths = (2 * gw, gw, gw, gw, 4 * gw)
    return pl.pallas_call(
        _inproj_kernel,
        grid=(1,),
        in_specs=[full(x), full(g), _layer_spec(w_in, layer)],
        out_specs=[pl.BlockSpec((t, w), lambda i: (0, 0)) for w in widths],
        out_shape=[jax.ShapeDtypeStruct((t, w), F32) for w in widths],
        compiler_params=_cparams("arbitrary"),
        name="inproj",
    )(x, g, w_in)


HEAD_GROUPS = ((0, 1), (2, 3))
MID_DIL = 4
UNITS_PER_STEP = 16


def _attn_kernel(q_scr, k_scr, v_scr, ob_ref, q_mid, k_mid, v_mid, qc, kc, vtc, o_scr, l_scr, *, seq):
    qb = Q_BLOCK
    hw = GROUP_W // 2
    kc[0:qb, :] = jnp.zeros((qb, GROUP_W), BF16)
    vtc[:, 0:qb] = jnp.zeros((GROUP_W, qb), BF16)

    gh = len(HEAD_GROUPS[0])
    head_q = _head_of_lane((qb, GROUP_W), 1)
    kk = lax.broadcasted_iota(jnp.int32, (2 * qb, gh * qb), 0)
    qi = lax.broadcasted_iota(jnp.int32, (2 * qb, gh * qb), 1) % qb
    band = jnp.logical_or(jnp.logical_and(kk < qb, kk >= qi), jnp.logical_and(kk >= qb, kk - qb <= qi))
    causal = (lax.broadcasted_iota(jnp.int32, (qb, gh * qb), 0)
              <= lax.broadcasted_iota(jnp.int32, (qb, gh * qb), 1) % qb)

    for c, (_, dil) in enumerate(DIL_CONFIGS):
        n_blocks = seq // (dil * qb)
        n_steps = seq // qb // UNITS_PER_STEP
        rows = lambda s0, dil=dil: pl.ds(s0, qb, stride=dil) if dil > 1 else pl.ds(s0, qb)

        def block_start(u, dil=dil, n_blocks=n_blocks):
            start = u // n_blocks + (u % n_blocks) * (qb * dil)
            return pl.multiple_of(start, qb) if dil == 1 else start

        two_hop = dil > MID_DIL and dil % MID_DIL == 0
        assert not two_hop or MID_DIL in [d for _, d in DIL_CONFIGS[:c]]

        def stage(u, carry, dil=dil, n_blocks=n_blocks, rows=rows, block_start=block_start, two_hop=two_hop):
            if two_hop:
                r, b = u // n_blocks, u % n_blocks
                start = (r % MID_DIL) * (seq // MID_DIL) + r // MID_DIL + b * (qb * dil // MID_DIL)
                src_rows = pl.ds(start, qb, stride=dil // MID_DIL)
                srcs = (q_mid, k_mid, v_mid)
            else:
                src_rows = rows(block_start(u))
                srcs = (q_scr, k_scr, v_scr)
            q, k, v = (jnp.concatenate([scr[0, src_rows, :], scr[1, src_rows, :]], axis=1) for scr in srcs)
            dst = pl.multiple_of(u * qb, qb)
            qc[pl.ds(dst, qb), :] = q
            kc[pl.ds(dst + qb, qb), :] = k.astype(BF16)
            vtc[:, pl.ds(dst + qb, qb)] = v.T.astype(BF16)
            if dil == MID_DIL:
                for half in range(2):
                    for val, mid in ((q, q_mid), (k, k_mid), (v, v_mid)):
                        mid[half, pl.ds(dst, qb), :] = val[:, half * hw:(half + 1) * hw]
            return carry

        lax.fori_loop(0, seq // qb, stage, 0, unroll=4)

        def unit(u, carry, c=c, n_blocks=n_blocks, rows=rows, block_start=block_start):
            us = [u * UNITS_PER_STEP + i for i in range(UNITS_PER_STEP)]
            w0s = [pl.multiple_of(ui * qb, qb) for ui in us]
            chains = [(i, g) for i in range(UNITS_PER_STEP) for g in range(len(HEAD_GROUPS))]
            ch = gh * HEAD_DIM
            k0, nk = (qb, qb) if n_blocks == 1 else (0, 2 * qb)
            k0s = [pl.multiple_of(w0 + k0, qb) for w0 in w0s]
            scores = []
            for i, g in chains:
                q = qc[pl.ds(w0s[i], qb), :]
                kwin = kc[pl.ds(k0s[i], nk), :]
                qm = jnp.concatenate([jnp.where(head_q == h, q, 0.0) for h in HEAD_GROUPS[g]], axis=0).astype(BF16)
                scores.append(lax.dot_general(kwin, qm, (((1,), (1,)), ((), ())), preferred_element_type=F32))
            probs, dens, lses = [], [], []
            for (i, g), s in zip(chains, scores):
                valid = causal if n_blocks == 1 else jnp.logical_and(band, jnp.logical_or(kk >= qb, us[i] % n_blocks > 0))
                s = jnp.where(valid, s, NEG)
                m = jnp.max(s, axis=0, keepdims=True)
                p = jnp.exp2(s - m)
                den = jnp.sum(p, axis=0, keepdims=True)
                probs.append(p.astype(BF16))
                dens.append(den)
                lses.append((m + jnp.log2(den)) * (1.0 / LOG2E))
            nums = [_dot(vtc[g * ch:(g + 1) * ch, pl.ds(k0s[i], nk)], p) for (i, g), p in zip(chains, probs)]
            for i in range(UNITS_PER_STEP):
                o_t, l_t = [], []
                for g in range(len(HEAD_GROUPS)):
                    n = chains.index((i, g))
                    for j in range(gh):
                        cols = slice(j * qb, (j + 1) * qb)
                        o_t.append(nums[n][j * HEAD_DIM:(j + 1) * HEAD_DIM, cols] / dens[n][:, cols])
                        l_t.append(jnp.broadcast_to(lses[n][:, cols], (HEAD_DIM, qb)))
                o = jnp.concatenate(o_t, axis=0).T
                l = jnp.concatenate(l_t, axis=0).T
                start = block_start(us[i])
                for half in range(2):
                    o_scr[c, half, rows(start), :] = o[:, half * hw:(half + 1) * hw]
                    l_scr[c, half, rows(start), :] = l[:, half * hw:(half + 1) * hw]
            return carry

        lax.fori_loop(0, n_steps, unit, 0)

    for half in range(2):
        l0, l1, l2 = l_scr[0, half], l_scr[1, half], l_scr[2, half]
        m = jnp.maximum(jnp.maximum(l0, l1), l2)
        e0, e1, e2 = jnp.exp(l0 - m), jnp.exp(l1 - m), jnp.exp(l2 - m)
        ob_ref[0, :, half * hw:(half + 1) * hw] = (
            (e0 * o_scr[0, half] + e1 * o_scr[1, half] + e2 * o_scr[2, half]) / (e0 + e1 + e2)).astype(ob_ref.dtype)


def _attn_prompt(q, k, v, n):
    _, t, hw = q.shape
    s, w = t // n, 2 * hw
    halves = pl.BlockSpec((2, s, hw), lambda i: (0, i, 0))
    return pl.pallas_call(
        functools.partial(_attn_kernel, seq=s),
        grid=(n,),
        in_specs=[halves] * 3,
        out_specs=pl.BlockSpec((1, s, w), lambda i: (i, 0, 0)),
        out_shape=jax.ShapeDtypeStruct((n, s, w), BF16),
        scratch_shapes=[pltpu.VMEM((2, s, w // 2), F32)] * 3
        + [pltpu.VMEM((s, w), F32), pltpu.VMEM((Q_BLOCK + s, w), BF16), pltpu.VMEM((w, Q_BLOCK + s), BF16)]
        + [pltpu.VMEM((len(DIL_CONFIGS), 2, s, w // 2), F32)] * 2,
        compiler_params=_cparams("parallel"),
        name="attn_prompt",
    )(q, k, v)


def _pool_select(s2, s4, s8, s16):
    grp = _head_of_lane(s2.shape, s2.ndim - 1)
    return jnp.where(grp == 0, s2, jnp.where(grp == 1, s4, jnp.where(grp == 2, s8, s16)))


def _pool_window_lanes(shape):
    grp = _head_of_lane(shape, len(shape) - 1)
    return jnp.where(grp == 0, 2, jnp.where(grp == 1, 4, jnp.where(grp == 2, 8, 16)))


def _out_proj(ya, ob, yc, yd, w_out_ref):
    gw = GROUP_W
    acc = _dot(ya.astype(BF16), w_out_ref[0:gw, :])
    acc += _dot(ob.astype(BF16), w_out_ref[gw:2 * gw, :])
    acc += _dot(yc.astype(BF16), w_out_ref[2 * gw:3 * gw, :])
    acc += _dot(yd.astype(BF16), w_out_ref[3 * gw:4 * gw, :])
    return acc


def _ffn(x, g2, w_gu_ref, w_down_ref, ff_chunk):
    ff = w_down_ref.shape[0]
    h = _rmsnorm(x, g2).astype(BF16)
    acc = jnp.zeros_like(x)
    for c in range(ff // ff_chunk):
        lo = c * ff_chunk
        g = _dot(h, w_gu_ref[:, lo:lo + ff_chunk])
        u = _dot(h, w_gu_ref[:, ff + lo:ff + lo + ff_chunk])
        acc += _dot((jax.nn.silu(g) * u).astype(BF16), w_down_ref[lo:lo + ff_chunk, :])
    return x + acc


A_PAD = 32
C_PAD = 8
P_PAD = 16
SUBLANES = 8


def _causal_conv(buf, w_ref, width, pad, tm):
    base = pad - SUBLANES
    rows = tm + SUBLANES
    y = None
    for a in range(min(SUBLANES, width)):
        z = None
        for lag in range(a, width, SUBLANES):
            term = w_ref[width - 1 - lag:width - lag, :] * buf[base - (lag - a):base - (lag - a) + rows, :]
            z = term if z is None else z + term
        z = pltpu.roll(z, a, 0) if a else z
        y = z if y is None else y + z
    return y[SUBLANES:SUBLANES + tm]


def _inmix_kernel(x_ref, g_ref, w_ref, caw_ref, cab_ref, lng_ref, lnb_ref, ccw_ref, pw_ref, ps_ref,
                  wo_f32, wgu_f32, wd_f32, *refs, n_alias, tm):
    (q_ref, k_ref, v_ref, ya_ref, yc_ref, yd_ref, sta_ref, stc_ref, stp_ref, kt_ref, vt_ref, wo_b16, wgu_b16, wd_b16,
     abuf, cbuf, pbuf) = refs[n_alias:]
    gw = GROUP_W
    w_ref = w_ref.at[0]
    j = pl.program_id(1)

    @pl.when(j == 0)
    def _():
        abuf[0:A_PAD, :] = jnp.zeros((A_PAD, gw), F32)
        cbuf[0:C_PAD, :] = jnp.zeros((C_PAD, gw), F32)
        pbuf[0:P_PAD, :] = jnp.zeros((P_PAD, gw), F32)
        for src, dst in ((wo_f32, wo_b16), (wgu_f32, wgu_b16), (wd_f32, wd_b16)):
            dst[...] = src[...].astype(BF16)

    h = _rmsnorm(x_ref[...], g_ref[...]).astype(BF16)

    abuf[A_PAD:A_PAD + tm, :] = _dot(h, w_ref[:, 0:gw]) * jax.nn.sigmoid(_dot(h, w_ref[:, gw:2 * gw]))
    acc = _causal_conv(abuf, caw_ref, CONV_A_WIDTH, A_PAD, tm)
    ya_ref[...] = jax.nn.silu(_layernorm(acc + cab_ref[...], lng_ref[...], lnb_ref[...])).astype(BF16)

    q = _dot(h, w_ref[:, 2 * gw:3 * gw]) * (ATTN_SCALE * LOG2E)
    k = _dot(h, w_ref[:, 3 * gw:4 * gw])
    v = _dot(h, w_ref[:, 4 * gw:5 * gw])
    for half in range(2):
        lanes = slice(half * gw // 2, (half + 1) * gw // 2)
        q_ref[half], k_ref[half], v_ref[half] = q[:, lanes], k[:, lanes], v[:, lanes]
    kt_ref[0, 0] = k.T
    vt_ref[0, 0] = v.T

    cbuf[C_PAD:C_PAD + tm, :] = _dot(h, w_ref[:, 7 * gw:8 * gw]) * _dot(h, w_ref[:, 5 * gw:6 * gw])
    yc_ref[...] = (_dot(h, w_ref[:, 6 * gw:7 * gw]) * _causal_conv(cbuf, ccw_ref, CONV_C_WIDTH, C_PAD, tm)).astype(BF16)

    u = _dot(h, w_ref[:, 8 * gw:9 * gw])
    pbuf[P_PAD:P_PAD + tm, :] = u
    s1 = pbuf[...]
    s2 = s1 + pltpu.roll(s1, 1, 0)
    s4 = s2 + pltpu.roll(s2, 2, 0)
    s8 = s4 + pltpu.roll(s4, 4, 0)
    s16 = s8 + pltpu.roll(s8, 8, 0)
    win = _pool_select(s2, s4, s8, s16)[P_PAD:P_PAD + tm]
    pos = j * tm + lax.broadcasted_iota(jnp.int32, (tm, gw), 0)
    cnt = jnp.minimum(_pool_window_lanes((tm, gw)), pos + 1).astype(F32)
    yd_ref[...] = (_dot((win / cnt - u).astype(BF16), pw_ref[...]) * ps_ref[...]).astype(BF16)

    @pl.when(j == pl.num_programs(1) - 1)
    def _():
        sta_ref[0] = abuf[A_PAD + tm - (CONV_A_WIDTH - 1):A_PAD + tm, :]
        stc_ref[0] = cbuf[C_PAD + tm - (CONV_C_WIDTH - 1):C_PAD + tm, :]
        stp_ref[0] = pbuf[P_PAD + tm - POOL_STATE:P_PAD + tm, :]

    abuf[0:A_PAD, :] = abuf[tm:tm + A_PAD, :]
    cbuf[0:C_PAD, :] = cbuf[tm:tm + C_PAD, :]
    pbuf[0:P_PAD, :] = pbuf[tm:tm + P_PAD, :]


def _inmix_prompt(x, g, w_in, caw, cab, lng, lnb, ccw, pool_bd, ps, later_weights, layer, n, tm, kt_all, vt_all):
    t, d = x.shape
    gw = GROUP_W
    depth = w_in.shape[0]
    spb = t // n // tm
    row = lambda w: pl.BlockSpec((tm, w), lambda i, j: (i * spb + j, 0))
    full = lambda a: pl.BlockSpec(a.shape, lambda i, j: (0,) * a.ndim)
    st = lambda r: pl.BlockSpec((1, r, gw), lambda i, j: (i, 0, 0))
    st_rows = (CONV_A_WIDTH - 1, CONV_C_WIDTH - 1, POOL_STATE)
    cm = pl.BlockSpec((1, 1, gw, tm), lambda i, j: (layer, i, 0, j))
    small = (caw, cab, lng, lnb, ccw, pool_bd, ps)
    slab = lambda a, l: pl.BlockSpec((1, a.shape[1] // n, a.shape[2]), lambda i, j: (l, i, 0))
    in_specs = ([row(d), full(g), _layer_spec(w_in, layer)] + [full(a) for a in small]
                + [slab(a, layer) for a in later_weights])
    args, aliases = [x, g, w_in, *small, *later_weights], {}
    if kt_all is not None:
        in_specs += [pl.BlockSpec(memory_space=pl.ANY)] * 2
        aliases = {len(args): 9, len(args) + 1: 10}
        args += [kt_all, vt_all]
    return pl.pallas_call(
        functools.partial(_inmix_kernel, n_alias=len(aliases), tm=tm),
        grid=(n, spb),
        in_specs=in_specs,
        out_specs=[pl.BlockSpec((2, tm, gw // 2), lambda i, j: (0, i * spb + j, 0))] * 3 + [row(gw)] * 3
        + [st(r) for r in st_rows] + [cm, cm] + [slab(a, 0) for a in later_weights],
        out_shape=[jax.ShapeDtypeStruct((2, t, gw // 2), F32)] * 3 + [jax.ShapeDtypeStruct((t, gw), BF16)] * 3
        + [jax.ShapeDtypeStruct((n, r, gw), F32) for r in st_rows]
        + [jax.ShapeDtypeStruct((depth, n, gw, t // n), F32)] * 2
        + [jax.ShapeDtypeStruct((1,) + a.shape[1:], BF16) for a in later_weights],
        input_output_aliases=aliases,
        scratch_shapes=[pltpu.VMEM((A_PAD + tm, gw), F32), pltpu.VMEM((C_PAD + tm, gw), F32),
                        pltpu.VMEM((P_PAD + tm, gw), F32)],
        compiler_params=_cparams("parallel", "arbitrary"),
        name="inmix_prompt",
    )(*args)


def _outffn_kernel(x_ref, ya_ref, ob_ref, yc_ref, yd_ref, wo_ref, g2_ref, wgu_ref, wd_ref, gf_ref,
                   qs_ref, kn_ref, vn_ref, kb_ref, vb_ref, *refs, n_alias, ff_chunk, final):
    o_ref, obs_ref, ko_ref, vo_ref = refs[n_alias:]
    seq_row = pl.ds(pl.program_id(0), 1)
    obs_ref[seq_row, :] = _cache_job(qs_ref[seq_row, :], kn_ref[seq_row, :], vn_ref[seq_row, :], kb_ref, vb_ref,
                                     ko_ref, vo_ref)
    x1 = x_ref[...] + _out_proj(ya_ref[...], ob_ref[...], yc_ref[...], yd_ref[...], wo_ref.at[0])
    x2 = _ffn(x1, g2_ref[...], wgu_ref.at[0], wd_ref.at[0], ff_chunk)
    o_ref[...] = _rmsnorm(x2, gf_ref[...]) if final else x2


def _outffn_prompt(x, ya, ob, yc, yd, w_out, g2, w_gu, w_down, gf, qs, kn, vn, kbuf, vbuf, k_all, v_all, layer, tm, final):
    t, d = x.shape
    depth, ns, w, buf = kbuf.shape
    assert t // tm == ns
    row = lambda a: pl.BlockSpec((tm, a.shape[1]), lambda i: (i, 0))
    full = lambda a: pl.BlockSpec(a.shape, lambda i: (0,) * a.ndim)
    tok = full(qs)
    blk = pl.BlockSpec((1, 1, w, buf), lambda i: (layer, i, 0, 0))
    in_specs = ([row(a) for a in (x, ya, ob, yc, yd)]
                + [_layer_spec(w_out, layer), full(g2), _layer_spec(w_gu, layer), _layer_spec(w_down, layer), full(gf)]
                + [tok, tok, tok, blk, blk])
    args, aliases = [x, ya, ob, yc, yd, w_out, g2, w_gu, w_down, gf, qs, kn, vn, kbuf, vbuf], {}
    if k_all is not None:
        in_specs += [pl.BlockSpec(memory_space=pl.ANY)] * 2
        aliases = {len(args): 2, len(args) + 1: 3}
        args += [k_all, v_all]
    return pl.pallas_call(
        functools.partial(_outffn_kernel, n_alias=len(aliases), ff_chunk=256, final=final),
        grid=(t // tm,),
        in_specs=in_specs,
        out_specs=[row(x), tok, blk, blk],
        out_shape=[jax.ShapeDtypeStruct((t, d), F32), jax.ShapeDtypeStruct((ns, w), F32)]
        + [jax.ShapeDtypeStruct(kbuf.shape, F32)] * 2,
        input_output_aliases=aliases,
        compiler_params=_cparams("arbitrary"),
        name="outffn_prompt",
    )(*args)


def _cache_job(q_row, kn_row, vn_row, kb_ref, vb_ref, ko_ref, vo_ref):
    gw = GROUP_W
    buf = kb_ref.shape[-1]
    assert max(win for win, _ in DIL_CONFIGS) <= buf
    ident = lax.broadcasted_iota(jnp.int32, (gw, gw), 0) == lax.broadcasted_iota(jnp.int32, (gw, gw), 1)
    to_col = lambda row: jnp.sum(jnp.where(ident, row, 0.0), axis=1, keepdims=True)
    to_row = lambda col: jnp.sum(jnp.where(ident, col, 0.0), axis=0, keepdims=True)
    qc, kc, vc = to_col(q_row), to_col(kn_row), to_col(vn_row)
    kb, vb = kb_ref[0, 0], vb_ref[0, 0]
    last = lax.broadcasted_iota(jnp.int32, (gw, buf), 1) == buf - 1
    ko_ref[0, 0] = jnp.where(last, kc, pltpu.roll(kb, buf - 1, 1))
    vo_ref[0, 0] = jnp.where(last, vc, pltpu.roll(vb, buf - 1, 1))

    def heads(a):
        return jnp.sum(a.reshape(N_HEADS, HEAD_DIM, a.shape[-1]), axis=1)

    def spread(a):
        return jnp.broadcast_to(a[:, None, :], (N_HEADS, HEAD_DIM, a.shape[-1])).reshape(gw, a.shape[-1])

    s_all = heads(kb * qc)
    s_new = heads(kc * qc)
    outs, lses = [], []
    for win, dil in DIL_CONFIGS:
        lo = buf - win
        s = s_all[:, lo:]
        if dil > 1:
            back = win - lax.broadcasted_iota(jnp.int32, s.shape, 1)
            s = jnp.where(back % dil == 0, s, NEG)
        m = jnp.maximum(jnp.max(s, axis=1, keepdims=True), s_new)
        p, p_new = jnp.exp(s - m), jnp.exp(s_new - m)
        den = jnp.sum(p, axis=1, keepdims=True) + p_new
        pv = jnp.sum(vb[:, lo:] * spread(p), axis=1, keepdims=True) + vc * spread(p_new)
        outs.append(pv / spread(den))
        lses.append(spread(m + jnp.log(den)))
    m = jnp.maximum(jnp.maximum(lses[0], lses[1]), lses[2])
    es = [jnp.exp(l - m) for l in lses]
    return to_row((es[0] * outs[0] + es[1] * outs[1] + es[2] * outs[2]) / (es[0] + es[1] + es[2]))


def _sample_rest_kernel(ag_ref, cd_ref, ob_ref, x_ref, sa_ref, sc_ref, sp_ref, caw_ref, cab_ref, lng_ref, lnb_ref,
                        ccw_ref, pw_ref, ps_ref, wo_ref, g2_ref, wgu_ref, wd_ref, gf_ref, *refs, pos0, final):
    if final:
        next_in, (y_ref, na_ref, nc_ref, np_ref), next_out = (), refs, ()
    else:
        next_in, (y_ref, na_ref, nc_ref, np_ref), next_out = refs[:2], refs[2:6], refs[6:]
    gw = GROUP_W
    ga = ag_ref[:, 0:gw] * jax.nn.sigmoid(ag_ref[:, gw:2 * gw])
    na = CONV_A_WIDTH - 1
    acc = caw_ref[na:na + 1, :] * ga
    for t in range(na):
        acc += caw_ref[t:t + 1, :] * sa_ref[0, t]
    ya = jax.nn.silu(_layernorm(acc + cab_ref[...], lng_ref[...], lnb_ref[...]))
    na_ref[0:na - 1] = sa_ref[0, 1:na]
    na_ref[na - 1] = ga

    cx = cd_ref[:, 2 * gw:3 * gw] * cd_ref[:, 0:gw]
    nc = CONV_C_WIDTH - 1
    acc = ccw_ref[nc:nc + 1, :] * cx
    for t in range(nc):
        acc += ccw_ref[t:t + 1, :] * sc_ref[0, t]
    yc = cd_ref[:, gw:2 * gw] * acc
    nc_ref[0:nc - 1] = sc_ref[0, 1:nc]
    nc_ref[nc - 1] = cx

    u = cd_ref[:, 3 * gw:4 * gw]
    npl = POOL_STATE
    back = lambda i: sp_ref[0, npl - i]
    s2 = u + back(1)
    s4 = s2 + back(2) + back(3)
    s8 = s4 + back(4) + back(5) + back(6) + back(7)
    s16 = s8
    for i in range(8, 16):
        s16 = s16 + back(i)
    cnt = jnp.minimum(_pool_window_lanes(u.shape), pos0 + 1).astype(F32)
    yd = _dot((_pool_select(s2, s4, s8, s16) / cnt - u).astype(BF16), pw_ref[...]) * ps_ref[...]
    np_ref[0:npl - 1] = sp_ref[0, 1:npl]
    np_ref[npl - 1] = u

    x1 = x_ref[...] + _out_proj(ya, ob_ref[...], yc, yd, wo_ref.at[0])
    x2 = _ffn(x1, g2_ref[...], wgu_ref.at[0], wd_ref.at[0], 256)
    y_ref[...] = _rmsnorm(x2, gf_ref[...]) if final else x2
    if not final:
        _inproj_rows(x2, *next_in, *next_out)


def _sample_rest(ag, cd, ob, x, sa, sc, sp, caw, cab, lng, lnb, ccw, pool_bd, ps, w_out, g2, w_gu, w_down, gf, layer, pos0, final,
                 g1_next=None, w_in=None):
    full = lambda a: pl.BlockSpec(a.shape, lambda i: (0,) * a.ndim)
    st_in = lambda a: pl.BlockSpec((1,) + a.shape[1:], lambda i: (layer, 0, 0, 0))
    st_out = lambda a: pl.BlockSpec(a.shape[1:], lambda i: (0, 0, 0))
    dense = (caw, cab, lng, lnb, ccw, pool_bd, ps, w_out, g2, w_gu, w_down, gf)
    stacked = (w_out, w_gu, w_down)
    dense_spec = lambda a: _layer_spec(a, layer) if any(a is w for w in stacked) else full(a)
    in_specs = [full(a) for a in (ag, cd, ob, x)] + [st_in(a) for a in (sa, sc, sp)] + [dense_spec(a) for a in dense]
    out_specs = [full(x)] + [st_out(a) for a in (sa, sc, sp)]
    out_shape = [jax.ShapeDtypeStruct(x.shape, F32)] + [jax.ShapeDtypeStruct(a.shape[1:], F32) for a in (sa, sc, sp)]
    args = [ag, cd, ob, x, sa, sc, sp, *dense]
    if not final:
        in_specs += [full(g1_next), _layer_spec(w_in, layer + 1)]
        args += [g1_next, w_in]
        proj = [jax.ShapeDtypeStruct((x.shape[0], w), F32) for w in (ag.shape[1], GROUP_W, GROUP_W, GROUP_W, cd.shape[1])]
        out_specs += [full(a) for a in proj]
        out_shape += proj
    return pl.pallas_call(
        functools.partial(_sample_rest_kernel, pos0=pos0, final=final),
        grid=(1,),
        in_specs=in_specs,
        out_specs=out_specs,
        out_shape=out_shape,
        compiler_params=_cparams("arbitrary"),
        name="sample_rest",
    )(*args)


def _block_diag(pool_w):
    g, c, e = pool_w.shape
    eye = jnp.eye(g, dtype=pool_w.dtype)
    return (pool_w[:, :, None, :] * eye[:, None, :, None]).reshape(g * c, g * e)


def kernel(x_prompt, x_sample, cache_win_k, cache_win_v, state_conv_a, state_conv_c, state_pool, w_in, conv_a_w, conv_a_b, ln_a_g, ln_a_b, conv_c_w, pool_w, pool_scale, w_out, norm1_g, norm2_g, w_gu, w_down, final_g):
    depth = w_in.shape[0]
    n, s, d = x_prompt.shape
    ns, ts, _ = x_sample.shape
    assert ts == 1
    gw = GROUP_W
    tm = 512
    tm_mix = 1024

    row = lambda a: a.reshape(1, -1)
    to_cm = lambda a: jnp.transpose(a, (0, 1, 3, 4, 2)).reshape(a.shape[0], a.shape[1], gw, a.shape[2])
    from_cm = lambda a: jnp.transpose(a.reshape(a.shape[0], a.shape[1], N_HEADS, HEAD_DIM, a.shape[3]), (0, 1, 4, 2, 3))
    swap = lambda a: jnp.transpose(a, (0, 2, 1, 3))
    kbuf, vbuf = to_cm(cache_win_k), to_cm(cache_win_v)
    sa_all, sc_all, sp_all = swap(state_conv_a), swap(state_conv_c), swap(state_pool)

    xp = x_prompt.reshape(n * s, d)
    xs = x_sample.reshape(ns, d)
    gf = row(final_g)
    kt_p = vt_p = kt_s = vt_s = None
    st_p = [[] for _ in range(3)]
    st_s = [[] for _ in range(3)]
    w_in = w_in.astype(BF16)
    for l in range(depth):
        pool_bd = _block_diag(pool_w[l]).astype(BF16)
        small = (conv_a_w[l], row(conv_a_b[l]), row(ln_a_g[l]), row(ln_a_b[l]), conv_c_w[l], pool_bd, row(pool_scale[l]))
        final = l == depth - 1

        if l == 0:
            ag, qs, ks, vs, cd = _inproj(xs, row(norm1_g[l]), w_in, l)
        q, k, v, ya, yc, yd, *states, kt_p, vt_p, w_out_l, w_gu_l, w_down_l = _inmix_prompt(
            xp, row(norm1_g[l]), w_in, *small, (w_out, w_gu, w_down), l, n, tm_mix, kt_p, vt_p)
        ob = _attn_prompt(q, k, v, n).reshape(n * s, gw)
        xp, obs, kt_s, vt_s = _outffn_prompt(
            xp, ya, ob, yc, yd, w_out_l, row(norm2_g[l]), w_gu_l, w_down_l, gf,
            qs, ks, vs, kbuf, vbuf, kt_s, vt_s, l, tm, final)
        for lst, a in zip(st_p, states):
            lst.append(a)
        xs, *states = _sample_rest(ag, cd, obs, xs, sa_all, sc_all, sp_all, *small, w_out_l,
                                   row(norm2_g[l]), w_gu_l, w_down_l, gf, l, PAST_LEN, final,
                                   None if final else row(norm1_g[l + 1]), w_in)
        if not final:
            *states, ag, qs, ks, vs, cd = states
        for lst, a in zip(st_s, states):
            lst.append(a)

    y_prompt = xp.reshape(n, s, d)
    y_sample = xs.reshape(ns, ts, d)
    return (y_prompt, y_sample, from_cm(kt_p), from_cm(vt_p), *[jnp.stack(a, axis=0) for a in st_p],
            from_cm(kt_s), from_cm(vt_s), *[swap(jnp.stack(a, axis=0)) for a in st_s])
```
